```python
import math
import jax, jax.numpy as jnp
from jax import lax
import numpy as np

D_MODEL = 1024
BATCH = 4
SEQ = 8192
DEPTH = 4

META_TOKENS = 16
POOL_WIDTH = D_MODEL
POOL_WINDOWS = (2, 4, 8, 16)
POOL_GROUPS = len(POOL_WINDOWS)
POOL_GROUP_DIM = POOL_WIDTH // POOL_GROUPS
N_HEADS = 16
HEAD_DIM = 64
ATTN_WIDTH = N_HEADS * HEAD_DIM
Q_BLOCK = 128
ATTN_PAD = (-META_TOKENS) % Q_BLOCK
D_FF = int(math.ceil(8 * D_MODEL / 3 / 256) * 256)
RMS_EPS = 1e-6
NEG_INF = -1e30
SPLIT_SIZES = (POOL_WIDTH, ATTN_WIDTH, ATTN_WIDTH, ATTN_WIDTH, N_HEADS, D_MODEL, D_MODEL)
SPLIT_POINTS = tuple(int(v) for v in np.cumsum(SPLIT_SIZES)[:-1])
N_IN = int(sum(SPLIT_SIZES))

kernel_name = "gated_pool_forgetting_attn_hybrid"


def rms_norm(x, g):
    xf = x.astype(jnp.float32)
    y = xf * lax.rsqrt(jnp.mean(xf * xf, axis=-1, keepdims=True) + RMS_EPS)
    return (y * g.astype(jnp.float32)).astype(x.dtype)


def multiscale_pool(u, w_pool, scale):
    B, T, _ = u.shape
    ug = u.reshape(B, T, POOL_GROUPS, POOL_GROUP_DIM)
    c0 = jnp.concatenate(
        [jnp.zeros((B, 1, POOL_GROUPS, POOL_GROUP_DIM), jnp.float32),
         jnp.cumsum(ug.astype(jnp.float32), axis=1)], axis=1)
    pos1 = jnp.arange(1, T + 1)
    outs = []
    for g, w in enumerate(POOL_WINDOWS):
        lag_idx = jnp.maximum(pos1 - w, 0)
        window_sum = c0[:, 1:, g] - jnp.take(c0[:, :, g], lag_idx, axis=1)
        count = jnp.minimum(pos1, w).astype(jnp.float32)[None, :, None]
        diff = (window_sum / count).astype(u.dtype) - ug[:, :, g]
        outs.append(jnp.einsum('btc,cd->btd', diff, w_pool[g]))
    return jnp.concatenate(outs, axis=-1) * scale


def forgetting_attention(q, k, v, log_f):
    B, T, H, Dh = q.shape
    pad4 = ((0, 0), (ATTN_PAD, 0), (0, 0), (0, 0))
    q = jnp.pad(q, pad4)
    k = jnp.pad(k, pad4)
    v = jnp.pad(v, pad4)
    F = jnp.cumsum(jnp.pad(log_f.astype(jnp.float32), ((0, 0), (ATTN_PAD, 0), (0, 0))), axis=1)
    L = T + ATTN_PAD
    n_blocks = L // Q_BLOCK
    key_pos = jnp.arange(L)
    key_valid = key_pos >= ATTN_PAD
    F_k = jnp.transpose(F, (0, 2, 1))[:, :, None, :]
    scale = 1.0 / math.sqrt(Dh)

    def block(i):
        start = i * Q_BLOCK
        qb = lax.dynamic_slice_in_dim(q, start, Q_BLOCK, axis=1)
        Fq = lax.dynamic_slice_in_dim(F, start, Q_BLOCK, axis=1)
        s = jnp.einsum('bqhd,bkhd->bhqk', qb, k, preferred_element_type=jnp.float32) * scale
        s = s + jnp.transpose(Fq, (0, 2, 1))[:, :, :, None] - F_k
        q_pos = start + jnp.arange(Q_BLOCK)
        mask = (key_pos[None, :] <= q_pos[:, None]) & key_valid[None, :]
        s = jnp.where(mask[None, None], s, NEG_INF)
        p = jax.nn.softmax(s, axis=-1)
        return jnp.einsum('bhqk,bkhd->bqhd', p.astype(v.dtype), v)

    out = lax.map(block, jnp.arange(n_blocks))
    out = jnp.transpose(out, (1, 0, 2, 3, 4)).reshape(B, L, H, Dh)
    return out[:, ATTN_PAD:]


def setup_inputs(seed: int = 0) -> dict:
    key = jax.random.key(seed)
    ks = jax.random.split(key, 16)
    f32 = jnp.float32
    nrm = lambda k, shape, s: jax.random.normal(k, shape, f32) * s
    gain = lambda k: 1.0 + 0.05 * jax.random.normal(k, (DEPTH, D_MODEL), f32)
    return {
        "x": jax.random.normal(ks[0], (BATCH, SEQ, D_MODEL), f32),
        "meta_tokens": nrm(ks[1], (META_TOKENS, D_MODEL), 1.0),
        "norm_mix_pre": gain(ks[2]),
        "norm_mix_post": gain(ks[3]),
        "norm_ffn_pre": gain(ks[4]),
        "norm_ffn_post": gain(ks[5]),
        "w_in": nrm(ks[6], (DEPTH, D_MODEL, N_IN), D_MODEL ** -0.5),
        "b_forget": jax.random.uniform(ks[7], (DEPTH, N_HEADS), f32, 1.0, 4.0),
        "w_pool": nrm(ks[8], (DEPTH, POOL_GROUPS, POOL_GROUP_DIM, POOL_GROUP_DIM), POOL_GROUP_DIM ** -0.5),
        "pool_scale": 1.0 + 0.05 * jax.random.normal(ks[9], (DEPTH, D_MODEL), f32),
        "w_out": nrm(ks[10], (DEPTH, D_MODEL, D_MODEL), D_MODEL ** -0.5),
        "w_ffn_gate": nrm(ks[11], (DEPTH, D_MODEL, D_FF), D_MODEL ** -0.5),
        "w_ffn_up": nrm(ks[12], (DEPTH, D_MODEL, D_FF), D_MODEL ** -0.5),
        "w_ffn_down": nrm(ks[13], (DEPTH, D_FF, D_MODEL), D_FF ** -0.5),
    }


def reference(x, meta_tokens, norm_mix_pre, norm_mix_post, norm_ffn_pre, norm_ffn_post,
              w_in, b_forget, w_pool, pool_scale, w_out, w_ffn_gate, w_ffn_up, w_ffn_down):
    B = x.shape[0]
    meta = jnp.broadcast_to(meta_tokens[None].astype(x.dtype), (B, META_TOKENS, D_MODEL))
    h_res = jnp.concatenate([meta, x], axis=1)
    T = h_res.shape[1]
    for l in range(DEPTH):
        h = rms_norm(h_res, norm_mix_pre[l])
        proj = jnp.einsum('btd,dn->btn', h, w_in[l])
        u_pool, q, k, v, f_logit, g_pool, g_attn = jnp.split(proj, SPLIT_POINTS, axis=-1)
        y_pool = multiscale_pool(u_pool, w_pool[l], pool_scale[l])
        log_f = jax.nn.log_sigmoid(f_logit.astype(jnp.float32) + b_forget[l].astype(jnp.float32))
        y_attn = forgetting_attention(
            q.reshape(B, T, N_HEADS, HEAD_DIM), k.reshape(B, T, N_HEADS, HEAD_DIM),
            v.reshape(B, T, N_HEADS, HEAD_DIM), log_f).reshape(B, T, ATTN_WIDTH)
        merged = jax.nn.sigmoid(g_pool) * y_pool + jax.nn.sigmoid(g_attn) * y_attn
        mix_out = jnp.einsum('btd,de->bte', merged, w_out[l])
        h_res = h_res + rms_norm(mix_out, norm_mix_post[l])
        h = rms_norm(h_res, norm_ffn_pre[l])
        ff = jax.nn.silu(jnp.einsum('btd,df->btf', h, w_ffn_gate[l])) * jnp.einsum('btd,df->btf', h, w_ffn_up[l])
        ff_out = jnp.einsum('btf,fd->btd', ff, w_ffn_down[l])
        h_res = h_res + rms_norm(ff_out, norm_ffn_post[l])
    return h_res[:, META_TOKENS:]
```

```python
import functools

import jax
import jax.numpy as jnp
from jax import lax
from jax.experimental import pallas as pl
from jax.experimental.pallas import tpu as pltpu

D_MODEL = 1024
N_HEADS = 16
HEAD_DIM = 64
META_TOKENS = 16
POOL_WINDOWS = (2, 4, 8, 16)
POOL_GROUP_DIM = D_MODEL // len(POOL_WINDOWS)
POOL_HALO = max(POOL_WINDOWS)
D_FF = 2816
FF_CHUNK = 1408
RMS_EPS = 1e-6
MASK_VALUE = -1e30

ROW_TILE = 640
PROJ_ROW_TILE = 1280
N_PROJ_BLOCKS = 6
COL_POOL, COL_Q, COL_K, COL_V, COL_GPOOL, COL_GATTN = range(N_PROJ_BLOCKS)
HEADS_PER_STEP = 2
VMEM_LIMIT_BYTES = 56 * 1024 * 1024

F32 = jnp.float32
BF16 = jnp.bfloat16


def _rms_norm(x, gain):
    return x * lax.rsqrt(jnp.mean(x * x, axis=-1, keepdims=True) + RMS_EPS) * gain


def _params(*semantics):
    return pltpu.CompilerParams(dimension_semantics=semantics, vmem_limit_bytes=VMEM_LIMIT_BYTES)


def _norm_kernel(h_ref, gain_ref, out_ref):
    out_ref[...] = _rms_norm(h_ref[...], gain_ref[...]).astype(out_ref.dtype)


def _norm_call(h_res, gain):
    rows = h_res.shape[0]
    return pl.pallas_call(
        _norm_kernel,
        grid=(rows // ROW_TILE,),
        in_specs=[pl.BlockSpec((ROW_TILE, D_MODEL), lambda i: (i, 0)),
                  pl.BlockSpec((1, D_MODEL), lambda i: (0, 0))],
        out_specs=pl.BlockSpec((ROW_TILE, D_MODEL), lambda i: (i, 0)),
        out_shape=jax.ShapeDtypeStruct((rows, D_MODEL), BF16),
        compiler_params=_params("parallel"),
        name="pre_norm",
    )(h_res, gain)


def _proj_kernel(h_ref, w_ref, out_ref):
    acc = jnp.dot(h_ref[...], w_ref[...], preferred_element_type=F32)
    scale = jnp.where(pl.program_id(1) == COL_Q, HEAD_DIM ** -0.5, 1.0).astype(F32)
    out_ref[...] = (acc * scale).astype(out_ref.dtype)


def _proj_call(h_norm, w_main):
    rows = h_norm.shape[0]
    return pl.pallas_call(
        _proj_kernel,
        grid=(rows // PROJ_ROW_TILE, N_PROJ_BLOCKS),
        in_specs=[pl.BlockSpec((PROJ_ROW_TILE, D_MODEL), lambda i, j: (i, 0)),
                  pl.BlockSpec((D_MODEL, D_MODEL), lambda i, j: (0, j))],
        out_specs=pl.BlockSpec((PROJ_ROW_TILE, D_MODEL), lambda i, j: (i, j)),
        out_shape=jax.ShapeDtypeStruct((rows, N_PROJ_BLOCKS * D_MODEL), BF16),
        compiler_params=_params("parallel", "arbitrary"),
        name="in_proj",
    )(h_norm, w_main)


def _split_bf16(x):
    hi = x.astype(BF16)
    rest = x - hi.astype(F32)
    mid = rest.astype(BF16)
    lo = (rest - mid.astype(F32)).astype(BF16)
    return hi, mid, lo


def _forget_kernel(h_ref, wf_ref, bias_ref, out_ref, carry_ref):
    @pl.when(pl.program_id(1) == 0)
    def _():
        carry_ref[...] = jnp.zeros_like(carry_ref)

    logit = lax.dot_general(wf_ref[...], h_ref[...], (((1,), (1,)), ((), ())),
                            preferred_element_type=F32) + bias_ref[...]
    log_f = jnp.minimum(logit, 0.0) - jnp.log(1.0 + jnp.exp(-jnp.abs(logit)))
    tile = log_f.shape[1]
    src = lax.broadcasted_iota(jnp.int32, (tile, tile), 0)
    dst = lax.broadcasted_iota(jnp.int32, (tile, tile), 1)
    prefix = (src <= dst).astype(BF16)
    total = carry_ref[...]
    for piece in _split_bf16(log_f):
        total = total + jnp.dot(piece, prefix, preferred_element_type=F32)
    out_ref[...] = total
    carry_ref[...] = total[:, tile - 1:tile]


def _forget_call(h_norm3, wf_t, bias):
    batch, length, _ = h_norm3.shape
    return pl.pallas_call(
        _forget_kernel,
        grid=(batch, length // ROW_TILE),
        in_specs=[pl.BlockSpec((None, ROW_TILE, D_MODEL), lambda b, i: (b, i, 0)),
                  pl.BlockSpec((N_HEADS, D_MODEL), lambda b, i: (0, 0)),
                  pl.BlockSpec((N_HEADS, 1), lambda b, i: (0, 0))],
        out_specs=pl.BlockSpec((None, N_HEADS, ROW_TILE), lambda b, i: (b, 0, i)),
        out_shape=jax.ShapeDtypeStruct((batch, N_HEADS, length), F32),
        scratch_shapes=[pltpu.VMEM((N_HEADS, 1), F32)],
        compiler_params=_params("parallel", "arbitrary"),
        name="forget_cumsum",
    )(h_norm3, wf_t, bias)


def _attn_kernel(q_ref, k_ref, v_ref, f_ref, out_ref, m_ref, l_ref, acc_ref):
    qi = pl.program_id(2)
    tile = q_ref.shape[0]
    q_start = pl.multiple_of(qi * tile, 128)
    heads_out = []
    for hh in range(HEADS_PER_STEP):
        lanes = slice(hh * HEAD_DIM, (hh + 1) * HEAD_DIM)
        q_h = q_ref[:, lanes]
        f_start = f_ref[hh, :, pl.ds(q_start, 128)][:, 0:1]
        m_ref[...] = jnp.full_like(m_ref, MASK_VALUE)
        l_ref[...] = jnp.zeros_like(l_ref)
        acc_ref[...] = jnp.zeros_like(acc_ref)

        def key_tile(kj, on_diagonal):
            k_start = pl.multiple_of(kj * tile, 128)
            k_h = k_ref[pl.ds(k_start, tile), lanes]
            v_h = v_ref[pl.ds(k_start, tile), lanes]
            bias = f_start - f_ref[hh, :, pl.ds(k_start, tile)]
            s = lax.dot_general(q_h, k_h, (((1,), (1,)), ((), ())),
                                preferred_element_type=F32) + bias
            if on_diagonal:
                row = lax.broadcasted_iota(jnp.int32, s.shape, 0)
                col = lax.broadcasted_iota(jnp.int32, s.shape, 1)
                s = jnp.where(col <= row, s, MASK_VALUE)
            m_prev = m_ref[...]
            m_new = jnp.maximum(m_prev, jnp.max(s, axis=1, keepdims=True))
            alpha = jnp.exp(m_prev - m_new)
            p = jnp.exp(s - m_new)
            l_ref[...] = alpha * l_ref[...] + jnp.sum(p, axis=1, keepdims=True)
            acc_ref[...] = alpha * acc_ref[...] + jnp.dot(p.astype(BF16), v_h,
                                                          preferred_element_type=F32)
            m_ref[...] = m_new

        def off_diagonal(kj, carry):
            key_tile(kj, False)
            return carry

        lax.fori_loop(0, qi, off_diagonal, 0)
        key_tile(qi, True)
        heads_out.append(acc_ref[...] / l_ref[...])
    out_ref[...] = jnp.concatenate(heads_out, axis=1).astype(out_ref.dtype)


def _attn_call(proj3, f_cum):
    batch, length, _ = proj3.shape
    lanes = HEADS_PER_STEP * HEAD_DIM
    pairs = N_HEADS // HEADS_PER_STEP
    blocks_per_section = D_MODEL // lanes
    return pl.pallas_call(
        _attn_kernel,
        grid=(batch, pairs, length // ROW_TILE),
        in_specs=[
            pl.BlockSpec((None, ROW_TILE, lanes), lambda b, p, i: (b, i, COL_Q * blocks_per_section + p)),
            pl.BlockSpec((None, length, lanes), lambda b, p, i: (b, 0, COL_K * blocks_per_section + p)),
            pl.BlockSpec((None, length, lanes), lambda b, p, i: (b, 0, COL_V * blocks_per_section + p)),
            pl.BlockSpec((None, HEADS_PER_STEP, 1, length), lambda b, p, i: (b, p, 0, 0)),
        ],
        out_specs=pl.BlockSpec((None, ROW_TILE, lanes), lambda b, p, i: (b, i, p)),
        out_shape=jax.ShapeDtypeStruct((batch, length, D_MODEL), BF16),
        scratch_shapes=[pltpu.VMEM((ROW_TILE, 1), F32),
                        pltpu.VMEM((ROW_TILE, 1), F32),
                        pltpu.VMEM((ROW_TILE, HEAD_DIM), F32)],
        compiler_params=_params("parallel", "parallel", "arbitrary"),
        name="forgetting_attention",
    )(proj3, proj3, proj3, f_cum)


def _mix_kernel(u_ref, halo_ref, gp_ref, ga_ref, ya_ref, h_ref, wp_ref, ps_ref, wo_ref,
                gain_post_ref, gain_next_ref, hres_out, hnorm_out):
    li = pl.program_id(1)
    tile = u_ref.shape[0]
    u = u_ref[...].astype(F32)
    halo = jnp.where(li > 0, halo_ref[...].astype(F32), 0.0)
    ext = jnp.concatenate([halo, u], axis=0)
    pos1 = li * tile + lax.broadcasted_iota(jnp.int32, (tile, 1), 0) + 1
    pooled = []
    for g, window in enumerate(POOL_WINDOWS):
        lanes = slice(g * POOL_GROUP_DIM, (g + 1) * POOL_GROUP_DIM)
        run = ext[:, lanes]
        width = 1
        while width < window:
            run = run[width:] + run[:-width]
            width *= 2
        first = POOL_HALO + 1 - window
        window_sum = run[first:first + tile]
        count = jnp.minimum(pos1, window).astype(F32)
        diff = window_sum / count - u[:, lanes]
        pooled.append(jnp.dot(diff.astype(BF16), wp_ref[g], preferred_element_type=F32))
    y_pool = jnp.concatenate(pooled, axis=1) * ps_ref[...]
    merged = (jax.nn.sigmoid(gp_ref[...].astype(F32)) * y_pool
              + jax.nn.sigmoid(ga_ref[...].astype(F32)) * ya_ref[...].astype(F32))
    mix = jnp.dot(merged.astype(BF16), wo_ref[...], preferred_element_type=F32)
    h_new = h_ref[...] + _rms_norm(mix, gain_post_ref[...])
    hres_out[...] = h_new
    hnorm_out[...] = _rms_norm(h_new, gain_next_ref[...]).astype(hnorm_out.dtype)


def _mix_call(proj3, y_attn, h_res3, w_pool, pool_scale, w_out, gain_post, gain_next):
    batch, length, _ = proj3.shape
    halo_blocks = ROW_TILE // POOL_HALO
    row_spec = lambda col: pl.BlockSpec((None, ROW_TILE, D_MODEL), lambda b, i: (b, i, col))
    const = lambda shape: pl.BlockSpec(shape, lambda b, i: (0,) * len(shape))
    return pl.pallas_call(
        _mix_kernel,
        grid=(batch, length // ROW_TILE),
        in_specs=[
            row_spec(COL_POOL),
            pl.BlockSpec((None, POOL_HALO, D_MODEL),
                         lambda b, i: (b, jnp.maximum(i * halo_blocks - 1, 0), COL_POOL)),
            row_spec(COL_GPOOL),
            row_spec(COL_GATTN),
            row_spec(0),
            row_spec(0),
            const((len(POOL_WINDOWS), POOL_GROUP_DIM, POOL_GROUP_DIM)),
            const((1, D_MODEL)),
            const((D_MODEL, D_MODEL)),
            const((1, D_MODEL)),
            const((1, D_MODEL)),
        ],
        out_specs=[row_spec(0), row_spec(0)],
        out_shape=[jax.ShapeDtypeStruct((batch, length, D_MODEL), F32),
                   jax.ShapeDtypeStruct((batch, length, D_MODEL), BF16)],
        compiler_params=_params("parallel", "parallel"),
        name="pool_merge_out",
    )(proj3, proj3, proj3, proj3, y_attn, h_res3, w_pool, pool_scale, w_out, gain_post, gain_next)


def _ffn_kernel(hn_ref, h_ref, wg_ref, wu_ref, wd_ref, gain_post_ref, gain_next_ref,
                hres_out, hnorm_out):
    hn = hn_ref[...]
    ff_out = jnp.zeros(h_ref.shape, F32)
    for c in range(D_FF // FF_CHUNK):
        cols = slice(c * FF_CHUNK, (c + 1) * FF_CHUNK)
        gate = jnp.dot(hn, wg_ref[:, cols], preferred_element_type=F32)
        up = jnp.dot(hn, wu_ref[:, cols], preferred_element_type=F32)
        ff = gate * jax.nn.sigmoid(gate) * up
        ff_out = ff_out + jnp.dot(ff.astype(BF16), wd_ref[cols, :], preferred_element_type=F32)
    h_new = h_ref[...] + _rms_norm(ff_out, gain_post_ref[...])
    hres_out[...] = h_new
    hnorm_out[...] = _rms_norm(h_new, gain_next_ref[...]).astype(hnorm_out.dtype)


def _ffn_call(h_norm, h_res, w_gate, w_up, w_down, gain_post, gain_next):
    rows = h_res.shape[0]
    row_spec = pl.BlockSpec((ROW_TILE, D_MODEL), lambda i: (i, 0))
    resident = lambda shape: pl.BlockSpec(shape, lambda i: (0, 0), pipeline_mode=pl.Buffered(1))
    return pl.pallas_call(
        _ffn_kernel,
        grid=(rows // ROW_TILE,),
        in_specs=[row_spec, row_spec,
                  resident((D_MODEL, D_FF)), resident((D_MODEL, D_FF)), resident((D_FF, D_MODEL)),
                  resident((1, D_MODEL)), resident((1, D_MODEL))],
        out_specs=[row_spec, row_spec],
        out_shape=[jax.ShapeDtypeStruct((rows, D_MODEL), F32),
                   jax.ShapeDtypeStruct((rows, D_MODEL), BF16)],
        compiler_params=_params("parallel"),
        name="swiglu_ffn",
    )(h_norm, h_res, w_gate, w_up, w_down, gain_post, gain_next)


def kernel(x, meta_tokens, norm_mix_pre, norm_mix_post, norm_ffn_pre, norm_ffn_post,
           w_in, b_forget, w_pool, pool_scale, w_out, w_ffn_gate, w_ffn_up, w_ffn_down):
    batch, seq, _ = x.shape
    depth = w_in.shape[0]
    tokens = META_TOKENS + seq
    length = -(-tokens // ROW_TILE) * ROW_TILE
    assert (batch * length) % PROJ_ROW_TILE == 0
    meta = jnp.broadcast_to(meta_tokens[None].astype(x.dtype), (batch, META_TOKENS, D_MODEL))
    tail = jnp.zeros((batch, length - tokens, D_MODEL), x.dtype)
    h_res = jnp.concatenate([meta, x, tail], axis=1).reshape(batch * length, D_MODEL)
    gain = lambda g: g.reshape(1, D_MODEL).astype(F32)
    attn_cols = COL_V * D_MODEL + D_MODEL

    h_norm = _norm_call(h_res, gain(norm_mix_pre[0]))
    for l in range(depth):
        w_main = jnp.concatenate([w_in[l][:, :attn_cols], w_in[l][:, attn_cols + N_HEADS:]],
                                 axis=1).astype(BF16)
        wf_t = w_in[l][:, attn_cols:attn_cols + N_HEADS].T.astype(BF16)
        proj = _proj_call(h_norm, w_main)
        proj3 = proj.reshape(batch, length, N_PROJ_BLOCKS * D_MODEL)
        f_cum = _forget_call(h_norm.reshape(batch, length, D_MODEL), wf_t,
                             b_forget[l].reshape(N_HEADS, 1).astype(F32))
        y_attn = _attn_call(proj3, f_cum.reshape(batch, N_HEADS, 1, length))
        h_res3, h_norm3 = _mix_call(
            proj3, y_attn, h_res.reshape(batch, length, D_MODEL),
            w_pool[l].astype(BF16), gain(pool_scale[l]), w_out[l].astype(BF16),
            gain(norm_mix_post[l]), gain(norm_ffn_pre[l]))
        next_gain = norm_mix_pre[(l + 1) % depth]
        h_res, h_norm = _ffn_call(
            h_norm3.reshape(batch * length, D_MODEL), h_res3.reshape(batch * length, D_MODEL),
            w_ffn_gate[l].astype(BF16), w_ffn_up[l].astype(BF16), w_ffn_down[l].astype(BF16),
            gain(norm_ffn_post[l]), gain(next_gain))
    return h_res.reshape(batch, length, D_MODEL)[:, META_TOKENS:tokens]
```

```python
import math

import jax
import jax.numpy as jnp
from jax import lax
from jax.experimental import pallas as pl
from jax.experimental.pallas import tpu as pltpu

D_MODEL = 1024
N_HEADS = 16
HEAD_DIM = 64
META_TOKENS = 16
POOL_WINDOWS = (2, 4, 8, 16)
POOL_GROUP_DIM = D_MODEL // len(POOL_WINDOWS)
POOL_HALO = max(POOL_WINDOWS)
D_FF = 2816
FF_CHUNK = 1408
RMS_EPS = 1e-6
MASK_VALUE = -1e30
LOG2_E = math.log2(math.e)

LANES = 128
ROW_TILE = 640
PROJ_ROW_TILE = 1280
N_ROW_BLOCKS = 4
COL_POOL, COL_K, COL_GPOOL, COL_GATTN = range(N_ROW_BLOCKS)
N_T_BLOCKS = 2
T_Q, T_V = range(N_T_BLOCKS)
HEADS_PER_STEP = LANES // HEAD_DIM
BIAS_PIECES = 3
ONES_ROWS = 16
VMEM_LIMIT_BYTES = 56 * 1024 * 1024

F32 = jnp.float32
BF16 = jnp.bfloat16


def _rms_norm(x, gain):
    return x * lax.rsqrt(jnp.mean(x * x, axis=-1, keepdims=True) + RMS_EPS) * gain


def _params(*semantics):
    return pltpu.CompilerParams(dimension_semantics=semantics, vmem_limit_bytes=VMEM_LIMIT_BYTES)


def _split_bf16(x):
    hi = x.astype(BF16)
    rest = x - hi.astype(F32)
    mid = rest.astype(BF16)
    lo = (rest - mid.astype(F32)).astype(BF16)
    return hi, mid, lo


def _norm_kernel(h_ref, gain_ref, out_ref):
    out_ref[...] = _rms_norm(h_ref[...], gain_ref[...]).astype(out_ref.dtype)


def _norm_call(h_res, gain):
    rows = h_res.shape[0]
    return pl.pallas_call(
        _norm_kernel,
        grid=(rows // ROW_TILE,),
        in_specs=[pl.BlockSpec((ROW_TILE, D_MODEL), lambda i: (i, 0)),
                  pl.BlockSpec((1, D_MODEL), lambda i: (0, 0))],
        out_specs=pl.BlockSpec((ROW_TILE, D_MODEL), lambda i: (i, 0)),
        out_shape=jax.ShapeDtypeStruct((rows, D_MODEL), BF16),
        compiler_params=_params("parallel"),
        name="pre_norm",
    )(h_res, gain)


def _proj_kernel(h_ref, w_ref, out_ref):
    out_ref[...] = jnp.dot(h_ref[...], w_ref[...], preferred_element_type=F32).astype(out_ref.dtype)


def _proj_call(h_norm, w_rows):
    rows = h_norm.shape[0]
    return pl.pallas_call(
        _proj_kernel,
        grid=(rows // PROJ_ROW_TILE, N_ROW_BLOCKS),
        in_specs=[pl.BlockSpec((PROJ_ROW_TILE, D_MODEL), lambda i, j: (i, 0)),
                  pl.BlockSpec((D_MODEL, D_MODEL), lambda i, j: (0, j))],
        out_specs=pl.BlockSpec((PROJ_ROW_TILE, D_MODEL), lambda i, j: (i, j)),
        out_shape=jax.ShapeDtypeStruct((rows, N_ROW_BLOCKS * D_MODEL), BF16),
        compiler_params=_params("parallel", "arbitrary"),
        name="in_proj",
    )(h_norm, w_rows)


def _proj_t_kernel(h_ref, wt_ref, out_ref):
    acc = lax.dot_general(wt_ref[...], h_ref[...], (((1,), (1,)), ((), ())),
                          preferred_element_type=F32)
    scale = jnp.where(pl.program_id(2) == T_Q, LOG2_E * HEAD_DIM ** -0.5, 1.0).astype(F32)
    out_ref[...] = (acc * scale).astype(out_ref.dtype)


def _proj_t_call(h_norm3, w_t):
    batch, length, _ = h_norm3.shape
    return pl.pallas_call(
        _proj_t_kernel,
        grid=(batch, length // ROW_TILE, N_T_BLOCKS),
        in_specs=[pl.BlockSpec((None, ROW_TILE, D_MODEL), lambda b, i, j: (b, i, 0)),
                  pl.BlockSpec((D_MODEL, D_MODEL), lambda b, i, j: (j, 0))],
        out_specs=pl.BlockSpec((None, D_MODEL, ROW_TILE), lambda b, i, j: (b, j, i)),
        out_shape=jax.ShapeDtypeStruct((batch, N_T_BLOCKS * D_MODEL, length), BF16),
        compiler_params=_params("parallel", "parallel", "arbitrary"),
        name="in_proj_t",
    )(h_norm3, w_t)


def _forget_kernel(h_ref, wf_ref, bias_ref, out_ref, carry_ref):
    @pl.when(pl.program_id(1) == 0)
    def _():
        carry_ref[...] = jnp.zeros_like(carry_ref)

    logit = jnp.dot(h_ref[...], wf_ref[...], preferred_element_type=F32) + bias_ref[...]
    log_f = jnp.minimum(logit, 0.0) - jnp.log(1.0 + jnp.exp(-jnp.abs(logit)))
    tile = log_f.shape[0]
    dst = lax.broadcasted_iota(jnp.int32, (tile, tile), 0)
    src = lax.broadcasted_iota(jnp.int32, (tile, tile), 1)
    prefix = (src <= dst).astype(BF16)
    total = jnp.broadcast_to(carry_ref[...], log_f.shape)
    for piece in _split_bf16(log_f):
        total = total + jnp.dot(prefix, piece, preferred_element_type=F32)
    carry_ref[...] = total[tile - 1:tile, :]
    hi, mid, lo = _split_bf16(-LOG2_E * total)
    lane = lax.broadcasted_iota(jnp.int32, total.shape, 1)
    piece = lane % BIAS_PIECES
    out = jnp.where(piece == 0, hi, jnp.where(piece == 1, mid, lo))
    out_ref[...] = jnp.where(lane < BIAS_PIECES * N_HEADS, out, jnp.zeros_like(out))


def _forget_call(h_norm3, wf_rep, bias_rep):
    batch, length, _ = h_norm3.shape
    return pl.pallas_call(
        _forget_kernel,
        grid=(batch, length // ROW_TILE),
        in_specs=[pl.BlockSpec((None, ROW_TILE, D_MODEL), lambda b, i: (b, i, 0)),
                  pl.BlockSpec((D_MODEL, LANES), lambda b, i: (0, 0)),
                  pl.BlockSpec((1, LANES), lambda b, i: (0, 0))],
        out_specs=pl.BlockSpec((None, ROW_TILE, LANES), lambda b, i: (b, i, 0)),
        out_shape=jax.ShapeDtypeStruct((batch, length, LANES), BF16),
        scratch_shapes=[pltpu.VMEM((1, LANES), F32)],
        compiler_params=_params("parallel", "arbitrary"),
        name="forget_bias",
    )(h_norm3, wf_rep, bias_rep)


def _attn_kernel(qt_ref, k_ref, fb_ref, vt_ref, out_ref, acc_ref):
    pair = pl.program_id(1)
    qi = pl.program_id(2)
    tile = qt_ref.shape[1]
    feat = lax.broadcasted_iota(jnp.int32, (LANES, tile), 0)
    key_row = lax.broadcasted_iota(jnp.int32, (tile, tile), 0)
    query_col = lax.broadcasted_iota(jnp.int32, (tile, tile), 1)
    heads_out = []
    for hh in range(HEADS_PER_STEP):
        head = pair * HEADS_PER_STEP + hh
        in_head = (feat >= hh * HEAD_DIM) & (feat < (hh + 1) * HEAD_DIM)
        q_top = jnp.where(in_head, qt_ref[...], jnp.zeros_like(qt_ref[...]))
        is_piece = (feat >= head * BIAS_PIECES) & (feat < (head + 1) * BIAS_PIECES)
        q_aug = jnp.concatenate([q_top, is_piece.astype(BF16)], axis=0)
        acc_ref[...] = jnp.zeros_like(acc_ref)
        ones = jnp.ones((ONES_ROWS, tile), BF16)

        def key_tile(kj, m_prev, on_diagonal):
            k_start = pl.multiple_of(kj * tile, LANES)
            keys = jnp.concatenate([k_ref[pl.ds(k_start, tile), :],
                                    fb_ref[pl.ds(k_start, tile), :]], axis=1)
            s_t = jnp.dot(keys, q_aug, preferred_element_type=F32)
            if on_diagonal:
                s_t = jnp.where(key_row <= query_col, s_t, MASK_VALUE)
            m_new = jnp.maximum(m_prev, jnp.max(s_t, axis=0, keepdims=True))
            alpha = jnp.exp2(m_prev - m_new)
            p_t = jnp.exp2(s_t - m_new).astype(BF16)
            v_aug = jnp.concatenate(
                [vt_ref[hh * HEAD_DIM:(hh + 1) * HEAD_DIM, pl.ds(k_start, tile)], ones], axis=0)
            acc_ref[...] = alpha * acc_ref[...] + jnp.dot(v_aug, p_t, preferred_element_type=F32)
            return m_new

        m_run = jnp.full((1, tile), MASK_VALUE, F32)
        m_run = lax.fori_loop(0, qi, lambda kj, m: key_tile(kj, m, False), m_run)
        key_tile(qi, m_run, True)
        acc = acc_ref[...]
        heads_out.append(acc[:HEAD_DIM] / acc[HEAD_DIM:HEAD_DIM + 1])
    out_ref[...] = jnp.concatenate(heads_out, axis=0).T.astype(out_ref.dtype)


def _attn_call(proj3, proj_t, key_bias):
    batch, length, _ = proj3.shape
    pairs = N_HEADS // HEADS_PER_STEP
    blocks_per_section = D_MODEL // LANES
    return pl.pallas_call(
        _attn_kernel,
        grid=(batch, pairs, length // ROW_TILE),
        in_specs=[
            pl.BlockSpec((None, LANES, ROW_TILE), lambda b, p, i: (b, T_Q * blocks_per_section + p, i)),
            pl.BlockSpec((None, length, LANES), lambda b, p, i: (b, 0, COL_K * blocks_per_section + p)),
            pl.BlockSpec((None, length, LANES), lambda b, p, i: (b, 0, 0)),
            pl.BlockSpec((None, LANES, length), lambda b, p, i: (b, T_V * blocks_per_section + p, 0)),
        ],
        out_specs=pl.BlockSpec((None, ROW_TILE, LANES), lambda b, p, i: (b, i, p)),
        out_shape=jax.ShapeDtypeStruct((batch, length, D_MODEL), BF16),
        scratch_shapes=[pltpu.VMEM((HEAD_DIM + ONES_ROWS, ROW_TILE), F32)],
        compiler_params=_params("parallel", "parallel", "arbitrary"),
        name="forgetting_attention",
    )(proj_t, proj3, key_bias, proj_t)


def _mix_kernel(u_ref, halo_ref, gp_ref, ga_ref, ya_ref, h_ref, wp_ref, ps_ref, wo_ref,
                gain_post_ref, gain_next_ref, hres_out, hnorm_out):
    li = pl.program_id(1)
    tile = u_ref.shape[0]
    u = u_ref[...].astype(F32)
    halo = jnp.where(li > 0, halo_ref[...].astype(F32), 0.0)
    ext = jnp.concatenate([halo, u], axis=0)
    pos1 = li * tile + lax.broadcasted_iota(jnp.int32, (tile, 1), 0) + 1
    pooled = []
    for g, window in enumerate(POOL_WINDOWS):
        lanes = slice(g * POOL_GROUP_DIM, (g + 1) * POOL_GROUP_DIM)
        run = ext[:, lanes]
        width = 1
        while width < window:
            run = run[width:] + run[:-width]
            width *= 2
        first = POOL_HALO + 1 - window
        window_sum = run[first:first + tile]
        count = jnp.minimum(pos1, window).astype(F32)
        diff = window_sum / count - u[:, lanes]
        pooled.append(jnp.dot(diff.astype(BF16), wp_ref[g], preferred_element_type=F32))
    y_pool = jnp.concatenate(pooled, axis=1) * ps_ref[...]
    merged = (jax.nn.sigmoid(gp_ref[...].astype(F32)) * y_pool
              + jax.nn.sigmoid(ga_ref[...].astype(F32)) * ya_ref[...].astype(F32))
    mix = jnp.dot(merged.astype(BF16), wo_ref[...], preferred_element_type=F32)
    h_new = h_ref[...] + _rms_norm(mix, gain_post_ref[...])
    hres_out[...] = h_new
    hnorm_out[...] = _rms_norm(h_new, gain_next_ref[...]).astype(hnorm_out.dtype)


def _mix_call(proj3, y_attn, h_res3, w_pool, pool_scale, w_out, gain_post, gain_next):
    batch, length, _ = proj3.shape
    halo_blocks = ROW_TILE // POOL_HALO
    row_spec = lambda col: pl.BlockSpec((None, ROW_TILE, D_MODEL), lambda b, i: (b, i, col))
    const = lambda shape: pl.BlockSpec(shape, lambda b, i: (0,) * len(shape))
    return pl.pallas_call(
        _mix_kernel,
        grid=(batch, length // ROW_TILE),
        in_specs=[
            row_spec(COL_POOL),
            pl.BlockSpec((None, POOL_HALO, D_MODEL),
                         lambda b, i: (b, jnp.maximum(i * halo_blocks - 1, 0), COL_POOL)),
            row_spec(COL_GPOOL),
            row_spec(COL_GATTN),
            row_spec(0),
            row_spec(0),
            const((len(POOL_WINDOWS), POOL_GROUP_DIM, POOL_GROUP_DIM)),
            const((1, D_MODEL)),
            const((D_MODEL, D_MODEL)),
            const((1, D_MODEL)),
            const((1, D_MODEL)),
        ],
        out_specs=[row_spec(0), row_spec(0)],
        out_shape=[jax.ShapeDtypeStruct((batch, length, D_MODEL), F32),
                   jax.ShapeDtypeStruct((batch, length, D_MODEL), BF16)],
        compiler_params=_params("parallel", "parallel"),
        name="pool_merge_out",
    )(proj3, proj3, proj3, proj3, y_attn, h_res3, w_pool, pool_scale, w_out, gain_post, gain_next)


def _ffn_kernel(hn_ref, h_ref, wg_ref, wu_ref, wd_ref, gain_post_ref, gain_next_ref,
                hres_out, hnorm_out):
    hn = hn_ref[...]
    ff_out = jnp.zeros(h_ref.shape, F32)
    for c in range(D_FF // FF_CHUNK):
        cols = slice(c * FF_CHUNK, (c + 1) * FF_CHUNK)
        gate = jnp.dot(hn, wg_ref[:, cols], preferred_element_type=F32)
        up = jnp.dot(hn, wu_ref[:, cols], preferred_element_type=F32)
        ff = gate * jax.nn.sigmoid(gate) * up
        ff_out = ff_out + jnp.dot(ff.astype(BF16), wd_ref[cols, :], preferred_element_type=F32)
    h_new = h_ref[...] + _rms_norm(ff_out, gain_post_ref[...])
    hres_out[...] = h_new
    hnorm_out[...] = _rms_norm(h_new, gain_next_ref[...]).astype(hnorm_out.dtype)


def _ffn_call(h_norm, h_res, w_gate, w_up, w_down, gain_post, gain_next):
    rows = h_res.shape[0]
    row_spec = pl.BlockSpec((ROW_TILE, D_MODEL), lambda i: (i, 0))
    resident = lambda shape: pl.BlockSpec(shape, lambda i: (0, 0), pipeline_mode=pl.Buffered(1))
    return pl.pallas_call(
        _ffn_kernel,
        grid=(rows // ROW_TILE,),
        in_specs=[row_spec, row_spec,
                  resident((D_MODEL, D_FF)), resident((D_MODEL, D_FF)), resident((D_FF, D_MODEL)),
                  resident((1, D_MODEL)), resident((1, D_MODEL))],
        out_specs=[row_spec, row_spec],
        out_shape=[jax.ShapeDtypeStruct((rows, D_MODEL), F32),
                   jax.ShapeDtypeStruct((rows, D_MODEL), BF16)],
        compiler_params=_params("parallel"),
        name="swiglu_ffn",
    )(h_norm, h_res, w_gate, w_up, w_down, gain_post, gain_next)


def kernel(x, meta_tokens, norm_mix_pre, norm_mix_post, norm_ffn_pre, norm_ffn_post,
           w_in, b_forget, w_pool, pool_scale, w_out, w_ffn_gate, w_ffn_up, w_ffn_down):
    batch, seq, _ = x.shape
    depth = w_in.shape[0]
    tokens = META_TOKENS + seq
    length = -(-tokens // ROW_TILE) * ROW_TILE
    assert (batch * length) % PROJ_ROW_TILE == 0
    meta = jnp.broadcast_to(meta_tokens[None].astype(x.dtype), (batch, META_TOKENS, D_MODEL))
    tail = jnp.zeros((batch, length - tokens, D_MODEL), x.dtype)
    h_res = jnp.concatenate([meta, x, tail], axis=1).reshape(batch * length, D_MODEL)
    gain = lambda g: g.reshape(1, D_MODEL).astype(F32)
    sec = lambda w, n: w[:, n * D_MODEL:(n + 1) * D_MODEL]
    gates_at = 4 * D_MODEL + N_HEADS
    lane_head = jnp.minimum(jnp.arange(LANES) // BIAS_PIECES, N_HEADS - 1)

    h_norm = _norm_call(h_res, gain(norm_mix_pre[0]))
    for l in range(depth):
        w = w_in[l]
        w_rows = jnp.concatenate([sec(w, 0), sec(w, 2), w[:, gates_at:]], axis=1).astype(BF16)
        w_t = jnp.concatenate([sec(w, 1), sec(w, 3)], axis=1).T.astype(BF16)
        w_forget = w[:, 4 * D_MODEL:gates_at]
        wf_rep = w_forget[:, lane_head].astype(BF16)
        bias_rep = b_forget[l][lane_head].reshape(1, LANES).astype(F32)

        h_norm3 = h_norm.reshape(batch, length, D_MODEL)
        proj3 = _proj_call(h_norm, w_rows).reshape(batch, length, N_ROW_BLOCKS * D_MODEL)
        proj_t = _proj_t_call(h_norm3, w_t)
        key_bias = _forget_call(h_norm3, wf_rep, bias_rep)
        y_attn = _attn_call(proj3, proj_t, key_bias)
        h_res3, h_norm3 = _mix_call(
            proj3, y_attn, h_res.reshape(batch, length, D_MODEL),
            w_pool[l].astype(BF16), gain(pool_scale[l]), w_out[l].astype(BF16),
            gain(norm_mix_post[l]), gain(norm_ffn_pre[l]))
        next_gain = norm_mix_pre[(l + 1) % depth]
        h_res, h_norm = _ffn_call(
            h_norm3.reshape(batch * length, D_MODEL), h_res3.reshape(batch * length, D_MODEL),
            w_ffn_gate[l].astype(BF16), w_ffn_up[l].astype(BF16), w_ffn_down[l].astype(BF16),
            gain(norm_ffn_post[l]), gain(next_gain))
    return h_res.reshape(batch, length, D_MODEL)[:, META_TOKENS:tokens]
```

```python
import math

import jax
import jax.numpy as jnp
from jax import lax
from jax.experimental import pallas as pl
from jax.experimental.pallas import tpu as pltpu

D_MODEL = 1024
N_HEADS = 16
HEAD_DIM = 64
META_TOKENS = 16
POOL_WINDOWS = (2, 4, 8, 16)
POOL_GROUP_DIM = D_MODEL // len(POOL_WINDOWS)
POOL_HALO = max(POOL_WINDOWS)
D_FF = 2816
FF_CHUNK = 1408
RMS_EPS = 1e-6
MASK_VALUE = -1e30
LOG2_E = math.log2(math.e)

LANES = 128
MXU_DIM = 256
ROW_TILE = 3 * MXU_DIM
CHUNK_COLS = 2 * MXU_DIM
EXP_ROWS = 32
PROJ_ROW_TILE = 2 * ROW_TILE
N_ROW_BLOCKS = 4
COL_POOL, COL_K, COL_GPOOL, COL_GATTN = range(N_ROW_BLOCKS)
N_T_BLOCKS = 2
T_Q, T_V = range(N_T_BLOCKS)
HEADS_PER_STEP = LANES // HEAD_DIM
BIAS_PIECES = 3
ONES_ROWS = 16
VMEM_LIMIT_BYTES = 56 * 1024 * 1024

F32 = jnp.float32
BF16 = jnp.bfloat16


def _rms_norm(x, gain):
    return x * lax.rsqrt(jnp.mean(x * x, axis=-1, keepdims=True) + RMS_EPS) * gain


def _params(*semantics):
    return pltpu.CompilerParams(dimension_semantics=semantics, vmem_limit_bytes=VMEM_LIMIT_BYTES)


def _split_bf16(x):
    hi = x.astype(BF16)
    rest = x - hi.astype(F32)
    mid = rest.astype(BF16)
    lo = (rest - mid.astype(F32)).astype(BF16)
    return hi, mid, lo


def _norm_kernel(h_ref, gain_ref, out_ref):
    out_ref[...] = _rms_norm(h_ref[...], gain_ref[...]).astype(out_ref.dtype)


def _norm_call(h_res, gain):
    rows = h_res.shape[0]
    return pl.pallas_call(
        _norm_kernel,
        grid=(rows // ROW_TILE,),
        in_specs=[pl.BlockSpec((ROW_TILE, D_MODEL), lambda i: (i, 0)),
                  pl.BlockSpec((1, D_MODEL), lambda i: (0, 0))],
        out_specs=pl.BlockSpec((ROW_TILE, D_MODEL), lambda i: (i, 0)),
        out_shape=jax.ShapeDtypeStruct((rows, D_MODEL), BF16),
        compiler_params=_params("parallel"),
        name="pre_norm",
    )(h_res, gain)


def _proj_kernel(h_ref, w_ref, out_ref):
    out_ref[...] = jnp.dot(h_ref[...], w_ref[...], preferred_element_type=F32).astype(out_ref.dtype)


def _proj_call(h_norm, w_rows):
    rows = h_norm.shape[0]
    return pl.pallas_call(
        _proj_kernel,
        grid=(rows // PROJ_ROW_TILE, N_ROW_BLOCKS),
        in_specs=[pl.BlockSpec((PROJ_ROW_TILE, D_MODEL), lambda i, j: (i, 0)),
                  pl.BlockSpec((D_MODEL, D_MODEL), lambda i, j: (0, j))],
        out_specs=pl.BlockSpec((PROJ_ROW_TILE, D_MODEL), lambda i, j: (i, j)),
        out_shape=jax.ShapeDtypeStruct((rows, N_ROW_BLOCKS * D_MODEL), BF16),
        compiler_params=_params("parallel", "arbitrary"),
        name="in_proj",
    )(h_norm, w_rows)


def _proj_t_kernel(h_ref, wt_ref, out_ref):
    acc = lax.dot_general(wt_ref[...], h_ref[...], (((1,), (1,)), ((), ())),
                          preferred_element_type=F32)
    scale = jnp.where(pl.program_id(2) == T_Q, LOG2_E * HEAD_DIM ** -0.5, 1.0).astype(F32)
    out_ref[...] = (acc * scale).astype(out_ref.dtype)


def _proj_t_call(h_norm3, w_t):
    batch, length, _ = h_norm3.shape
    return pl.pallas_call(
        _proj_t_kernel,
        grid=(batch, length // ROW_TILE, N_T_BLOCKS),
        in_specs=[pl.BlockSpec((None, ROW_TILE, D_MODEL), lambda b, i, j: (b, i, 0)),
                  pl.BlockSpec((D_MODEL, D_MODEL), lambda b, i, j: (j, 0))],
        out_specs=pl.BlockSpec((None, D_MODEL, ROW_TILE), lambda b, i, j: (b, j, i)),
        out_shape=jax.ShapeDtypeStruct((batch, N_T_BLOCKS * D_MODEL, length), BF16),
        compiler_params=_params("parallel", "parallel", "arbitrary"),
        name="in_proj_t",
    )(h_norm3, w_t)


def _forget_kernel(h_ref, wf_ref, bias_ref, out_ref, carry_ref):
    @pl.when(pl.program_id(1) == 0)
    def _():
        carry_ref[...] = jnp.zeros_like(carry_ref)

    logit = jnp.dot(h_ref[...], wf_ref[...], preferred_element_type=F32) + bias_ref[...]
    log_f = jnp.minimum(logit, 0.0) - jnp.log(1.0 + jnp.exp(-jnp.abs(logit)))
    tile = log_f.shape[0]
    dst = lax.broadcasted_iota(jnp.int32, (tile, tile), 0)
    src = lax.broadcasted_iota(jnp.int32, (tile, tile), 1)
    prefix = (src <= dst).astype(BF16)
    total = jnp.broadcast_to(carry_ref[...], log_f.shape)
    for piece in _split_bf16(log_f):
        total = total + jnp.dot(prefix, piece, preferred_element_type=F32)
    carry_ref[...] = total[tile - 1:tile, :]
    hi, mid, lo = _split_bf16(-LOG2_E * total)
    lane = lax.broadcasted_iota(jnp.int32, total.shape, 1)
    piece = lane % BIAS_PIECES
    out = jnp.where(piece == 0, hi, jnp.where(piece == 1, mid, lo))
    out_ref[...] = jnp.where(lane < BIAS_PIECES * N_HEADS, out, jnp.zeros_like(out))


def _forget_call(h_norm3, wf_rep, bias_rep):
    batch, length, _ = h_norm3.shape
    return pl.pallas_call(
        _forget_kernel,
        grid=(batch, length // ROW_TILE),
        in_specs=[pl.BlockSpec((None, ROW_TILE, D_MODEL), lambda b, i: (b, i, 0)),
                  pl.BlockSpec((D_MODEL, LANES), lambda b, i: (0, 0)),
                  pl.BlockSpec((1, LANES), lambda b, i: (0, 0))],
        out_specs=pl.BlockSpec((None, ROW_TILE, LANES), lambda b, i: (b, i, 0)),
        out_shape=jax.ShapeDtypeStruct((batch, length, LANES), BF16),
        scratch_shapes=[pltpu.VMEM((1, LANES), F32)],
        compiler_params=_params("parallel", "arbitrary"),
        name="forget_bias",
    )(h_norm3, wf_rep, bias_rep)


def _attn_kernel(qt_ref, k_ref, fb_ref, vt_ref, out_ref, acc_ref, qaug_ref, s_ref, p_ref):
    pair = pl.program_id(1)
    qi = pl.program_id(2)
    tile = qt_ref.shape[1]
    feat = lax.broadcasted_iota(jnp.int32, (LANES, tile), 0)
    for hh in range(HEADS_PER_STEP):
        head = pair * HEADS_PER_STEP + hh
        in_head = (feat >= hh * HEAD_DIM) & (feat < (hh + 1) * HEAD_DIM)
        q_top = jnp.where(in_head, qt_ref[...], jnp.zeros_like(qt_ref[...]))
        is_piece = (feat >= head * BIAS_PIECES) & (feat < (head + 1) * BIAS_PIECES)
        qaug_ref[:, hh * tile:(hh + 1) * tile] = jnp.concatenate(
            [q_top, is_piece.astype(BF16)], axis=0)
    acc_ref[...] = jnp.zeros_like(acc_ref)

    n_chunks = HEADS_PER_STEP * tile // CHUNK_COLS

    def head_segments(c):
        segments = []
        start = c * CHUNK_COLS
        while start < (c + 1) * CHUNK_COLS:
            hh = start // tile
            stop = min((c + 1) * CHUNK_COLS, (hh + 1) * tile)
            segments.append((hh, slice(start - hh * tile, stop - hh * tile),
                             slice(start - c * CHUNK_COLS, stop - c * CHUNK_COLS)))
            start = stop
        return segments

    def scores_stage(c, k_start, on_diagonal):
        keys = jnp.concatenate([k_ref[pl.ds(k_start, tile), :],
                                fb_ref[pl.ds(k_start, tile), :]], axis=1)
        s_t = jnp.dot(keys, qaug_ref[:, c * CHUNK_COLS:(c + 1) * CHUNK_COLS],
                      preferred_element_type=F32)
        if on_diagonal:
            key_row = lax.broadcasted_iota(jnp.int32, s_t.shape, 0)
            col = lax.broadcasted_iota(jnp.int32, s_t.shape, 1) + c * CHUNK_COLS
            query_col = jnp.where(col >= tile, col - tile, col)
            s_t = jnp.where(key_row <= query_col, s_t, MASK_VALUE)
        s_ref[c] = s_t
        return jnp.max(s_t, axis=0, keepdims=True)

    def exp_stage(c, m_prev, m_tile):
        m_new = jnp.maximum(m_prev, m_tile)
        for r in range(0, tile, EXP_ROWS):
            rows = slice(r, r + EXP_ROWS)
            p_ref[c, rows, :] = jnp.exp2(s_ref[c, rows, :] - m_new).astype(BF16)
        return m_new, jnp.exp2(m_prev - m_new)

    def value_stage(c, k_start, alpha):
        for hh, head_cols, chunk_cols in head_segments(c):
            v_aug = jnp.concatenate(
                [vt_ref[hh * HEAD_DIM:(hh + 1) * HEAD_DIM, pl.ds(k_start, tile)],
                 jnp.ones((ONES_ROWS, tile), BF16)], axis=0)
            acc_ref[hh, :, head_cols] = (
                alpha[:, chunk_cols] * acc_ref[hh, :, head_cols]
                + jnp.dot(v_aug, p_ref[c, :, chunk_cols], preferred_element_type=F32))

    def key_tile(kj, m_all, on_diagonal):
        k_start = pl.multiple_of(kj * tile, LANES)
        m_all = list(m_all)
        m_tile = [None] * n_chunks
        alpha = [None] * n_chunks
        for step in range(n_chunks + 2):
            if step < n_chunks:
                m_tile[step] = scores_stage(step, k_start, on_diagonal)
            if 1 <= step <= n_chunks:
                c = step - 1
                m_all[c], alpha[c] = exp_stage(c, m_all[c], m_tile[c])
            if step >= 2:
                value_stage(step - 2, k_start, alpha[step - 2])
        return tuple(m_all)

    m_all = tuple(jnp.full((1, CHUNK_COLS), MASK_VALUE, F32) for _ in range(n_chunks))
    m_all = lax.fori_loop(0, qi, lambda kj, m: key_tile(kj, m, False), m_all)
    key_tile(qi, m_all, True)
    heads_out = [acc_ref[hh, :HEAD_DIM, :] / acc_ref[hh, HEAD_DIM:HEAD_DIM + 1, :]
                 for hh in range(HEADS_PER_STEP)]
    out_ref[...] = jnp.concatenate(heads_out, axis=0).T.astype(out_ref.dtype)


def _attn_call(proj3, proj_t, key_bias):
    batch, length, _ = proj3.shape
    pairs = N_HEADS // HEADS_PER_STEP
    blocks_per_section = D_MODEL // LANES
    return pl.pallas_call(
        _attn_kernel,
        grid=(batch, pairs, length // ROW_TILE),
        in_specs=[
            pl.BlockSpec((None, LANES, ROW_TILE), lambda b, p, i: (b, T_Q * blocks_per_section + p, i)),
            pl.BlockSpec((None, length, LANES), lambda b, p, i: (b, 0, COL_K * blocks_per_section + p)),
            pl.BlockSpec((None, length, LANES), lambda b, p, i: (b, 0, 0)),
            pl.BlockSpec((None, LANES, length), lambda b, p, i: (b, T_V * blocks_per_section + p, 0)),
        ],
        out_specs=pl.BlockSpec((None, ROW_TILE, LANES), lambda b, p, i: (b, i, p)),
        out_shape=jax.ShapeDtypeStruct((batch, length, D_MODEL), BF16),
        scratch_shapes=[
            pltpu.VMEM((HEADS_PER_STEP, HEAD_DIM + ONES_ROWS, ROW_TILE), F32),
            pltpu.VMEM((2 * LANES, HEADS_PER_STEP * ROW_TILE), BF16),
            pltpu.VMEM((HEADS_PER_STEP * ROW_TILE // CHUNK_COLS, ROW_TILE, CHUNK_COLS), F32),
            pltpu.VMEM((HEADS_PER_STEP * ROW_TILE // CHUNK_COLS, ROW_TILE, CHUNK_COLS), BF16),
        ],
        compiler_params=_params("parallel", "parallel", "arbitrary"),
        name="forgetting_attention",
    )(proj_t, proj3, key_bias, proj_t)


def _mix_kernel(u_ref, halo_ref, gp_ref, ga_ref, ya_ref, h_ref, wp_ref, ps_ref, wo_ref,
                gain_post_ref, gain_next_ref, hres_out, hnorm_out):
    li = pl.program_id(1)
    tile = u_ref.shape[0]
    u = u_ref[...].astype(F32)
    halo = jnp.where(li > 0, halo_ref[...].astype(F32), 0.0)
    ext = jnp.concatenate([halo, u], axis=0)
    pos1 = li * tile + lax.broadcasted_iota(jnp.int32, (tile, 1), 0) + 1
    pooled = []
    for g, window in enumerate(POOL_WINDOWS):
        lanes = slice(g * POOL_GROUP_DIM, (g + 1) * POOL_GROUP_DIM)
        run = ext[:, lanes]
        width = 1
        while width < window:
            run = run[width:] + run[:-width]
            width *= 2
        first = POOL_HALO + 1 - window
        window_sum = run[first:first + tile]
        count = jnp.minimum(pos1, window).astype(F32)
        diff = window_sum / count - u[:, lanes]
        pooled.append(jnp.dot(diff.astype(BF16), wp_ref[g], preferred_element_type=F32))
    y_pool = jnp.concatenate(pooled, axis=1) * ps_ref[...]
    merged = (jax.nn.sigmoid(gp_ref[...].astype(F32)) * y_pool
              + jax.nn.sigmoid(ga_ref[...].astype(F32)) * ya_ref[...].astype(F32))
    mix = jnp.dot(merged.astype(BF16), wo_ref[...], preferred_element_type=F32)
    h_new = h_ref[...] + _rms_norm(mix, gain_post_ref[...])
    hres_out[...] = h_new
    hnorm_out[...] = _rms_norm(h_new, gain_next_ref[...]).astype(hnorm_out.dtype)


def _mix_call(proj3, y_attn, h_res3, w_pool, pool_scale, w_out, gain_post, gain_next):
    batch, length, _ = proj3.shape
    halo_blocks = ROW_TILE // POOL_HALO
    row_spec = lambda col: pl.BlockSpec((None, ROW_TILE, D_MODEL), lambda b, i: (b, i, col))
    const = lambda shape: pl.BlockSpec(shape, lambda b, i: (0,) * len(shape))
    return pl.pallas_call(
        _mix_kernel,
        grid=(batch, length // ROW_TILE),
        in_specs=[
            row_spec(COL_POOL),
            pl.BlockSpec((None, POOL_HALO, D_MODEL),
                         lambda b, i: (b, jnp.maximum(i * halo_blocks - 1, 0), COL_POOL)),
            row_spec(COL_GPOOL),
            row_spec(COL_GATTN),
            row_spec(0),
            row_spec(0),
            const((len(POOL_WINDOWS), POOL_GROUP_DIM, POOL_GROUP_DIM)),
            const((1, D_MODEL)),
            const((D_MODEL, D_MODEL)),
            const((1, D_MODEL)),
            const((1, D_MODEL)),
        ],
        out_specs=[row_spec(0), row_spec(0)],
        out_shape=[jax.ShapeDtypeStruct((batch, length, D_MODEL), F32),
                   jax.ShapeDtypeStruct((batch, length, D_MODEL), BF16)],
        compiler_params=_params("parallel", "parallel"),
        name="pool_merge_out",
    )(proj3, proj3, proj3, proj3, y_attn, h_res3, w_pool, pool_scale, w_out, gain_post, gain_next)


def _ffn_kernel(hn_ref, h_ref, wg_ref, wu_ref, wd_ref, gain_post_ref, gain_next_ref,
                hres_out, hnorm_out):
    hn = hn_ref[...]
    ff_out = jnp.zeros(h_ref.shape, F32)
    for c in range(D_FF // FF_CHUNK):
        cols = slice(c * FF_CHUNK, (c + 1) * FF_CHUNK)
        gate = jnp.dot(hn, wg_ref[:, cols], preferred_element_type=F32)
        up = jnp.dot(hn, wu_ref[:, cols], preferred_element_type=F32)
        ff = gate * jax.nn.sigmoid(gate) * up
        ff_out = ff_out + jnp.dot(ff.astype(BF16), wd_ref[cols, :], preferred_element_type=F32)
    h_new = h_ref[...] + _rms_norm(ff_out, gain_post_ref[...])
    hres_out[...] = h_new
    hnorm_out[...] = _rms_norm(h_new, gain_next_ref[...]).astype(hnorm_out.dtype)


def _ffn_call(h_norm, h_res, w_gate, w_up, w_down, gain_post, gain_next):
    rows = h_res.shape[0]
    row_spec = pl.BlockSpec((ROW_TILE, D_MODEL), lambda i: (i, 0))
    resident = lambda shape: pl.BlockSpec(shape, lambda i: (0, 0), pipeline_mode=pl.Buffered(1))
    return pl.pallas_call(
        _ffn_kernel,
        grid=(rows // ROW_TILE,),
        in_specs=[row_spec, row_spec,
                  resident((D_MODEL, D_FF)), resident((D_MODEL, D_FF)), resident((D_FF, D_MODEL)),
                  resident((1, D_MODEL)), resident((1, D_MODEL))],
        out_specs=[row_spec, row_spec],
        out_shape=[jax.ShapeDtypeStruct((rows, D_MODEL), F32),
                   jax.ShapeDtypeStruct((rows, D_MODEL), BF16)],
        compiler_params=_params("parallel"),
        name="swiglu_ffn",
    )(h_norm, h_res, w_gate, w_up, w_down, gain_post, gain_next)


def kernel(x, meta_tokens, norm_mix_pre, norm_mix_post, norm_ffn_pre, norm_ffn_post,
           w_in, b_forget, w_pool, pool_scale, w_out, w_ffn_gate, w_ffn_up, w_ffn_down):
    batch, seq, _ = x.shape
    depth = w_in.shape[0]
    tokens = META_TOKENS + seq
    length = -(-tokens // ROW_TILE) * ROW_TILE
    assert (batch * length) % PROJ_ROW_TILE == 0
    meta = jnp.broadcast_to(meta_tokens[None].astype(x.dtype), (batch, META_TOKENS, D_MODEL))
    tail = jnp.zeros((batch, length - tokens, D_MODEL), x.dtype)
    h_res = jnp.concatenate([meta, x, tail], axis=1).reshape(batch * length, D_MODEL)
    gain = lambda g: g.reshape(1, D_MODEL).astype(F32)
    sec = lambda w, n: w[:, n * D_MODEL:(n + 1) * D_MODEL]
    gates_at = 4 * D_MODEL + N_HEADS
    lane_head = jnp.minimum(jnp.arange(LANES) // BIAS_PIECES, N_HEADS - 1)

    h_norm = _norm_call(h_res, gain(norm_mix_pre[0]))
    for l in range(depth):
        w = w_in[l]
        w_rows = jnp.concatenate([sec(w, 0), sec(w, 2), w[:, gates_at:]], axis=1).astype(BF16)
        w_t = jnp.concatenate([sec(w, 1), sec(w, 3)], axis=1).T.astype(BF16)
        w_forget = w[:, 4 * D_MODEL:gates_at]
        wf_rep = w_forget[:, lane_head].astype(BF16)
        bias_rep = b_forget[l][lane_head].reshape(1, LANES).astype(F32)

        h_norm3 = h_norm.reshape(batch, length, D_MODEL)
        proj3 = _proj_call(h_norm, w_rows).reshape(batch, length, N_ROW_BLOCKS * D_MODEL)
        proj_t = _proj_t_call(h_norm3, w_t)
        key_bias = _forget_call(h_norm3, wf_rep, bias_rep)
        y_attn = _attn_call(proj3, proj_t, key_bias)
        h_res3, h_norm3 = _mix_call(
            proj3, y_attn, h_res.reshape(batch, length, D_MODEL),
            w_pool[l].astype(BF16), gain(pool_scale[l]), w_out[l].astype(BF16),
            gain(norm_mix_post[l]), gain(norm_ffn_pre[l]))
        next_gain = norm_mix_pre[(l + 1) % depth]
        h_res, h_norm = _ffn_call(
            h_norm3.reshape(batch * length, D_MODEL), h_res3.reshape(batch * length, D_MODEL),
            w_ffn_gate[l].astype(BF16), w_ffn_up[l].astype(BF16), w_ffn_down[l].astype(BF16),
            gain(norm_ffn_post[l]), gain(next_gain))
    return h_res.reshape(batch, length, D_MODEL)[:, META_TOKENS:tokens]
```

```python
import math

import jax
import jax.numpy as jnp
from jax import lax
from jax.experimental import pallas as pl
from jax.experimental.pallas import tpu as pltpu

D_MODEL = 1024
N_HEADS = 16
HEAD_DIM = 64
META_TOKENS = 16
POOL_WINDOWS = (2, 4, 8, 16)
POOL_GROUP_DIM = D_MODEL // len(POOL_WINDOWS)
POOL_HALO = max(POOL_WINDOWS)
D_FF = 2816
FF_CHUNK = 1408
RMS_EPS = 1e-6
MASK_VALUE = -1e30
LOG2_E = math.log2(math.e)

LANES = 128
SUBLANES = 8
MXU_DIM = 256
ROW_TILE = 3 * MXU_DIM
CHUNK_COLS = 2 * MXU_DIM
PROJ_ROW_TILE = 2 * ROW_TILE
N_ROW_BLOCKS = 4
COL_POOL, COL_K, COL_GPOOL, COL_GATTN = range(N_ROW_BLOCKS)
N_T_BLOCKS = 2
T_Q, T_V = range(N_T_BLOCKS)
HEADS_PER_STEP = LANES // HEAD_DIM
BIAS_PIECES = 3
ONES_LANE = BIAS_PIECES * N_HEADS
REF_ROW = LANES + ONES_LANE
PACKED_ROWS = 16
NORM_SLACK = 1.01
BIAS_SLACK = 0.05
DENOM_FLOOR = 2.0 ** -60
VMEM_LIMIT_BYTES = 56 * 1024 * 1024

F32 = jnp.float32
BF16 = jnp.bfloat16


def _rms_norm(x, gain):
    return x * lax.rsqrt(jnp.mean(x * x, axis=-1, keepdims=True) + RMS_EPS) * gain


def _params(*semantics):
    return pltpu.CompilerParams(dimension_semantics=semantics, vmem_limit_bytes=VMEM_LIMIT_BYTES)


def _split_bf16(x):
    hi = x.astype(BF16)
    rest = x - hi.astype(F32)
    mid = rest.astype(BF16)
    lo = (rest - mid.astype(F32)).astype(BF16)
    return hi, mid, lo


def _norm_kernel(h_ref, gain_ref, out_ref):
    out_ref[...] = _rms_norm(h_ref[...], gain_ref[...]).astype(out_ref.dtype)


def _norm_call(h_res, gain):
    rows = h_res.shape[0]
    return pl.pallas_call(
        _norm_kernel,
        grid=(rows // ROW_TILE,),
        in_specs=[pl.BlockSpec((ROW_TILE, D_MODEL), lambda i: (i, 0)),
                  pl.BlockSpec((1, D_MODEL), lambda i: (0, 0))],
        out_specs=pl.BlockSpec((ROW_TILE, D_MODEL), lambda i: (i, 0)),
        out_shape=jax.ShapeDtypeStruct((rows, D_MODEL), BF16),
        compiler_params=_params("parallel"),
        name="pre_norm",
    )(h_res, gain)


def _proj_kernel(h_ref, w_ref, out_ref):
    out_ref[...] = jnp.dot(h_ref[...], w_ref[...], preferred_element_type=F32).astype(out_ref.dtype)


def _proj_call(h_norm, w_rows):
    rows = h_norm.shape[0]
    return pl.pallas_call(
        _proj_kernel,
        grid=(rows // PROJ_ROW_TILE, N_ROW_BLOCKS),
        in_specs=[pl.BlockSpec((PROJ_ROW_TILE, D_MODEL), lambda i, j: (i, 0)),
                  pl.BlockSpec((D_MODEL, D_MODEL), lambda i, j: (0, j))],
        out_specs=pl.BlockSpec((PROJ_ROW_TILE, D_MODEL), lambda i, j: (i, j)),
        out_shape=jax.ShapeDtypeStruct((rows, N_ROW_BLOCKS * D_MODEL), BF16),
        compiler_params=_params("parallel", "arbitrary"),
        name="in_proj",
    )(h_norm, w_rows)


def _proj_t_kernel(h_ref, wt_ref, out_ref):
    acc = lax.dot_general(wt_ref[...], h_ref[...], (((1,), (1,)), ((), ())),
                          preferred_element_type=F32)
    scale = jnp.where(pl.program_id(2) == T_Q, LOG2_E * HEAD_DIM ** -0.5, 1.0).astype(F32)
    out_ref[...] = (acc * scale).astype(out_ref.dtype)


def _proj_t_call(h_norm3, w_t):
    batch, length, _ = h_norm3.shape
    return pl.pallas_call(
        _proj_t_kernel,
        grid=(batch, length // ROW_TILE, N_T_BLOCKS),
        in_specs=[pl.BlockSpec((None, ROW_TILE, D_MODEL), lambda b, i, j: (b, i, 0)),
                  pl.BlockSpec((D_MODEL, D_MODEL), lambda b, i, j: (j, 0))],
        out_specs=pl.BlockSpec((None, D_MODEL, ROW_TILE), lambda b, i, j: (b, j, i)),
        out_shape=jax.ShapeDtypeStruct((batch, N_T_BLOCKS * D_MODEL, length), BF16),
        compiler_params=_params("parallel", "parallel", "arbitrary"),
        name="in_proj_t",
    )(h_norm3, w_t)


def _forget_kernel(h_ref, wf_ref, bias_ref, out_ref, end_ref, carry_ref):
    @pl.when(pl.program_id(1) == 0)
    def _():
        carry_ref[...] = jnp.zeros_like(carry_ref)

    logit = jnp.dot(h_ref[...], wf_ref[...], preferred_element_type=F32) + bias_ref[...]
    log_f = jnp.minimum(logit, 0.0) - jnp.log(1.0 + jnp.exp(-jnp.abs(logit)))
    tile = log_f.shape[0]
    dst = lax.broadcasted_iota(jnp.int32, (tile, tile), 0)
    src = lax.broadcasted_iota(jnp.int32, (tile, tile), 1)
    prefix = (src <= dst).astype(BF16)
    total = jnp.broadcast_to(carry_ref[...], log_f.shape)
    for piece in _split_bf16(log_f):
        total = total + jnp.dot(prefix, piece, preferred_element_type=F32)
    carry_ref[...] = total[tile - 1:tile, :]
    bias = -LOG2_E * total
    end_ref[...] = jnp.broadcast_to(bias[tile - 1:tile, :], end_ref.shape)
    hi, mid, lo = _split_bf16(bias)
    lane = lax.broadcasted_iota(jnp.int32, total.shape, 1)
    piece = lane % BIAS_PIECES
    out = jnp.where(piece == 0, hi, jnp.where(piece == 1, mid, lo))
    tail = jnp.where(lane < ONES_LANE + BIAS_PIECES, 1.0, 0.0).astype(BF16)
    out_ref[...] = jnp.where(lane < ONES_LANE, out, tail)


def _forget_call(h_norm3, wf_rep, bias_rep):
    batch, length, _ = h_norm3.shape
    n_tiles = length // ROW_TILE
    return pl.pallas_call(
        _forget_kernel,
        grid=(batch, n_tiles),
        in_specs=[pl.BlockSpec((None, ROW_TILE, D_MODEL), lambda b, i: (b, i, 0)),
                  pl.BlockSpec((D_MODEL, LANES), lambda b, i: (0, 0)),
                  pl.BlockSpec((1, LANES), lambda b, i: (0, 0))],
        out_specs=[pl.BlockSpec((None, ROW_TILE, LANES), lambda b, i: (b, i, 0)),
                   pl.BlockSpec((None, None, SUBLANES, LANES), lambda b, i: (b, i, 0, 0))],
        out_shape=[jax.ShapeDtypeStruct((batch, length, LANES), BF16),
                   jax.ShapeDtypeStruct((batch, n_tiles, SUBLANES, LANES), F32)],
        scratch_shapes=[pltpu.VMEM((1, LANES), F32)],
        compiler_params=_params("parallel", "arbitrary"),
        name="forget_bias",
    )(h_norm3, wf_rep, bias_rep)


def _key_norm_kernel(k_ref, head_ref, out_ref):
    k = k_ref[...].astype(F32)
    sums = jnp.dot((k * k).astype(BF16), head_ref[...], preferred_element_type=F32)
    out_ref[...] = jnp.broadcast_to(jnp.sqrt(jnp.max(sums, axis=0, keepdims=True)), out_ref.shape)


def _key_norm_call(proj3, head_indicator):
    batch, length, _ = proj3.shape
    n_tiles = length // ROW_TILE
    return pl.pallas_call(
        _key_norm_kernel,
        grid=(batch, n_tiles),
        in_specs=[pl.BlockSpec((None, ROW_TILE, D_MODEL), lambda b, i: (b, i, COL_K)),
                  pl.BlockSpec((D_MODEL, LANES), lambda b, i: (0, 0))],
        out_specs=pl.BlockSpec((None, None, SUBLANES, LANES), lambda b, i: (b, i, 0, 0)),
        out_shape=jax.ShapeDtypeStruct((batch, n_tiles, SUBLANES, LANES), F32),
        compiler_params=_params("parallel", "parallel"),
        name="key_norms",
    )(proj3, head_indicator)


def _attn_kernel(kmax_ref, gend_ref, qt_ref, k_ref, fb_ref, vt_ref, out_ref,
                 acc_ref, qaug_ref, *chunk_refs):
    bi = pl.program_id(0)
    pair = pl.program_id(1)
    qi = pl.program_id(2)
    n_tiles = pl.num_programs(2)
    tile = qt_ref.shape[1]
    both = HEADS_PER_STEP * tile
    n_chunks = both // CHUNK_COLS
    s_refs, p_refs = chunk_refs[:n_chunks], chunk_refs[n_chunks:]

    feat = lax.broadcasted_iota(jnp.int32, (LANES, tile), 0)
    q_norms = []
    for hh in range(HEADS_PER_STEP):
        head = pair * HEADS_PER_STEP + hh
        in_head = (feat >= hh * HEAD_DIM) & (feat < (hh + 1) * HEAD_DIM)
        q_top = jnp.where(in_head, qt_ref[...], jnp.zeros_like(qt_ref[...]))
        is_piece = (feat >= head * BIAS_PIECES) & (feat < (head + 1) * BIAS_PIECES)
        qaug_ref[:, hh * tile:(hh + 1) * tile] = jnp.concatenate(
            [q_top, is_piece.astype(BF16)], axis=0)
        q_head = qt_ref[hh * HEAD_DIM:(hh + 1) * HEAD_DIM, :].astype(F32)
        q_norms.append(jnp.sqrt(jnp.sum(q_head * q_head, axis=0, keepdims=True)))
    q_norm = jnp.concatenate(q_norms, axis=1)
    second_head = lax.broadcasted_iota(jnp.int32, (1, both), 1) >= tile

    def head_segments(c):
        segments = []
        start = c * CHUNK_COLS
        while start < (c + 1) * CHUNK_COLS:
            hh = start // tile
            stop = min((c + 1) * CHUNK_COLS, (hh + 1) * tile)
            segments.append((hh, slice(start - hh * tile, stop - hh * tile),
                             slice(start - c * CHUNK_COLS, stop - c * CHUNK_COLS)))
            start = stop
        return segments

    def scores(c, k_start, on_diagonal):
        keys = jnp.concatenate([k_ref[pl.ds(k_start, tile), :],
                                fb_ref[pl.ds(k_start, tile), :]], axis=1)
        s_t = jnp.dot(keys, qaug_ref[:, c * CHUNK_COLS:(c + 1) * CHUNK_COLS],
                      preferred_element_type=F32)
        if on_diagonal:
            key_row = lax.broadcasted_iota(jnp.int32, s_t.shape, 0)
            col = lax.broadcasted_iota(jnp.int32, s_t.shape, 1) + c * CHUNK_COLS
            query_col = jnp.where(col >= tile, col - tile, col)
            s_t = jnp.where(key_row <= query_col, s_t, MASK_VALUE)
        return s_t

    def value_stage(c, k_start, alpha):
        for hh, head_cols, chunk_cols in head_segments(c):
            v_aug = jnp.concatenate(
                [vt_ref[hh * HEAD_DIM:(hh + 1) * HEAD_DIM, pl.ds(k_start, tile)],
                 jnp.ones((PACKED_ROWS, tile), BF16)], axis=0)
            acc_ref[hh, :, head_cols] = (
                alpha[:, c * CHUNK_COLS:(c + 1) * CHUNK_COLS][:, chunk_cols]
                * acc_ref[hh, :, head_cols]
                + jnp.dot(v_aug, p_refs[c][:, chunk_cols], preferred_element_type=F32))

    def set_reference(r):
        hi, mid, lo = (piece.astype(F32) for piece in _split_bf16(-r))
        row = lax.broadcasted_iota(jnp.int32, (PACKED_ROWS, both), 0)
        pieces = jnp.where(row == 0, hi, jnp.where(row == 1, mid, jnp.where(row == 2, lo, 0.0)))
        qaug_ref[REF_ROW:REF_ROW + PACKED_ROWS, :] = pieces.astype(BF16)

    def bound_tile(kj, r_prev, on_diagonal):
        k_start = pl.multiple_of(kj * tile, LANES)
        base = (bi * n_tiles + kj) * N_HEADS + pair * HEADS_PER_STEP
        k_max = jnp.where(second_head, kmax_ref[base + 1], kmax_ref[base])
        g_max = jnp.where(second_head, gend_ref[base + 1], gend_ref[base])
        r_new = jnp.maximum(r_prev, q_norm * k_max * NORM_SLACK + g_max + BIAS_SLACK)
        alpha = jnp.exp2(r_prev - r_new)
        set_reference(r_new)

        def probabilities(c):
            p_refs[c][...] = jnp.exp2(scores(c, k_start, on_diagonal)).astype(BF16)

        probabilities(0)
        probabilities(1)
        value_stage(0, k_start, alpha)
        probabilities(2)
        value_stage(1, k_start, alpha)
        value_stage(2, k_start, alpha)
        return r_new

    assert n_chunks == 3
    acc_ref[...] = jnp.zeros_like(acc_ref)
    r_run = jnp.full((1, both), MASK_VALUE, F32)
    r_run = lax.fori_loop(0, qi, lambda kj, r: bound_tile(kj, r, False), r_run)
    bound_tile(qi, r_run, True)
    smallest_denominator = jnp.min(acc_ref[:, HEAD_DIM:HEAD_DIM + 1, :])

    @pl.when(jnp.logical_not(smallest_denominator > DENOM_FLOOR))
    def _():
        def max_tile(kj, m_prev, on_diagonal):
            k_start = pl.multiple_of(kj * tile, LANES)
            m_tile = []
            for c in range(n_chunks):
                s_t = scores(c, k_start, on_diagonal)
                s_refs[c][...] = s_t
                m_tile.append(jnp.max(s_t, axis=0, keepdims=True))
            m_new = jnp.maximum(m_prev, jnp.concatenate(m_tile, axis=1))
            alpha = jnp.exp2(m_prev - m_new)
            for c in range(n_chunks):
                cols = slice(c * CHUNK_COLS, (c + 1) * CHUNK_COLS)
                p_refs[c][...] = jnp.exp2(s_refs[c][...] - m_new[:, cols]).astype(BF16)
                value_stage(c, k_start, alpha)
            return m_new

        acc_ref[...] = jnp.zeros_like(acc_ref)
        set_reference(jnp.zeros((1, both), F32))
        m_run = jnp.full((1, both), MASK_VALUE, F32)
        m_run = lax.fori_loop(0, qi, lambda kj, m: max_tile(kj, m, False), m_run)
        max_tile(qi, m_run, True)

    heads_out = [acc_ref[hh, :HEAD_DIM, :] / acc_ref[hh, HEAD_DIM:HEAD_DIM + 1, :]
                 for hh in range(HEADS_PER_STEP)]
    out_ref[...] = jnp.concatenate(heads_out, axis=0).T.astype(out_ref.dtype)


def _attn_call(proj3, proj_t, key_bias, key_norm_max, bias_max):
    batch, length, _ = proj3.shape
    pairs = N_HEADS // HEADS_PER_STEP
    blocks_per_section = D_MODEL // LANES
    n_chunks = HEADS_PER_STEP * ROW_TILE // CHUNK_COLS
    return pl.pallas_call(
        _attn_kernel,
        grid=(batch, pairs, length // ROW_TILE),
        in_specs=[
            pl.BlockSpec(memory_space=pltpu.SMEM),
            pl.BlockSpec(memory_space=pltpu.SMEM),
            pl.BlockSpec((None, LANES, ROW_TILE), lambda b, p, i: (b, T_Q * blocks_per_section + p, i)),
            pl.BlockSpec((None, length, LANES), lambda b, p, i: (b, 0, COL_K * blocks_per_section + p)),
            pl.BlockSpec((None, length, LANES), lambda b, p, i: (b, 0, 0)),
            pl.BlockSpec((None, LANES, length), lambda b, p, i: (b, T_V * blocks_per_section + p, 0)),
        ],
        out_specs=pl.BlockSpec((None, ROW_TILE, LANES), lambda b, p, i: (b, i, p)),
        out_shape=jax.ShapeDtypeStruct((batch, length, D_MODEL), BF16),
        scratch_shapes=[
            pltpu.VMEM((HEADS_PER_STEP, HEAD_DIM + PACKED_ROWS, ROW_TILE), F32),
            pltpu.VMEM((2 * LANES, HEADS_PER_STEP * ROW_TILE), BF16),
        ] + [pltpu.VMEM((ROW_TILE, CHUNK_COLS), dtype) for dtype in (F32, BF16)
             for _ in range(n_chunks)],
        compiler_params=_params("parallel", "parallel", "arbitrary"),
        name="forgetting_attention",
    )(key_norm_max, bias_max, proj_t, proj3, key_bias, proj_t)


def _mix_kernel(u_ref, halo_ref, gp_ref, ga_ref, ya_ref, h_ref, wp_ref, ps_ref, wo_ref,
                gain_post_ref, gain_next_ref, hres_out, hnorm_out):
    li = pl.program_id(1)
    tile = u_ref.shape[0]
    u = u_ref[...].astype(F32)
    halo = jnp.where(li > 0, halo_ref[...].astype(F32), 0.0)
    ext = jnp.concatenate([halo, u], axis=0)
    pos1 = li * tile + lax.broadcasted_iota(jnp.int32, (tile, 1), 0) + 1
    pooled = []
    for g, window in enumerate(POOL_WINDOWS):
        lanes = slice(g * POOL_GROUP_DIM, (g + 1) * POOL_GROUP_DIM)
        run = ext[:, lanes]
        width = 1
        while width < window:
            run = run[width:] + run[:-width]
            width *= 2
        first = POOL_HALO + 1 - window
        window_sum = run[first:first + tile]
        count = jnp.minimum(pos1, window).astype(F32)
        diff = window_sum / count - u[:, lanes]
        pooled.append(jnp.dot(diff.astype(BF16), wp_ref[g], preferred_element_type=F32))
    y_pool = jnp.concatenate(pooled, axis=1) * ps_ref[...]
    merged = (jax.nn.sigmoid(gp_ref[...].astype(F32)) * y_pool
              + jax.nn.sigmoid(ga_ref[...].astype(F32)) * ya_ref[...].astype(F32))
    mix = jnp.dot(merged.astype(BF16), wo_ref[...], preferred_element_type=F32)
    h_new = h_ref[...] + _rms_norm(mix, gain_post_ref[...])
    hres_out[...] = h_new
    hnorm_out[...] = _rms_norm(h_new, gain_next_ref[...]).astype(hnorm_out.dtype)


def _mix_call(proj3, y_attn, h_res3, w_pool, pool_scale, w_out, gain_post, gain_next):
    batch, length, _ = proj3.shape
    halo_blocks = ROW_TILE // POOL_HALO
    row_spec = lambda col: pl.BlockSpec((None, ROW_TILE, D_MODEL), lambda b, i: (b, i, col))
    const = lambda shape: pl.BlockSpec(shape, lambda b, i: (0,) * len(shape))
    return pl.pallas_call(
        _mix_kernel,
        grid=(batch, length // ROW_TILE),
        in_specs=[
            row_spec(COL_POOL),
            pl.BlockSpec((None, POOL_HALO, D_MODEL),
                         lambda b, i: (b, jnp.maximum(i * halo_blocks - 1, 0), COL_POOL)),
            row_spec(COL_GPOOL),
            row_spec(COL_GATTN),
            row_spec(0),
            row_spec(0),
            const((len(POOL_WINDOWS), POOL_GROUP_DIM, POOL_GROUP_DIM)),
            const((1, D_MODEL)),
            const((D_MODEL, D_MODEL)),
            const((1, D_MODEL)),
            const((1, D_MODEL)),
        ],
        out_specs=[row_spec(0), row_spec(0)],
        out_shape=[jax.ShapeDtypeStruct((batch, length, D_MODEL), F32),
                   jax.ShapeDtypeStruct((batch, length, D_MODEL), BF16)],
        compiler_params=_params("parallel", "parallel"),
        name="pool_merge_out",
    )(proj3, proj3, proj3, proj3, y_attn, h_res3, w_pool, pool_scale, w_out, gain_post, gain_next)


def _ffn_kernel(hn_ref, h_ref, wg_ref, wu_ref, wd_ref, gain_post_ref, gain_next_ref,
                hres_out, hnorm_out):
    hn = hn_ref[...]
    ff_out = jnp.zeros(h_ref.shape, F32)
    for c in range(D_FF // FF_CHUNK):
        cols = slice(c * FF_CHUNK, (c + 1) * FF_CHUNK)
        gate = jnp.dot(hn, wg_ref[:, cols], preferred_element_type=F32)
        up = jnp.dot(hn, wu_ref[:, cols], preferred_element_type=F32)
        ff = gate * jax.nn.sigmoid(gate) * up
        ff_out = ff_out + jnp.dot(ff.astype(BF16), wd_ref[cols, :], preferred_element_type=F32)
    h_new = h_ref[...] + _rms_norm(ff_out, gain_post_ref[...])
    hres_out[...] = h_new
    hnorm_out[...] = _rms_norm(h_new, gain_next_ref[...]).astype(hnorm_out.dtype)


def _ffn_call(h_norm, h_res, w_gate, w_up, w_down, gain_post, gain_next):
    rows = h_res.shape[0]
    row_spec = pl.BlockSpec((ROW_TILE, D_MODEL), lambda i: (i, 0))
    resident = lambda shape: pl.BlockSpec(shape, lambda i: (0, 0), pipeline_mode=pl.Buffered(1))
    return pl.pallas_call(
        _ffn_kernel,
        grid=(rows // ROW_TILE,),
        in_specs=[row_spec, row_spec,
                  resident((D_MODEL, D_FF)), resident((D_MODEL, D_FF)), resident((D_FF, D_MODEL)),
                  resident((1, D_MODEL)), resident((1, D_MODEL))],
        out_specs=[row_spec, row_spec],
        out_shape=[jax.ShapeDtypeStruct((rows, D_MODEL), F32),
                   jax.ShapeDtypeStruct((rows, D_MODEL), BF16)],
        compiler_params=_params("parallel"),
        name="swiglu_ffn",
    )(h_norm, h_res, w_gate, w_up, w_down, gain_post, gain_next)


def kernel(x, meta_tokens, norm_mix_pre, norm_mix_post, norm_ffn_pre, norm_ffn_post,
           w_in, b_forget, w_pool, pool_scale, w_out, w_ffn_gate, w_ffn_up, w_ffn_down):
    batch, seq, _ = x.shape
    depth = w_in.shape[0]
    tokens = META_TOKENS + seq
    length = -(-tokens // ROW_TILE) * ROW_TILE
    assert (batch * length) % PROJ_ROW_TILE == 0
    meta = jnp.broadcast_to(meta_tokens[None].astype(x.dtype), (batch, META_TOKENS, D_MODEL))
    tail = jnp.zeros((batch, length - tokens, D_MODEL), x.dtype)
    h_res = jnp.concatenate([meta, x, tail], axis=1).reshape(batch * length, D_MODEL)
    gain = lambda g: g.reshape(1, D_MODEL).astype(F32)
    sec = lambda w, n: w[:, n * D_MODEL:(n + 1) * D_MODEL]
    gates_at = 4 * D_MODEL + N_HEADS
    lane_head = jnp.minimum(jnp.arange(LANES) // BIAS_PIECES, N_HEADS - 1)
    head_indicator = (jnp.arange(D_MODEL)[:, None] // HEAD_DIM == jnp.arange(LANES)[None, :]).astype(BF16)

    h_norm = _norm_call(h_res, gain(norm_mix_pre[0]))
    for l in range(depth):
        w = w_in[l]
        w_rows = jnp.concatenate([sec(w, 0), sec(w, 2), w[:, gates_at:]], axis=1).astype(BF16)
        w_t = jnp.concatenate([sec(w, 1), sec(w, 3)], axis=1).T.astype(BF16)
        w_forget = w[:, 4 * D_MODEL:gates_at]
        wf_rep = w_forget[:, lane_head].astype(BF16)
        bias_rep = b_forget[l][lane_head].reshape(1, LANES).astype(F32)

        h_norm3 = h_norm.reshape(batch, length, D_MODEL)
        proj3 = _proj_call(h_norm, w_rows).reshape(batch, length, N_ROW_BLOCKS * D_MODEL)
        proj_t = _proj_t_call(h_norm3, w_t)
        key_bias, bias_end = _forget_call(h_norm3, wf_rep, bias_rep)
        key_norms = _key_norm_call(proj3, head_indicator)
        key_norm_max = key_norms[:, :, 0, :N_HEADS].reshape(-1)
        bias_max = bias_end[:, :, 0, 0:ONES_LANE:BIAS_PIECES].reshape(-1)
        y_attn = _attn_call(proj3, proj_t, key_bias, key_norm_max, bias_max)
        h_res3, h_norm3 = _mix_call(
            proj3, y_attn, h_res.reshape(batch, length, D_MODEL),
            w_pool[l].astype(BF16), gain(pool_scale[l]), w_out[l].astype(BF16),
            gain(norm_mix_post[l]), gain(norm_ffn_pre[l]))
        next_gain = norm_mix_pre[(l + 1) % depth]
        h_res, h_norm = _ffn_call(
            h_norm3.reshape(batch * length, D_MODEL), h_res3.reshape(batch * length, D_MODEL),
            w_ffn_gate[l].astype(BF16), w_ffn_up[l].astype(BF16), w_ffn_down[l].astype(BF16),
            gain(norm_ffn_post[l]), gain(next_gain))
    return h_res.reshape(batch, length, D_MODEL)[:, META_TOKENS:tokens]
```

```python
import math

import jax
import jax.numpy as jnp
from jax import lax
from jax.experimental import pallas as pl
from jax.experimental.pallas import tpu as pltpu

D_MODEL = 1024
N_HEADS = 16
HEAD_DIM = 64
META_TOKENS = 16
POOL_WINDOWS = (2, 4, 8, 16)
POOL_GROUP_DIM = D_MODEL // len(POOL_WINDOWS)
POOL_HALO = max(POOL_WINDOWS)
D_FF = 2816
FF_CHUNK = 1408
RMS_EPS = 1e-6
MASK_VALUE = -1e30
LOG2_E = math.log2(math.e)

LANES = 128
SUBLANES = 8
MXU_DIM = 256
ROW_TILE = 3 * MXU_DIM
CHUNK_COLS = 2 * MXU_DIM
PROJ_ROW_TILE = 2 * ROW_TILE
N_ROW_BLOCKS = 4
COL_POOL, COL_K, COL_GPOOL, COL_GATTN = range(N_ROW_BLOCKS)
N_T_BLOCKS = 2
T_Q, T_V = range(N_T_BLOCKS)
HEADS_PER_STEP = LANES // HEAD_DIM
BIAS_PIECES = 3
ONES_LANE = BIAS_PIECES * N_HEADS
REF_ROW = LANES + ONES_LANE
PACKED_ROWS = 16
NORM_SLACK = 1.01
BIAS_SLACK = 0.05
DENOM_FLOOR = 2.0 ** -80
VMEM_LIMIT_BYTES = 56 * 1024 * 1024

F32 = jnp.float32
BF16 = jnp.bfloat16


def _rms_norm(x, gain):
    return x * lax.rsqrt(jnp.mean(x * x, axis=-1, keepdims=True) + RMS_EPS) * gain


def _params(*semantics):
    return pltpu.CompilerParams(dimension_semantics=semantics, vmem_limit_bytes=VMEM_LIMIT_BYTES)


def _split_bf16(x):
    hi = x.astype(BF16)
    rest = x - hi.astype(F32)
    mid = rest.astype(BF16)
    lo = (rest - mid.astype(F32)).astype(BF16)
    return hi, mid, lo


def _norm_kernel(h_ref, gain_ref, out_ref):
    out_ref[...] = _rms_norm(h_ref[...], gain_ref[...]).astype(out_ref.dtype)


def _norm_call(h_res, gain):
    rows = h_res.shape[0]
    return pl.pallas_call(
        _norm_kernel,
        grid=(rows // ROW_TILE,),
        in_specs=[pl.BlockSpec((ROW_TILE, D_MODEL), lambda i: (i, 0)),
                  pl.BlockSpec((1, D_MODEL), lambda i: (0, 0))],
        out_specs=pl.BlockSpec((ROW_TILE, D_MODEL), lambda i: (i, 0)),
        out_shape=jax.ShapeDtypeStruct((rows, D_MODEL), BF16),
        compiler_params=_params("parallel"),
        name="pre_norm",
    )(h_res, gain)


def _proj_kernel(h_ref, w_ref, out_ref):
    out_ref[...] = jnp.dot(h_ref[...], w_ref[...], preferred_element_type=F32).astype(out_ref.dtype)


def _proj_call(h_norm, w_rows):
    rows = h_norm.shape[0]
    return pl.pallas_call(
        _proj_kernel,
        grid=(rows // PROJ_ROW_TILE, N_ROW_BLOCKS),
        in_specs=[pl.BlockSpec((PROJ_ROW_TILE, D_MODEL), lambda i, j: (i, 0)),
                  pl.BlockSpec((D_MODEL, D_MODEL), lambda i, j: (0, j))],
        out_specs=pl.BlockSpec((PROJ_ROW_TILE, D_MODEL), lambda i, j: (i, j)),
        out_shape=jax.ShapeDtypeStruct((rows, N_ROW_BLOCKS * D_MODEL), BF16),
        compiler_params=_params("parallel", "arbitrary"),
        name="in_proj",
    )(h_norm, w_rows)


def _proj_t_kernel(h_ref, wt_ref, out_ref):
    acc = lax.dot_general(wt_ref[...], h_ref[...], (((1,), (1,)), ((), ())),
                          preferred_element_type=F32)
    scale = jnp.where(pl.program_id(2) == T_Q, LOG2_E * HEAD_DIM ** -0.5, 1.0).astype(F32)
    out_ref[...] = (acc * scale).astype(out_ref.dtype)


def _proj_t_call(h_norm3, w_t):
    batch, length, _ = h_norm3.shape
    return pl.pallas_call(
        _proj_t_kernel,
        grid=(batch, length // ROW_TILE, N_T_BLOCKS),
        in_specs=[pl.BlockSpec((None, ROW_TILE, D_MODEL), lambda b, i, j: (b, i, 0)),
                  pl.BlockSpec((D_MODEL, D_MODEL), lambda b, i, j: (j, 0))],
        out_specs=pl.BlockSpec((None, D_MODEL, ROW_TILE), lambda b, i, j: (b, j, i)),
        out_shape=jax.ShapeDtypeStruct((batch, N_T_BLOCKS * D_MODEL, length), BF16),
        compiler_params=_params("parallel", "parallel", "arbitrary"),
        name="in_proj_t",
    )(h_norm3, w_t)


def _forget_kernel(h_ref, wf_ref, bias_ref, out_ref, end_ref, carry_ref):
    @pl.when(pl.program_id(1) == 0)
    def _():
        carry_ref[...] = jnp.zeros_like(carry_ref)

    logit = jnp.dot(h_ref[...], wf_ref[...], preferred_element_type=F32) + bias_ref[...]
    log_f = jnp.minimum(logit, 0.0) - jnp.log(1.0 + jnp.exp(-jnp.abs(logit)))
    tile = log_f.shape[0]
    dst = lax.broadcasted_iota(jnp.int32, (tile, tile), 0)
    src = lax.broadcasted_iota(jnp.int32, (tile, tile), 1)
    prefix = (src <= dst).astype(BF16)
    total = jnp.broadcast_to(carry_ref[...], log_f.shape)
    for piece in _split_bf16(log_f):
        total = total + jnp.dot(prefix, piece, preferred_element_type=F32)
    carry_ref[...] = total[tile - 1:tile, :]
    bias = -LOG2_E * total
    end_ref[...] = jnp.broadcast_to(bias[tile - 1:tile, :], end_ref.shape)
    hi, mid, lo = _split_bf16(bias)
    lane = lax.broadcasted_iota(jnp.int32, total.shape, 1)
    piece = lane % BIAS_PIECES
    out = jnp.where(piece == 0, hi, jnp.where(piece == 1, mid, lo))
    tail = jnp.where(lane < ONES_LANE + BIAS_PIECES, 1.0, 0.0).astype(BF16)
    out_ref[...] = jnp.where(lane < ONES_LANE, out, tail)


def _forget_call(h_norm3, wf_rep, bias_rep):
    batch, length, _ = h_norm3.shape
    n_tiles = length // ROW_TILE
    return pl.pallas_call(
        _forget_kernel,
        grid=(batch, n_tiles),
        in_specs=[pl.BlockSpec((None, ROW_TILE, D_MODEL), lambda b, i: (b, i, 0)),
                  pl.BlockSpec((D_MODEL, LANES), lambda b, i: (0, 0)),
                  pl.BlockSpec((1, LANES), lambda b, i: (0, 0))],
        out_specs=[pl.BlockSpec((None, ROW_TILE, LANES), lambda b, i: (b, i, 0)),
                   pl.BlockSpec((None, None, SUBLANES, LANES), lambda b, i: (b, i, 0, 0))],
        out_shape=[jax.ShapeDtypeStruct((batch, length, LANES), BF16),
                   jax.ShapeDtypeStruct((batch, n_tiles, SUBLANES, LANES), F32)],
        scratch_shapes=[pltpu.VMEM((1, LANES), F32)],
        compiler_params=_params("parallel", "arbitrary"),
        name="forget_bias",
    )(h_norm3, wf_rep, bias_rep)


def _key_norm_kernel(k_ref, head_ref, out_ref):
    k = k_ref[...].astype(F32)
    sums = jnp.dot((k * k).astype(BF16), head_ref[...], preferred_element_type=F32)
    out_ref[...] = jnp.broadcast_to(jnp.sqrt(jnp.max(sums, axis=0, keepdims=True)), out_ref.shape)


def _key_norm_call(proj3, head_indicator):
    batch, length, _ = proj3.shape
    n_tiles = length // ROW_TILE
    return pl.pallas_call(
        _key_norm_kernel,
        grid=(batch, n_tiles),
        in_specs=[pl.BlockSpec((None, ROW_TILE, D_MODEL), lambda b, i: (b, i, COL_K)),
                  pl.BlockSpec((D_MODEL, LANES), lambda b, i: (0, 0))],
        out_specs=pl.BlockSpec((None, None, SUBLANES, LANES), lambda b, i: (b, i, 0, 0)),
        out_shape=jax.ShapeDtypeStruct((batch, n_tiles, SUBLANES, LANES), F32),
        compiler_params=_params("parallel", "parallel"),
        name="key_norms",
    )(proj3, head_indicator)


def _attn_kernel(kmax_ref, gend_ref, qt_ref, k_ref, fb_ref, vt_ref, out_ref,
                 acc_ref, qaug_ref, *chunk_refs):
    bi = pl.program_id(0)
    pair = pl.program_id(1)
    qi = pl.program_id(2)
    n_tiles = pl.num_programs(2)
    tile = qt_ref.shape[1]
    both = HEADS_PER_STEP * tile
    n_chunks = both // CHUNK_COLS
    s_refs, p_refs = chunk_refs[:n_chunks], chunk_refs[n_chunks:]

    feat = lax.broadcasted_iota(jnp.int32, (LANES, tile), 0)
    q_norms = []
    for hh in range(HEADS_PER_STEP):
        head = pair * HEADS_PER_STEP + hh
        in_head = (feat >= hh * HEAD_DIM) & (feat < (hh + 1) * HEAD_DIM)
        q_top = jnp.where(in_head, qt_ref[...], jnp.zeros_like(qt_ref[...]))
        is_piece = (feat >= head * BIAS_PIECES) & (feat < (head + 1) * BIAS_PIECES)
        qaug_ref[:, hh * tile:(hh + 1) * tile] = jnp.concatenate(
            [q_top, is_piece.astype(BF16)], axis=0)
        q_head = qt_ref[hh * HEAD_DIM:(hh + 1) * HEAD_DIM, :].astype(F32)
        q_norms.append(jnp.sqrt(jnp.sum(q_head * q_head, axis=0, keepdims=True)))
    q_norm = jnp.concatenate(q_norms, axis=1)
    second_head = lax.broadcasted_iota(jnp.int32, (1, both), 1) >= tile

    def head_segments(c):
        segments = []
        start = c * CHUNK_COLS
        while start < (c + 1) * CHUNK_COLS:
            hh = start // tile
            stop = min((c + 1) * CHUNK_COLS, (hh + 1) * tile)
            segments.append((hh, slice(start - hh * tile, stop - hh * tile),
                             slice(start - c * CHUNK_COLS, stop - c * CHUNK_COLS)))
            start = stop
        return segments

    def scores(c, k_start, on_diagonal):
        keys = jnp.concatenate([k_ref[pl.ds(k_start, tile), :],
                                fb_ref[pl.ds(k_start, tile), :]], axis=1)
        s_t = jnp.dot(keys, qaug_ref[:, c * CHUNK_COLS:(c + 1) * CHUNK_COLS],
                      preferred_element_type=F32)
        if on_diagonal:
            key_row = lax.broadcasted_iota(jnp.int32, s_t.shape, 0)
            col = lax.broadcasted_iota(jnp.int32, s_t.shape, 1) + c * CHUNK_COLS
            query_col = jnp.where(col >= tile, col - tile, col)
            s_t = jnp.where(key_row <= query_col, s_t, MASK_VALUE)
        return s_t

    def value_stage(c, k_start, alpha):
        for hh, head_cols, chunk_cols in head_segments(c):
            v_aug = jnp.concatenate(
                [vt_ref[hh * HEAD_DIM:(hh + 1) * HEAD_DIM, pl.ds(k_start, tile)],
                 jnp.ones((PACKED_ROWS, tile), BF16)], axis=0)
            acc_ref[hh, :, head_cols] = (
                alpha[:, c * CHUNK_COLS:(c + 1) * CHUNK_COLS][:, chunk_cols]
                * acc_ref[hh, :, head_cols]
                + jnp.dot(v_aug, p_refs[c][:, chunk_cols], preferred_element_type=F32))

    def set_reference(r):
        hi, mid, lo = (piece.astype(F32) for piece in _split_bf16(-r))
        row = lax.broadcasted_iota(jnp.int32, (PACKED_ROWS, both), 0)
        pieces = jnp.where(row == 0, hi, jnp.where(row == 1, mid, jnp.where(row == 2, lo, 0.0)))
        qaug_ref[REF_ROW:REF_ROW + PACKED_ROWS, :] = pieces.astype(BF16)

    def bound_tile(kj, r_prev, on_diagonal):
        k_start = pl.multiple_of(kj * tile, LANES)
        base = (bi * n_tiles + kj) * N_HEADS + pair * HEADS_PER_STEP
        k_max = jnp.where(second_head, kmax_ref[base + 1], kmax_ref[base])
        if on_diagonal:
            lane = lax.broadcasted_iota(jnp.int32, (PACKED_ROWS, LANES), 1)
            head_row = pair * HEADS_PER_STEP + lax.broadcasted_iota(
                jnp.int32, (PACKED_ROWS, LANES), 0)
            pick = ((lane >= head_row * BIAS_PIECES) & (lane < (head_row + 1) * BIAS_PIECES))
            g_rows = lax.dot_general(jnp.where(pick, 1.0, 0.0).astype(BF16),
                                     fb_ref[pl.ds(k_start, tile), :], (((1,), (1,)), ((), ())),
                                     preferred_element_type=F32)
            g_max = jnp.concatenate([g_rows[hh:hh + 1, :] for hh in range(HEADS_PER_STEP)], axis=1)
        else:
            g_max = jnp.where(second_head, gend_ref[base + 1], gend_ref[base])
        r_new = jnp.maximum(r_prev, q_norm * k_max * NORM_SLACK + g_max + BIAS_SLACK)
        alpha = jnp.exp2(r_prev - r_new)
        set_reference(r_new)

        def probabilities(c):
            p_refs[c][...] = jnp.exp2(scores(c, k_start, on_diagonal)).astype(BF16)

        probabilities(0)
        probabilities(1)
        value_stage(0, k_start, alpha)
        probabilities(2)
        value_stage(1, k_start, alpha)
        value_stage(2, k_start, alpha)
        return r_new

    assert n_chunks == 3
    acc_ref[...] = jnp.zeros_like(acc_ref)
    r_run = jnp.full((1, both), MASK_VALUE, F32)
    r_run = lax.fori_loop(0, qi, lambda kj, r: bound_tile(kj, r, False), r_run)
    bound_tile(qi, r_run, True)
    smallest_denominator = jnp.min(acc_ref[:, HEAD_DIM:HEAD_DIM + 1, :])

    @pl.when(jnp.logical_not(smallest_denominator > DENOM_FLOOR))
    def _():
        def max_tile(kj, m_prev, on_diagonal):
            k_start = pl.multiple_of(kj * tile, LANES)
            m_tile = []
            for c in range(n_chunks):
                s_t = scores(c, k_start, on_diagonal)
                s_refs[c][...] = s_t
                m_tile.append(jnp.max(s_t, axis=0, keepdims=True))
            m_new = jnp.maximum(m_prev, jnp.concatenate(m_tile, axis=1))
            alpha = jnp.exp2(m_prev - m_new)
            for c in range(n_chunks):
                cols = slice(c * CHUNK_COLS, (c + 1) * CHUNK_COLS)
                p_refs[c][...] = jnp.exp2(s_refs[c][...] - m_new[:, cols]).astype(BF16)
                value_stage(c, k_start, alpha)
            return m_new

        acc_ref[...] = jnp.zeros_like(acc_ref)
        set_reference(jnp.zeros((1, both), F32))
        m_run = jnp.full((1, both), MASK_VALUE, F32)
        m_run = lax.fori_loop(0, qi, lambda kj, m: max_tile(kj, m, False), m_run)
        max_tile(qi, m_run, True)

    heads_out = [acc_ref[hh, :HEAD_DIM, :] / acc_ref[hh, HEAD_DIM:HEAD_DIM + 1, :]
                 for hh in range(HEADS_PER_STEP)]
    out_ref[...] = jnp.concatenate(heads_out, axis=0).T.astype(out_ref.dtype)


def _attn_call(proj3, proj_t, key_bias, key_norm_max, bias_max):
    batch, length, _ = proj3.shape
    pairs = N_HEADS // HEADS_PER_STEP
    blocks_per_section = D_MODEL // LANES
    n_chunks = HEADS_PER_STEP * ROW_TILE // CHUNK_COLS
    return pl.pallas_call(
        _attn_kernel,
        grid=(batch, pairs, length // ROW_TILE),
        in_specs=[
            pl.BlockSpec(memory_space=pltpu.SMEM),
            pl.BlockSpec(memory_space=pltpu.SMEM),
            pl.BlockSpec((None, LANES, ROW_TILE), lambda b, p, i: (b, T_Q * blocks_per_section + p, i)),
            pl.BlockSpec((None, length, LANES), lambda b, p, i: (b, 0, COL_K * blocks_per_section + p)),
            pl.BlockSpec((None, length, LANES), lambda b, p, i: (b, 0, 0)),
            pl.BlockSpec((None, LANES, length), lambda b, p, i: (b, T_V * blocks_per_section + p, 0)),
        ],
        out_specs=pl.BlockSpec((None, ROW_TILE, LANES), lambda b, p, i: (b, i, p)),
        out_shape=jax.ShapeDtypeStruct((batch, length, D_MODEL), BF16),
        scratch_shapes=[
            pltpu.VMEM((HEADS_PER_STEP, HEAD_DIM + PACKED_ROWS, ROW_TILE), F32),
            pltpu.VMEM((2 * LANES, HEADS_PER_STEP * ROW_TILE), BF16),
        ] + [pltpu.VMEM((ROW_TILE, CHUNK_COLS), dtype) for dtype in (F32, BF16)
             for _ in range(n_chunks)],
        compiler_params=_params("parallel", "parallel", "arbitrary"),
        name="forgetting_attention",
    )(key_norm_max, bias_max, proj_t, proj3, key_bias, proj_t)


def _mix_kernel(u_ref, halo_ref, gp_ref, ga_ref, ya_ref, h_ref, wp_ref, ps_ref, wo_ref,
                gain_post_ref, gain_next_ref, hres_out, hnorm_out):
    li = pl.program_id(1)
    tile = u_ref.shape[0]
    u = u_ref[...].astype(F32)
    halo = jnp.where(li > 0, halo_ref[...].astype(F32), 0.0)
    ext = jnp.concatenate([halo, u], axis=0)
    pos1 = li * tile + lax.broadcasted_iota(jnp.int32, (tile, 1), 0) + 1
    pooled = []
    for g, window in enumerate(POOL_WINDOWS):
        lanes = slice(g * POOL_GROUP_DIM, (g + 1) * POOL_GROUP_DIM)
        run = ext[:, lanes]
        width = 1
        while width < window:
            run = run[width:] + run[:-width]
            width *= 2
        first = POOL_HALO + 1 - window
        window_sum = run[first:first + tile]
        count = jnp.minimum(pos1, window).astype(F32)
        diff = window_sum / count - u[:, lanes]
        pooled.append(jnp.dot(diff.astype(BF16), wp_ref[g], preferred_element_type=F32))
    y_pool = jnp.concatenate(pooled, axis=1) * ps_ref[...]
    merged = (jax.nn.sigmoid(gp_ref[...].astype(F32)) * y_pool
              + jax.nn.sigmoid(ga_ref[...].astype(F32)) * ya_ref[...].astype(F32))
    mix = jnp.dot(merged.astype(BF16), wo_ref[...], preferred_element_type=F32)
    h_new = h_ref[...] + _rms_norm(mix, gain_post_ref[...])
    hres_out[...] = h_new
    hnorm_out[...] = _rms_norm(h_new, gain_next_ref[...]).astype(hnorm_out.dtype)


def _mix_call(proj3, y_attn, h_res3, w_pool, pool_scale, w_out, gain_post, gain_next):
    batch, length, _ = proj3.shape
    halo_blocks = ROW_TILE // POOL_HALO
    row_spec = lambda col: pl.BlockSpec((None, ROW_TILE, D_MODEL), lambda b, i: (b, i, col))
    const = lambda shape: pl.BlockSpec(shape, lambda b, i: (0,) * len(shape))
    return pl.pallas_call(
        _mix_kernel,
        grid=(batch, length // ROW_TILE),
        in_specs=[
            row_spec(COL_POOL),
            pl.BlockSpec((None, POOL_HALO, D_MODEL),
                         lambda b, i: (b, jnp.maximum(i * halo_blocks - 1, 0), COL_POOL)),
            row_spec(COL_GPOOL),
            row_spec(COL_GATTN),
            row_spec(0),
            row_spec(0),
            const((len(POOL_WINDOWS), POOL_GROUP_DIM, POOL_GROUP_DIM)),
            const((1, D_MODEL)),
            const((D_MODEL, D_MODEL)),
            const((1, D_MODEL)),
            const((1, D_MODEL)),
        ],
        out_specs=[row_spec(0), row_spec(0)],
        out_shape=[jax.ShapeDtypeStruct((batch, length, D_MODEL), F32),
                   jax.ShapeDtypeStruct((batch, length, D_MODEL), BF16)],
        compiler_params=_params("parallel", "parallel"),
        name="pool_merge_out",
    )(proj3, proj3, proj3, proj3, y_attn, h_res3, w_pool, pool_scale, w_out, gain_post, gain_next)


def _ffn_kernel(hn_ref, h_ref, wg_ref, wu_ref, wd_ref, gain_post_ref, gain_next_ref,
                hres_out, hnorm_out):
    hn = hn_ref[...]
    ff_out = jnp.zeros(h_ref.shape, F32)
    for c in range(D_FF // FF_CHUNK):
        cols = slice(c * FF_CHUNK, (c + 1) * FF_CHUNK)
        gate = jnp.dot(hn, wg_ref[:, cols], preferred_element_type=F32)
        up = jnp.dot(hn, wu_ref[:, cols], preferred_element_type=F32)
        ff = gate * jax.nn.sigmoid(gate) * up
        ff_out = ff_out + jnp.dot(ff.astype(BF16), wd_ref[cols, :], preferred_element_type=F32)
    h_new = h_ref[...] + _rms_norm(ff_out, gain_post_ref[...])
    hres_out[...] = h_new
    hnorm_out[...] = _rms_norm(h_new, gain_next_ref[...]).astype(hnorm_out.dtype)


def _ffn_call(h_norm, h_res, w_gate, w_up, w_down, gain_post, gain_next):
    rows = h_res.shape[0]
    row_spec = pl.BlockSpec((ROW_TILE, D_MODEL), lambda i: (i, 0))
    resident = lambda shape: pl.BlockSpec(shape, lambda i: (0, 0), pipeline_mode=pl.Buffered(1))
    return pl.pallas_call(
        _ffn_kernel,
        grid=(rows // ROW_TILE,),
        in_specs=[row_spec, row_spec,
                  resident((D_MODEL, D_FF)), resident((D_MODEL, D_FF)), resident((D_FF, D_MODEL)),
                  resident((1, D_MODEL)), resident((1, D_MODEL))],
        out_specs=[row_spec, row_spec],
        out_shape=[jax.ShapeDtypeStruct((rows, D_MODEL), F32),
                   jax.ShapeDtypeStruct((rows, D_MODEL), BF16)],
        compiler_params=_params("parallel"),
        name="swiglu_ffn",
    )(h_norm, h_res, w_gate, w_up, w_down, gain_post, gain_next)


def kernel(x, meta_tokens, norm_mix_pre, norm_mix_post, norm_ffn_pre, norm_ffn_post,
           w_in, b_forget, w_pool, pool_scale, w_out, w_ffn_gate, w_ffn_up, w_ffn_down):
    batch, seq, _ = x.shape
    depth = w_in.shape[0]
    tokens = META_TOKENS + seq
    length = -(-tokens // ROW_TILE) * ROW_TILE
    assert (batch * length) % PROJ_ROW_TILE == 0
    meta = jnp.broadcast_to(meta_tokens[None].astype(x.dtype), (batch, META_TOKENS, D_MODEL))
    tail = jnp.zeros((batch, length - tokens, D_MODEL), x.dtype)
    h_res = jnp.concatenate([meta, x, tail], axis=1).reshape(batch * length, D_MODEL)
    gain = lambda g: g.reshape(1, D_MODEL).astype(F32)
    sec = lambda w, n: w[:, n * D_MODEL:(n + 1) * D_MODEL]
    gates_at = 4 * D_MODEL + N_HEADS
    lane_head = jnp.minimum(jnp.arange(LANES) // BIAS_PIECES, N_HEADS - 1)
    head_indicator = (jnp.arange(D_MODEL)[:, None] // HEAD_DIM == jnp.arange(LANES)[None, :]).astype(BF16)

    h_norm = _norm_call(h_res, gain(norm_mix_pre[0]))
    for l in range(depth):
        w = w_in[l]
        w_rows = jnp.concatenate([sec(w, 0), sec(w, 2), w[:, gates_at:]], axis=1).astype(BF16)
        w_t = jnp.concatenate([sec(w, 1), sec(w, 3)], axis=1).T.astype(BF16)
        w_forget = w[:, 4 * D_MODEL:gates_at]
        wf_rep = w_forget[:, lane_head].astype(BF16)
        bias_rep = b_forget[l][lane_head].reshape(1, LANES).astype(F32)

        h_norm3 = h_norm.reshape(batch, length, D_MODEL)
        proj3 = _proj_call(h_norm, w_rows).reshape(batch, length, N_ROW_BLOCKS * D_MODEL)
        proj_t = _proj_t_call(h_norm3, w_t)
        key_bias, bias_end = _forget_call(h_norm3, wf_rep, bias_rep)
        key_norms = _key_norm_call(proj3, head_indicator)
        key_norm_max = key_norms[:, :, 0, :N_HEADS].reshape(-1)
        bias_max = bias_end[:, :, 0, 0:ONES_LANE:BIAS_PIECES].reshape(-1)
        y_attn = _attn_call(proj3, proj_t, key_bias, key_norm_max, bias_max)
        h_res3, h_norm3 = _mix_call(
            proj3, y_attn, h_res.reshape(batch, length, D_MODEL),
            w_pool[l].astype(BF16), gain(pool_scale[l]), w_out[l].astype(BF16),
            gain(norm_mix_post[l]), gain(norm_ffn_pre[l]))
        next_gain = norm_mix_pre[(l + 1) % depth]
        h_res, h_norm = _ffn_call(
            h_norm3.reshape(batch * length, D_MODEL), h_res3.reshape(batch * length, D_MODEL),
            w_ffn_gate[l].astype(BF16), w_ffn_up[l].astype(BF16), w_ffn_down[l].astype(BF16),
            gain(norm_ffn_post[l]), gain(next_gain))
    return h_res.reshape(batch, length, D_MODEL)[:, META_TOKENS:tokens]
```

```python
import math

import jax
import jax.numpy as jnp
from jax import lax
from jax.experimental import pallas as pl
from jax.experimental.pallas import tpu as pltpu

D_MODEL = 1024
N_HEADS = 16
HEAD_DIM = 64
META_TOKENS = 16
POOL_WINDOWS = (2, 4, 8, 16)
POOL_GROUP_DIM = D_MODEL // len(POOL_WINDOWS)
POOL_HALO = max(POOL_WINDOWS)
D_FF = 2816
FF_CHUNK = 1408
RMS_EPS = 1e-6
MASK_VALUE = -1e30
LOG2_E = math.log2(math.e)

LANES = 128
SUBLANES = 8
MXU_DIM = 256
ROW_TILE = 3 * MXU_DIM
CHUNK_COLS = 2 * MXU_DIM
PROJ_ROW_TILE = 2 * ROW_TILE
N_ROW_BLOCKS = 4
COL_POOL, COL_K, COL_GPOOL, COL_GATTN = range(N_ROW_BLOCKS)
N_T_BLOCKS = 2
T_Q, T_V = range(N_T_BLOCKS)
HEADS_PER_STEP = LANES // HEAD_DIM
BIAS_PIECES = 3
ONES_LANE = BIAS_PIECES * N_HEADS
REF_ROW = LANES + ONES_LANE
PACKED_ROWS = 16
NORM_SLACK = 1.01
BIAS_SLACK = 0.05
DENOM_FLOOR = 2.0 ** -80
VMEM_LIMIT_BYTES = 56 * 1024 * 1024

F32 = jnp.float32
BF16 = jnp.bfloat16


def _rms_norm(x, gain):
    return x * lax.rsqrt(jnp.mean(x * x, axis=-1, keepdims=True) + RMS_EPS) * gain


def _params(*semantics):
    return pltpu.CompilerParams(dimension_semantics=semantics, vmem_limit_bytes=VMEM_LIMIT_BYTES)


def _split_bf16(x):
    hi = x.astype(BF16)
    rest = x - hi.astype(F32)
    mid = rest.astype(BF16)
    lo = (rest - mid.astype(F32)).astype(BF16)
    return hi, mid, lo


def _norm_kernel(h_ref, gain_ref, out_ref):
    out_ref[...] = _rms_norm(h_ref[...], gain_ref[...]).astype(out_ref.dtype)


def _norm_call(h_res, gain):
    rows = h_res.shape[0]
    return pl.pallas_call(
        _norm_kernel,
        grid=(rows // ROW_TILE,),
        in_specs=[pl.BlockSpec((ROW_TILE, D_MODEL), lambda i: (i, 0)),
                  pl.BlockSpec((1, D_MODEL), lambda i: (0, 0))],
        out_specs=pl.BlockSpec((ROW_TILE, D_MODEL), lambda i: (i, 0)),
        out_shape=jax.ShapeDtypeStruct((rows, D_MODEL), BF16),
        compiler_params=_params("parallel"),
        name="pre_norm",
    )(h_res, gain)


def _proj_kernel(h_ref, w_ref, out_ref):
    out_ref[...] = jnp.dot(h_ref[...], w_ref[...], preferred_element_type=F32).astype(out_ref.dtype)


def _proj_call(h_norm, w_rows):
    rows = h_norm.shape[0]
    return pl.pallas_call(
        _proj_kernel,
        grid=(rows // PROJ_ROW_TILE, N_ROW_BLOCKS),
        in_specs=[pl.BlockSpec((PROJ_ROW_TILE, D_MODEL), lambda i, j: (i, 0)),
                  pl.BlockSpec((D_MODEL, D_MODEL), lambda i, j: (0, j))],
        out_specs=pl.BlockSpec((PROJ_ROW_TILE, D_MODEL), lambda i, j: (i, j)),
        out_shape=jax.ShapeDtypeStruct((rows, N_ROW_BLOCKS * D_MODEL), BF16),
        compiler_params=_params("parallel", "arbitrary"),
        name="in_proj",
    )(h_norm, w_rows)


def _proj_t_kernel(h_ref, wt_ref, out_ref):
    acc = lax.dot_general(wt_ref[...], h_ref[...], (((1,), (1,)), ((), ())),
                          preferred_element_type=F32)
    scale = jnp.where(pl.program_id(2) == T_Q, LOG2_E * HEAD_DIM ** -0.5, 1.0).astype(F32)
    out_ref[...] = (acc * scale).astype(out_ref.dtype)


def _proj_t_call(h_norm3, w_t):
    batch, length, _ = h_norm3.shape
    return pl.pallas_call(
        _proj_t_kernel,
        grid=(batch, length // ROW_TILE, N_T_BLOCKS),
        in_specs=[pl.BlockSpec((None, ROW_TILE, D_MODEL), lambda b, i, j: (b, i, 0)),
                  pl.BlockSpec((D_MODEL, D_MODEL), lambda b, i, j: (j, 0))],
        out_specs=pl.BlockSpec((None, D_MODEL, ROW_TILE), lambda b, i, j: (b, j, i)),
        out_shape=jax.ShapeDtypeStruct((batch, N_T_BLOCKS * D_MODEL, length), BF16),
        compiler_params=_params("parallel", "parallel", "arbitrary"),
        name="in_proj_t",
    )(h_norm3, w_t)


def _forget_kernel(h_ref, wf_ref, bias_ref, out_ref, end_ref, carry_ref):
    @pl.when(pl.program_id(1) == 0)
    def _():
        carry_ref[...] = jnp.zeros_like(carry_ref)

    logit = jnp.dot(h_ref[...], wf_ref[...], preferred_element_type=F32) + bias_ref[...]
    log_f = jnp.minimum(logit, 0.0) - jnp.log(1.0 + jnp.exp(-jnp.abs(logit)))
    tile = log_f.shape[0]
    dst = lax.broadcasted_iota(jnp.int32, (tile, tile), 0)
    src = lax.broadcasted_iota(jnp.int32, (tile, tile), 1)
    prefix = (src <= dst).astype(BF16)
    total = jnp.broadcast_to(carry_ref[...], log_f.shape)
    for piece in _split_bf16(log_f):
        total = total + jnp.dot(prefix, piece, preferred_element_type=F32)
    carry_ref[...] = total[tile - 1:tile, :]
    bias = -LOG2_E * total
    end_ref[...] = jnp.broadcast_to(bias[tile - 1:tile, :], end_ref.shape)
    hi, mid, lo = _split_bf16(bias)
    lane = lax.broadcasted_iota(jnp.int32, total.shape, 1)
    piece = lane % BIAS_PIECES
    out = jnp.where(piece == 0, hi, jnp.where(piece == 1, mid, lo))
    tail = jnp.where(lane < ONES_LANE + BIAS_PIECES, 1.0, 0.0).astype(BF16)
    out_ref[...] = jnp.where(lane < ONES_LANE, out, tail)


def _forget_call(h_norm3, wf_rep, bias_rep):
    batch, length, _ = h_norm3.shape
    n_tiles = length // ROW_TILE
    return pl.pallas_call(
        _forget_kernel,
        grid=(batch, n_tiles),
        in_specs=[pl.BlockSpec((None, ROW_TILE, D_MODEL), lambda b, i: (b, i, 0)),
                  pl.BlockSpec((D_MODEL, LANES), lambda b, i: (0, 0)),
                  pl.BlockSpec((1, LANES), lambda b, i: (0, 0))],
        out_specs=[pl.BlockSpec((None, ROW_TILE, LANES), lambda b, i: (b, i, 0)),
                   pl.BlockSpec((None, None, SUBLANES, LANES), lambda b, i: (b, i, 0, 0))],
        out_shape=[jax.ShapeDtypeStruct((batch, length, LANES), BF16),
                   jax.ShapeDtypeStruct((batch, n_tiles, SUBLANES, LANES), F32)],
        scratch_shapes=[pltpu.VMEM((1, LANES), F32)],
        compiler_params=_params("parallel", "arbitrary"),
        name="forget_bias",
    )(h_norm3, wf_rep, bias_rep)


def _key_norm_kernel(k_ref, head_ref, out_ref):
    k = k_ref[...].astype(F32)
    sums = jnp.dot((k * k).astype(BF16), head_ref[...], preferred_element_type=F32)
    out_ref[...] = jnp.broadcast_to(jnp.sqrt(jnp.max(sums, axis=0, keepdims=True)), out_ref.shape)


def _key_norm_call(proj3, head_indicator):
    batch, length, _ = proj3.shape
    n_tiles = length // ROW_TILE
    return pl.pallas_call(
        _key_norm_kernel,
        grid=(batch, n_tiles),
        in_specs=[pl.BlockSpec((None, ROW_TILE, D_MODEL), lambda b, i: (b, i, COL_K)),
                  pl.BlockSpec((D_MODEL, LANES), lambda b, i: (0, 0))],
        out_specs=pl.BlockSpec((None, None, SUBLANES, LANES), lambda b, i: (b, i, 0, 0)),
        out_shape=jax.ShapeDtypeStruct((batch, n_tiles, SUBLANES, LANES), F32),
        compiler_params=_params("parallel", "parallel"),
        name="key_norms",
    )(proj3, head_indicator)


def _attn_kernel(kmax_ref, gend_ref, qt_ref, k_ref, fb_ref, vt_ref, out_ref,
                 acc_ref, qaug_ref, *chunk_refs):
    bi = pl.program_id(0)
    pair = pl.program_id(1)
    qi = pl.program_id(2)
    n_tiles = pl.num_programs(2)
    tile = qt_ref.shape[1]
    both = HEADS_PER_STEP * tile
    n_chunks = both // CHUNK_COLS
    s_refs, p_refs = chunk_refs[:n_chunks], chunk_refs[n_chunks:]

    feat = lax.broadcasted_iota(jnp.int32, (LANES, tile), 0)
    q_norms = []
    for hh in range(HEADS_PER_STEP):
        head = pair * HEADS_PER_STEP + hh
        in_head = (feat >= hh * HEAD_DIM) & (feat < (hh + 1) * HEAD_DIM)
        q_top = jnp.where(in_head, qt_ref[...], jnp.zeros_like(qt_ref[...]))
        is_piece = (feat >= head * BIAS_PIECES) & (feat < (head + 1) * BIAS_PIECES)
        qaug_ref[:, hh * tile:(hh + 1) * tile] = jnp.concatenate(
            [q_top, is_piece.astype(BF16)], axis=0)
        q_head = qt_ref[hh * HEAD_DIM:(hh + 1) * HEAD_DIM, :].astype(F32)
        q_norms.append(jnp.sqrt(jnp.sum(q_head * q_head, axis=0, keepdims=True)))
    q_norm = jnp.concatenate(q_norms, axis=1)
    second_head = lax.broadcasted_iota(jnp.int32, (1, both), 1) >= tile

    def head_segments(c):
        segments = []
        start = c * CHUNK_COLS
        while start < (c + 1) * CHUNK_COLS:
            hh = start // tile
            stop = min((c + 1) * CHUNK_COLS, (hh + 1) * tile)
            segments.append((hh, slice(start - hh * tile, stop - hh * tile),
                             slice(start - c * CHUNK_COLS, stop - c * CHUNK_COLS)))
            start = stop
        return segments

    def scores(c, k_start, on_diagonal):
        keys = jnp.concatenate([k_ref[pl.ds(k_start, tile), :],
                                fb_ref[pl.ds(k_start, tile), :]], axis=1)
        s_t = jnp.dot(keys, qaug_ref[:, c * CHUNK_COLS:(c + 1) * CHUNK_COLS],
                      preferred_element_type=F32)
        if on_diagonal:
            key_row = lax.broadcasted_iota(jnp.int32, s_t.shape, 0)
            col = lax.broadcasted_iota(jnp.int32, s_t.shape, 1) + c * CHUNK_COLS
            query_col = jnp.where(col >= tile, col - tile, col)
            s_t = jnp.where(key_row <= query_col, s_t, MASK_VALUE)
        return s_t

    def value_stage(c, k_start, alpha):
        for hh, head_cols, chunk_cols in head_segments(c):
            v_aug = jnp.concatenate(
                [vt_ref[hh * HEAD_DIM:(hh + 1) * HEAD_DIM, pl.ds(k_start, tile)],
                 jnp.ones((PACKED_ROWS, tile), BF16)], axis=0)
            acc_ref[hh, :, head_cols] = (
                alpha[:, c * CHUNK_COLS:(c + 1) * CHUNK_COLS][:, chunk_cols]
                * acc_ref[hh, :, head_cols]
                + jnp.dot(v_aug, p_refs[c][:, chunk_cols], preferred_element_type=F32))

    def set_reference(r):
        hi, mid, lo = (piece.astype(F32) for piece in _split_bf16(-r))
        row = lax.broadcasted_iota(jnp.int32, (PACKED_ROWS, both), 0)
        pieces = jnp.where(row == 0, hi, jnp.where(row == 1, mid, jnp.where(row == 2, lo, 0.0)))
        qaug_ref[REF_ROW:REF_ROW + PACKED_ROWS, :] = pieces.astype(BF16)

    def bound_tile(kj, r_prev, on_diagonal):
        k_start = pl.multiple_of(kj * tile, LANES)
        base = (bi * n_tiles + kj) * N_HEADS + pair * HEADS_PER_STEP
        k_max = jnp.where(second_head, kmax_ref[base + 1], kmax_ref[base])
        if on_diagonal:
            lane = lax.broadcasted_iota(jnp.int32, (PACKED_ROWS, LANES), 1)
            head_row = pair * HEADS_PER_STEP + lax.broadcasted_iota(
                jnp.int32, (PACKED_ROWS, LANES), 0)
            pick = ((lane >= head_row * BIAS_PIECES) & (lane < (head_row + 1) * BIAS_PIECES))
            g_rows = lax.dot_general(jnp.where(pick, 1.0, 0.0).astype(BF16),
                                     fb_ref[pl.ds(k_start, tile), :], (((1,), (1,)), ((), ())),
                                     preferred_element_type=F32)
            g_max = jnp.concatenate([g_rows[hh:hh + 1, :] for hh in range(HEADS_PER_STEP)], axis=1)
        else:
            g_max = jnp.where(second_head, gend_ref[base + 1], gend_ref[base])
        r_new = jnp.maximum(r_prev, q_norm * k_max * NORM_SLACK + g_max + BIAS_SLACK)
        alpha = jnp.exp2(r_prev - r_new)
        set_reference(r_new)

        if on_diagonal:
            diagonal_groups(k_start, alpha)
            return r_new

        def probabilities(c):
            p_refs[c][...] = jnp.exp2(scores(c, k_start, False)).astype(BF16)

        probabilities(0)
        probabilities(1)
        value_stage(0, k_start, alpha)
        probabilities(2)
        value_stage(1, k_start, alpha)
        value_stage(2, k_start, alpha)
        return r_new

    def diagonal_groups(k_start, alpha):
        n_groups = tile // MXU_DIM
        assert n_groups == n_chunks

        def group_cols(hh, g):
            return slice(hh * tile + g * MXU_DIM, hh * tile + (g + 1) * MXU_DIM)

        def probabilities(g):
            n_keys = (g + 1) * MXU_DIM
            keys = jnp.concatenate([k_ref[pl.ds(k_start, n_keys), :],
                                    fb_ref[pl.ds(k_start, n_keys), :]], axis=1)
            queries = jnp.concatenate(
                [qaug_ref[:, group_cols(hh, g)] for hh in range(HEADS_PER_STEP)], axis=1)
            s_t = jnp.dot(keys, queries, preferred_element_type=F32)
            key_row = lax.broadcasted_iota(jnp.int32, s_t.shape, 0)
            col = lax.broadcasted_iota(jnp.int32, s_t.shape, 1)
            query_col = jnp.where(col >= MXU_DIM, col - MXU_DIM, col) + g * MXU_DIM
            s_t = jnp.where(key_row <= query_col, s_t, MASK_VALUE)
            p_refs[g][0:n_keys, :] = jnp.exp2(s_t).astype(BF16)

        def values(g):
            n_keys = (g + 1) * MXU_DIM
            for hh in range(HEADS_PER_STEP):
                head_cols = slice(g * MXU_DIM, (g + 1) * MXU_DIM)
                v_aug = jnp.concatenate(
                    [vt_ref[hh * HEAD_DIM:(hh + 1) * HEAD_DIM, pl.ds(k_start, n_keys)],
                     jnp.ones((PACKED_ROWS, n_keys), BF16)], axis=0)
                acc_ref[hh, :, head_cols] = (
                    alpha[:, group_cols(hh, g)] * acc_ref[hh, :, head_cols]
                    + jnp.dot(v_aug, p_refs[g][0:n_keys, hh * MXU_DIM:(hh + 1) * MXU_DIM],
                              preferred_element_type=F32))

        probabilities(2)
        probabilities(1)
        values(2)
        probabilities(0)
        values(1)
        values(0)

    assert n_chunks == 3
    acc_ref[...] = jnp.zeros_like(acc_ref)
    r_run = jnp.full((1, both), MASK_VALUE, F32)
    r_run = lax.fori_loop(
        0, qi // 2,
        lambda kk, r: bound_tile(2 * kk + 1, bound_tile(2 * kk, r, False), False), r_run)
    r_run = lax.cond(qi % 2 == 1, lambda r: bound_tile(qi - 1, r, False), lambda r: r, r_run)
    bound_tile(qi, r_run, True)
    smallest_denominator = jnp.min(acc_ref[:, HEAD_DIM:HEAD_DIM + 1, :])

    @pl.when(jnp.logical_not(smallest_denominator > DENOM_FLOOR))
    def _():
        def max_tile(kj, m_prev, on_diagonal):
            k_start = pl.multiple_of(kj * tile, LANES)
            m_tile = []
            for c in range(n_chunks):
                s_t = scores(c, k_start, on_diagonal)
                s_refs[c][...] = s_t
                m_tile.append(jnp.max(s_t, axis=0, keepdims=True))
            m_new = jnp.maximum(m_prev, jnp.concatenate(m_tile, axis=1))
            alpha = jnp.exp2(m_prev - m_new)
            for c in range(n_chunks):
                cols = slice(c * CHUNK_COLS, (c + 1) * CHUNK_COLS)
                p_refs[c][...] = jnp.exp2(s_refs[c][...] - m_new[:, cols]).astype(BF16)
                value_stage(c, k_start, alpha)
            return m_new

        acc_ref[...] = jnp.zeros_like(acc_ref)
        set_reference(jnp.zeros((1, both), F32))
        m_run = jnp.full((1, both), MASK_VALUE, F32)
        m_run = lax.fori_loop(0, qi, lambda kj, m: max_tile(kj, m, False), m_run)
        max_tile(qi, m_run, True)

    heads_out = [acc_ref[hh, :HEAD_DIM, :] / acc_ref[hh, HEAD_DIM:HEAD_DIM + 1, :]
                 for hh in range(HEADS_PER_STEP)]
    out_ref[...] = jnp.concatenate(heads_out, axis=0).T.astype(out_ref.dtype)


def _attn_call(proj3, proj_t, key_bias, key_norm_max, bias_max):
    batch, length, _ = proj3.shape
    pairs = N_HEADS // HEADS_PER_STEP
    blocks_per_section = D_MODEL // LANES
    n_chunks = HEADS_PER_STEP * ROW_TILE // CHUNK_COLS
    return pl.pallas_call(
        _attn_kernel,
        grid=(batch, pairs, length // ROW_TILE),
        in_specs=[
            pl.BlockSpec(memory_space=pltpu.SMEM),
            pl.BlockSpec(memory_space=pltpu.SMEM),
            pl.BlockSpec((None, LANES, ROW_TILE), lambda b, p, i: (b, T_Q * blocks_per_section + p, i)),
            pl.BlockSpec((None, length, LANES), lambda b, p, i: (b, 0, COL_K * blocks_per_section + p)),
            pl.BlockSpec((None, length, LANES), lambda b, p, i: (b, 0, 0)),
            pl.BlockSpec((None, LANES, length), lambda b, p, i: (b, T_V * blocks_per_section + p, 0)),
        ],
        out_specs=pl.BlockSpec((None, ROW_TILE, LANES), lambda b, p, i: (b, i, p)),
        out_shape=jax.ShapeDtypeStruct((batch, length, D_MODEL), BF16),
        scratch_shapes=[
            pltpu.VMEM((HEADS_PER_STEP, HEAD_DIM + PACKED_ROWS, ROW_TILE), F32),
            pltpu.VMEM((2 * LANES, HEADS_PER_STEP * ROW_TILE), BF16),
        ] + [pltpu.VMEM((ROW_TILE, CHUNK_COLS), dtype) for dtype in (F32, BF16)
             for _ in range(n_chunks)],
        compiler_params=_params("parallel", "parallel", "arbitrary"),
        name="forgetting_attention",
    )(key_norm_max, bias_max, proj_t, proj3, key_bias, proj_t)


def _mix_kernel(u_ref, halo_ref, gp_ref, ga_ref, ya_ref, h_ref, wp_ref, ps_ref, wo_ref,
                gain_post_ref, gain_next_ref, hres_out, hnorm_out):
    li = pl.program_id(1)
    tile = u_ref.shape[0]
    u = u_ref[...].astype(F32)
    halo = jnp.where(li > 0, halo_ref[...].astype(F32), 0.0)
    ext = jnp.concatenate([halo, u], axis=0)
    pos1 = li * tile + lax.broadcasted_iota(jnp.int32, (tile, 1), 0) + 1
    pooled = []
    for g, window in enumerate(POOL_WINDOWS):
        lanes = slice(g * POOL_GROUP_DIM, (g + 1) * POOL_GROUP_DIM)
        run = ext[:, lanes]
        width = 1
        while width < window:
            run = run[width:] + run[:-width]
            width *= 2
        first = POOL_HALO + 1 - window
        window_sum = run[first:first + tile]
        count = jnp.minimum(pos1, window).astype(F32)
        diff = window_sum / count - u[:, lanes]
        pooled.append(jnp.dot(diff.astype(BF16), wp_ref[g], preferred_element_type=F32))
    y_pool = jnp.concatenate(pooled, axis=1) * ps_ref[...]
    merged = (jax.nn.sigmoid(gp_ref[...].astype(F32)) * y_pool
              + jax.nn.sigmoid(ga_ref[...].astype(F32)) * ya_ref[...].astype(F32))
    mix = jnp.dot(merged.astype(BF16), wo_ref[...], preferred_element_type=F32)
    h_new = h_ref[...] + _rms_norm(mix, gain_post_ref[...])
    hres_out[...] = h_new
    hnorm_out[...] = _rms_norm(h_new, gain_next_ref[...]).astype(hnorm_out.dtype)


def _mix_call(proj3, y_attn, h_res3, w_pool, pool_scale, w_out, gain_post, gain_next):
    batch, length, _ = proj3.shape
    halo_blocks = ROW_TILE // POOL_HALO
    row_spec = lambda col: pl.BlockSpec((None, ROW_TILE, D_MODEL), lambda b, i: (b, i, col))
    const = lambda shape: pl.BlockSpec(shape, lambda b, i: (0,) * len(shape))
    return pl.pallas_call(
        _mix_kernel,
        grid=(batch, length // ROW_TILE),
        in_specs=[
            row_spec(COL_POOL),
            pl.BlockSpec((None, POOL_HALO, D_MODEL),
                         lambda b, i: (b, jnp.maximum(i * halo_blocks - 1, 0), COL_POOL)),
            row_spec(COL_GPOOL),
            row_spec(COL_GATTN),
            row_spec(0),
            row_spec(0),
            const((len(POOL_WINDOWS), POOL_GROUP_DIM, POOL_GROUP_DIM)),
            const((1, D_MODEL)),
            const((D_MODEL, D_MODEL)),
            const((1, D_MODEL)),
            const((1, D_MODEL)),
        ],
        out_specs=[row_spec(0), row_spec(0)],
        out_shape=[jax.ShapeDtypeStruct((batch, length, D_MODEL), F32),
                   jax.ShapeDtypeStruct((batch, length, D_MODEL), BF16)],
        compiler_params=_params("parallel", "parallel"),
        name="pool_merge_out",
    )(proj3, proj3, proj3, proj3, y_attn, h_res3, w_pool, pool_scale, w_out, gain_post, gain_next)


def _ffn_kernel(hn_ref, h_ref, wg_ref, wu_ref, wd_ref, gain_post_ref, gain_next_ref,
                hres_out, hnorm_out):
    hn = hn_ref[...]
    ff_out = jnp.zeros(h_ref.shape, F32)
    for c in range(D_FF // FF_CHUNK):
        cols = slice(c * FF_CHUNK, (c + 1) * FF_CHUNK)
        gate = jnp.dot(hn, wg_ref[:, cols], preferred_element_type=F32)
        up = jnp.dot(hn, wu_ref[:, cols], preferred_element_type=F32)
        ff = gate * jax.nn.sigmoid(gate) * up
        ff_out = ff_out + jnp.dot(ff.astype(BF16), wd_ref[cols, :], preferred_element_type=F32)
    h_new = h_ref[...] + _rms_norm(ff_out, gain_post_ref[...])
    hres_out[...] = h_new
    hnorm_out[...] = _rms_norm(h_new, gain_next_ref[...]).astype(hnorm_out.dtype)


def _ffn_call(h_norm, h_res, w_gate, w_up, w_down, gain_post, gain_next):
    rows = h_res.shape[0]
    row_spec = pl.BlockSpec((ROW_TILE, D_MODEL), lambda i: (i, 0))
    resident = lambda shape: pl.BlockSpec(shape, lambda i: (0, 0), pipeline_mode=pl.Buffered(1))
    return pl.pallas_call(
        _ffn_kernel,
        grid=(rows // ROW_TILE,),
        in_specs=[row_spec, row_spec,
                  resident((D_MODEL, D_FF)), resident((D_MODEL, D_FF)), resident((D_FF, D_MODEL)),
                  resident((1, D_MODEL)), resident((1, D_MODEL))],
        out_specs=[row_spec, row_spec],
        out_shape=[jax.ShapeDtypeStruct((rows, D_MODEL), F32),
                   jax.ShapeDtypeStruct((rows, D_MODEL), BF16)],
        compiler_params=_params("parallel"),
        name="swiglu_ffn",
    )(h_norm, h_res, w_gate, w_up, w_down, gain_post, gain_next)


def kernel(x, meta_tokens, norm_mix_pre, norm_mix_post, norm_ffn_pre, norm_ffn_post,
           w_in, b_forget, w_pool, pool_scale, w_out, w_ffn_gate, w_ffn_up, w_ffn_down):
    batch, seq, _ = x.shape
    depth = w_in.shape[0]
    tokens = META_TOKENS + seq
    length = -(-tokens // ROW_TILE) * ROW_TILE
    assert (batch * length) % PROJ_ROW_TILE == 0
    meta = jnp.broadcast_to(meta_tokens[None].astype(x.dtype), (batch, META_TOKENS, D_MODEL))
    tail = jnp.zeros((batch, length - tokens, D_MODEL), x.dtype)
    h_res = jnp.concatenate([meta, x, tail], axis=1).reshape(batch * length, D_MODEL)
    gain = lambda g: g.reshape(1, D_MODEL).astype(F32)
    sec = lambda w, n: w[:, n * D_MODEL:(n + 1) * D_MODEL]
    gates_at = 4 * D_MODEL + N_HEADS
    lane_head = jnp.minimum(jnp.arange(LANES) // BIAS_PIECES, N_HEADS - 1)
    head_indicator = (jnp.arange(D_MODEL)[:, None] // HEAD_DIM == jnp.arange(LANES)[None, :]).astype(BF16)

    h_norm = _norm_call(h_res, gain(norm_mix_pre[0]))
    for l in range(depth):
        w = w_in[l]
        w_rows = jnp.concatenate([sec(w, 0), sec(w, 2), w[:, gates_at:]], axis=1).astype(BF16)
        w_t = jnp.concatenate([sec(w, 1), sec(w, 3)], axis=1).T.astype(BF16)
        w_forget = w[:, 4 * D_MODEL:gates_at]
        wf_rep = w_forget[:, lane_head].astype(BF16)
        bias_rep = b_forget[l][lane_head].reshape(1, LANES).astype(F32)

        h_norm3 = h_norm.reshape(batch, length, D_MODEL)
        proj3 = _proj_call(h_norm, w_rows).reshape(batch, length, N_ROW_BLOCKS * D_MODEL)
        proj_t = _proj_t_call(h_norm3, w_t)
        key_bias, bias_end = _forget_call(h_norm3, wf_rep, bias_rep)
        key_norms = _key_norm_call(proj3, head_indicator)
        key_norm_max = key_norms[:, :, 0, :N_HEADS].reshape(-1)
        bias_max = bias_end[:, :, 0, 0:ONES_LANE:BIAS_PIECES].reshape(-1)
        y_attn = _attn_call(proj3, proj_t, key_bias, key_norm_max, bias_max)
        h_res3, h_norm3 = _mix_call(
            proj3, y_attn, h_res.reshape(batch, length, D_MODEL),
            w_pool[l].astype(BF16), gain(pool_scale[l]), w_out[l].astype(BF16),
            gain(norm_mix_post[l]), gain(norm_ffn_pre[l]))
        next_gain = norm_mix_pre[(l + 1) % depth]
        h_res, h_norm = _ffn_call(
            h_norm3.reshape(batch * length, D_MODEL), h_res3.reshape(batch * length, D_MODEL),
            w_ffn_gate[l].astype(BF16), w_ffn_up[l].astype(BF16), w_ffn_down[l].astype(BF16),
            gain(norm_ffn_post[l]), gain(next_gain))
    return h_res.reshape(batch, length, D_MODEL)[:, META_TOKENS:tokens]
```

```python
import math

import jax
import jax.numpy as jnp
from jax import lax
from jax.experimental import pallas as pl
from jax.experimental.pallas import tpu as pltpu

D_MODEL = 1024
N_HEADS = 16
HEAD_DIM = 64
META_TOKENS = 16
POOL_WINDOWS = (2, 4, 8, 16)
POOL_GROUP_DIM = D_MODEL // len(POOL_WINDOWS)
POOL_HALO = max(POOL_WINDOWS)
D_FF = 2816
FF_CHUNK = 1408
RMS_EPS = 1e-6
MASK_VALUE = -1e30
LOG2_E = math.log2(math.e)

LANES = 128
SUBLANES = 8
MXU_DIM = 256
ROW_TILE = 3 * MXU_DIM
CHUNK_COLS = 2 * MXU_DIM
TILES_PER_TRIP = 4
MIX_BLOCK = MXU_DIM
PROJ_ROW_TILE = 2 * ROW_TILE
N_ROW_BLOCKS = 4
COL_POOL, COL_K, COL_GPOOL, COL_GATTN = range(N_ROW_BLOCKS)
N_T_BLOCKS = 2
T_Q, T_V = range(N_T_BLOCKS)
HEADS_PER_STEP = LANES // HEAD_DIM
BIAS_PIECES = 3
ONES_LANE = BIAS_PIECES * N_HEADS
REF_ROW = LANES + ONES_LANE
PACKED_ROWS = 16
NORM_SLACK = 1.01
BIAS_SLACK = 0.05
DENOM_FLOOR = 2.0 ** -80
VMEM_LIMIT_BYTES = 56 * 1024 * 1024

F32 = jnp.float32
BF16 = jnp.bfloat16


def _rms_norm(x, gain):
    return x * lax.rsqrt(jnp.mean(x * x, axis=-1, keepdims=True) + RMS_EPS) * gain


def _params(*semantics):
    return pltpu.CompilerParams(dimension_semantics=semantics, vmem_limit_bytes=VMEM_LIMIT_BYTES)


def _split_bf16(x):
    hi = x.astype(BF16)
    rest = x - hi.astype(F32)
    mid = rest.astype(BF16)
    lo = (rest - mid.astype(F32)).astype(BF16)
    return hi, mid, lo


def _norm_kernel(h_ref, gain_ref, out_ref):
    out_ref[...] = _rms_norm(h_ref[...], gain_ref[...]).astype(out_ref.dtype)


def _norm_call(h_res, gain):
    rows = h_res.shape[0]
    return pl.pallas_call(
        _norm_kernel,
        grid=(rows // ROW_TILE,),
        in_specs=[pl.BlockSpec((ROW_TILE, D_MODEL), lambda i: (i, 0)),
                  pl.BlockSpec((1, D_MODEL), lambda i: (0, 0))],
        out_specs=pl.BlockSpec((ROW_TILE, D_MODEL), lambda i: (i, 0)),
        out_shape=jax.ShapeDtypeStruct((rows, D_MODEL), BF16),
        compiler_params=_params("parallel"),
        name="pre_norm",
    )(h_res, gain)


def _proj_kernel(h_ref, w_ref, out_ref):
    out_ref[...] = jnp.dot(h_ref[...], w_ref[...], preferred_element_type=F32).astype(out_ref.dtype)


def _proj_call(h_norm, w_rows):
    rows = h_norm.shape[0]
    return pl.pallas_call(
        _proj_kernel,
        grid=(rows // PROJ_ROW_TILE, N_ROW_BLOCKS),
        in_specs=[pl.BlockSpec((PROJ_ROW_TILE, D_MODEL), lambda i, j: (i, 0)),
                  pl.BlockSpec((D_MODEL, D_MODEL), lambda i, j: (0, j))],
        out_specs=pl.BlockSpec((PROJ_ROW_TILE, D_MODEL), lambda i, j: (i, j)),
        out_shape=jax.ShapeDtypeStruct((rows, N_ROW_BLOCKS * D_MODEL), BF16),
        compiler_params=_params("parallel", "arbitrary"),
        name="in_proj",
    )(h_norm, w_rows)


def _proj_t_kernel(h_ref, wt_ref, out_ref):
    acc = lax.dot_general(wt_ref[...], h_ref[...], (((1,), (1,)), ((), ())),
                          preferred_element_type=F32)
    scale = jnp.where(pl.program_id(2) == T_Q, LOG2_E * HEAD_DIM ** -0.5, 1.0).astype(F32)
    out_ref[...] = (acc * scale).astype(out_ref.dtype)


def _proj_t_call(h_norm3, w_t):
    batch, length, _ = h_norm3.shape
    return pl.pallas_call(
        _proj_t_kernel,
        grid=(batch, length // ROW_TILE, N_T_BLOCKS),
        in_specs=[pl.BlockSpec((None, ROW_TILE, D_MODEL), lambda b, i, j: (b, i, 0)),
                  pl.BlockSpec((D_MODEL, D_MODEL), lambda b, i, j: (j, 0))],
        out_specs=pl.BlockSpec((None, D_MODEL, ROW_TILE), lambda b, i, j: (b, j, i)),
        out_shape=jax.ShapeDtypeStruct((batch, N_T_BLOCKS * D_MODEL, length), BF16),
        compiler_params=_params("parallel", "parallel", "arbitrary"),
        name="in_proj_t",
    )(h_norm3, w_t)


def _forget_kernel(h_ref, wf_ref, bias_ref, out_ref, end_ref, carry_ref):
    @pl.when(pl.program_id(1) == 0)
    def _():
        carry_ref[...] = jnp.zeros_like(carry_ref)

    logit = jnp.dot(h_ref[...], wf_ref[...], preferred_element_type=F32) + bias_ref[...]
    log_f = jnp.minimum(logit, 0.0) - jnp.log(1.0 + jnp.exp(-jnp.abs(logit)))
    tile = log_f.shape[0]
    dst = lax.broadcasted_iota(jnp.int32, (tile, tile), 0)
    src = lax.broadcasted_iota(jnp.int32, (tile, tile), 1)
    prefix = (src <= dst).astype(BF16)
    total = jnp.broadcast_to(carry_ref[...], log_f.shape)
    for piece in _split_bf16(log_f):
        total = total + jnp.dot(prefix, piece, preferred_element_type=F32)
    carry_ref[...] = total[tile - 1:tile, :]
    bias = -LOG2_E * total
    end_ref[...] = jnp.broadcast_to(bias[tile - 1:tile, :], end_ref.shape)
    hi, mid, lo = _split_bf16(bias)
    lane = lax.broadcasted_iota(jnp.int32, total.shape, 1)
    piece = lane % BIAS_PIECES
    out = jnp.where(piece == 0, hi, jnp.where(piece == 1, mid, lo))
    tail = jnp.where(lane < ONES_LANE + BIAS_PIECES, 1.0, 0.0).astype(BF16)
    out_ref[...] = jnp.where(lane < ONES_LANE, out, tail)


def _forget_call(h_norm3, wf_rep, bias_rep):
    batch, length, _ = h_norm3.shape
    n_tiles = length // ROW_TILE
    return pl.pallas_call(
        _forget_kernel,
        grid=(batch, n_tiles),
        in_specs=[pl.BlockSpec((None, ROW_TILE, D_MODEL), lambda b, i: (b, i, 0)),
                  pl.BlockSpec((D_MODEL, LANES), lambda b, i: (0, 0)),
                  pl.BlockSpec((1, LANES), lambda b, i: (0, 0))],
        out_specs=[pl.BlockSpec((None, ROW_TILE, LANES), lambda b, i: (b, i, 0)),
                   pl.BlockSpec((None, None, SUBLANES, LANES), lambda b, i: (b, i, 0, 0))],
        out_shape=[jax.ShapeDtypeStruct((batch, length, LANES), BF16),
                   jax.ShapeDtypeStruct((batch, n_tiles, SUBLANES, LANES), F32)],
        scratch_shapes=[pltpu.VMEM((1, LANES), F32)],
        compiler_params=_params("parallel", "arbitrary"),
        name="forget_bias",
    )(h_norm3, wf_rep, bias_rep)


def _key_norm_kernel(k_ref, head_ref, out_ref):
    k = k_ref[...].astype(F32)
    sums = jnp.dot((k * k).astype(BF16), head_ref[...], preferred_element_type=F32)
    out_ref[...] = jnp.broadcast_to(jnp.sqrt(jnp.max(sums, axis=0, keepdims=True)), out_ref.shape)


def _key_norm_call(proj3, head_indicator):
    batch, length, _ = proj3.shape
    n_tiles = length // ROW_TILE
    return pl.pallas_call(
        _key_norm_kernel,
        grid=(batch, n_tiles),
        in_specs=[pl.BlockSpec((None, ROW_TILE, D_MODEL), lambda b, i: (b, i, COL_K)),
                  pl.BlockSpec((D_MODEL, LANES), lambda b, i: (0, 0))],
        out_specs=pl.BlockSpec((None, None, SUBLANES, LANES), lambda b, i: (b, i, 0, 0)),
        out_shape=jax.ShapeDtypeStruct((batch, n_tiles, SUBLANES, LANES), F32),
        compiler_params=_params("parallel", "parallel"),
        name="key_norms",
    )(proj3, head_indicator)


def _attn_kernel(kmax_ref, gend_ref, qt_ref, k_ref, fb_ref, vt_ref, out_ref,
                 acc_ref, qaug_ref, *chunk_refs):
    bi = pl.program_id(0)
    pair = pl.program_id(1)
    qi = pl.program_id(2)
    n_tiles = pl.num_programs(2)
    tile = qt_ref.shape[1]
    both = HEADS_PER_STEP * tile
    n_chunks = both // CHUNK_COLS
    s_refs, p_refs = chunk_refs[:n_chunks], chunk_refs[n_chunks:]

    feat = lax.broadcasted_iota(jnp.int32, (LANES, tile), 0)
    q_norms = []
    for hh in range(HEADS_PER_STEP):
        head = pair * HEADS_PER_STEP + hh
        in_head = (feat >= hh * HEAD_DIM) & (feat < (hh + 1) * HEAD_DIM)
        q_top = jnp.where(in_head, qt_ref[...], jnp.zeros_like(qt_ref[...]))
        is_piece = (feat >= head * BIAS_PIECES) & (feat < (head + 1) * BIAS_PIECES)
        qaug_ref[:, hh * tile:(hh + 1) * tile] = jnp.concatenate(
            [q_top, is_piece.astype(BF16)], axis=0)
        q_head = qt_ref[hh * HEAD_DIM:(hh + 1) * HEAD_DIM, :].astype(F32)
        q_norms.append(jnp.sqrt(jnp.sum(q_head * q_head, axis=0, keepdims=True)))
    q_norm = jnp.concatenate(q_norms, axis=1)
    second_head = lax.broadcasted_iota(jnp.int32, (1, both), 1) >= tile

    def head_segments(c):
        segments = []
        start = c * CHUNK_COLS
        while start < (c + 1) * CHUNK_COLS:
            hh = start // tile
            stop = min((c + 1) * CHUNK_COLS, (hh + 1) * tile)
            segments.append((hh, slice(start - hh * tile, stop - hh * tile),
                             slice(start - c * CHUNK_COLS, stop - c * CHUNK_COLS)))
            start = stop
        return segments

    def scores(c, k_start, on_diagonal):
        keys = jnp.concatenate([k_ref[pl.ds(k_start, tile), :],
                                fb_ref[pl.ds(k_start, tile), :]], axis=1)
        s_t = jnp.dot(keys, qaug_ref[:, c * CHUNK_COLS:(c + 1) * CHUNK_COLS],
                      preferred_element_type=F32)
        if on_diagonal:
            key_row = lax.broadcasted_iota(jnp.int32, s_t.shape, 0)
            col = lax.broadcasted_iota(jnp.int32, s_t.shape, 1) + c * CHUNK_COLS
            query_col = jnp.where(col >= tile, col - tile, col)
            s_t = jnp.where(key_row <= query_col, s_t, MASK_VALUE)
        return s_t

    def value_stage(c, k_start, alpha):
        for hh, head_cols, chunk_cols in head_segments(c):
            v_aug = jnp.concatenate(
                [vt_ref[hh * HEAD_DIM:(hh + 1) * HEAD_DIM, pl.ds(k_start, tile)],
                 jnp.ones((PACKED_ROWS, tile), BF16)], axis=0)
            acc_ref[hh, :, head_cols] = (
                alpha[:, c * CHUNK_COLS:(c + 1) * CHUNK_COLS][:, chunk_cols]
                * acc_ref[hh, :, head_cols]
                + jnp.dot(v_aug, p_refs[c][:, chunk_cols], preferred_element_type=F32))

    def set_reference(r):
        hi, mid, lo = (piece.astype(F32) for piece in _split_bf16(-r))
        row = lax.broadcasted_iota(jnp.int32, (PACKED_ROWS, both), 0)
        pieces = jnp.where(row == 0, hi, jnp.where(row == 1, mid, jnp.where(row == 2, lo, 0.0)))
        qaug_ref[REF_ROW:REF_ROW + PACKED_ROWS, :] = pieces.astype(BF16)

    def bound_tile(kj, r_prev, on_diagonal):
        k_start = pl.multiple_of(kj * tile, LANES)
        base = (bi * n_tiles + kj) * N_HEADS + pair * HEADS_PER_STEP
        k_max = jnp.where(second_head, kmax_ref[base + 1], kmax_ref[base])
        if on_diagonal:
            lane = lax.broadcasted_iota(jnp.int32, (PACKED_ROWS, LANES), 1)
            head_row = pair * HEADS_PER_STEP + lax.broadcasted_iota(
                jnp.int32, (PACKED_ROWS, LANES), 0)
            pick = ((lane >= head_row * BIAS_PIECES) & (lane < (head_row + 1) * BIAS_PIECES))
            g_rows = lax.dot_general(jnp.where(pick, 1.0, 0.0).astype(BF16),
                                     fb_ref[pl.ds(k_start, tile), :], (((1,), (1,)), ((), ())),
                                     preferred_element_type=F32)
            g_max = jnp.concatenate([g_rows[hh:hh + 1, :] for hh in range(HEADS_PER_STEP)], axis=1)
        else:
            g_max = jnp.where(second_head, gend_ref[base + 1], gend_ref[base])
        r_new = jnp.maximum(r_prev, q_norm * k_max * NORM_SLACK + g_max + BIAS_SLACK)
        alpha = jnp.exp2(r_prev - r_new)
        set_reference(r_new)

        if on_diagonal:
            diagonal_groups(k_start, alpha)
            return r_new

        def probabilities(c):
            p_refs[c][...] = jnp.exp2(scores(c, k_start, False)).astype(BF16)

        probabilities(0)
        probabilities(1)
        value_stage(0, k_start, alpha)
        probabilities(2)
        value_stage(1, k_start, alpha)
        value_stage(2, k_start, alpha)
        return r_new

    def diagonal_groups(k_start, alpha):
        n_groups = tile // MXU_DIM
        assert n_groups == n_chunks

        def group_cols(hh, g):
            return slice(hh * tile + g * MXU_DIM, hh * tile + (g + 1) * MXU_DIM)

        def probabilities(g):
            n_keys = (g + 1) * MXU_DIM
            keys = jnp.concatenate([k_ref[pl.ds(k_start, n_keys), :],
                                    fb_ref[pl.ds(k_start, n_keys), :]], axis=1)
            queries = jnp.concatenate(
                [qaug_ref[:, group_cols(hh, g)] for hh in range(HEADS_PER_STEP)], axis=1)
            s_t = jnp.dot(keys, queries, preferred_element_type=F32)
            key_row = lax.broadcasted_iota(jnp.int32, s_t.shape, 0)
            col = lax.broadcasted_iota(jnp.int32, s_t.shape, 1)
            query_col = jnp.where(col >= MXU_DIM, col - MXU_DIM, col) + g * MXU_DIM
            s_t = jnp.where(key_row <= query_col, s_t, MASK_VALUE)
            p_refs[g][0:n_keys, :] = jnp.exp2(s_t).astype(BF16)

        def values(g):
            n_keys = (g + 1) * MXU_DIM
            for hh in range(HEADS_PER_STEP):
                head_cols = slice(g * MXU_DIM, (g + 1) * MXU_DIM)
                v_aug = jnp.concatenate(
                    [vt_ref[hh * HEAD_DIM:(hh + 1) * HEAD_DIM, pl.ds(k_start, n_keys)],
                     jnp.ones((PACKED_ROWS, n_keys), BF16)], axis=0)
                acc_ref[hh, :, head_cols] = (
                    alpha[:, group_cols(hh, g)] * acc_ref[hh, :, head_cols]
                    + jnp.dot(v_aug, p_refs[g][0:n_keys, hh * MXU_DIM:(hh + 1) * MXU_DIM],
                              preferred_element_type=F32))

        probabilities(2)
        probabilities(1)
        values(2)
        probabilities(0)
        values(1)
        values(0)

    assert n_chunks == 3
    acc_ref[...] = jnp.zeros_like(acc_ref)
    r_run = jnp.full((1, both), MASK_VALUE, F32)
    def tiles(first, count, r):
        for t in range(count):
            r = bound_tile(first + t, r, False)
        return r

    r_run = lax.fori_loop(0, qi // TILES_PER_TRIP,
                          lambda kk, r: tiles(kk * TILES_PER_TRIP, TILES_PER_TRIP, r), r_run)
    done = qi // TILES_PER_TRIP * TILES_PER_TRIP
    count = TILES_PER_TRIP // 2
    while count:
        has = ((qi - done) & count) != 0
        r_run = lax.cond(has, lambda r, at=done, n=count: tiles(at, n, r), lambda r: r, r_run)
        done = done + jnp.where(has, count, 0)
        count //= 2
    bound_tile(qi, r_run, True)
    smallest_denominator = jnp.min(acc_ref[:, HEAD_DIM:HEAD_DIM + 1, :])

    @pl.when(jnp.logical_not(smallest_denominator > DENOM_FLOOR))
    def _():
        def max_tile(kj, m_prev, on_diagonal):
            k_start = pl.multiple_of(kj * tile, LANES)
            m_tile = []
            for c in range(n_chunks):
                s_t = scores(c, k_start, on_diagonal)
                s_refs[c][...] = s_t
                m_tile.append(jnp.max(s_t, axis=0, keepdims=True))
            m_new = jnp.maximum(m_prev, jnp.concatenate(m_tile, axis=1))
            alpha = jnp.exp2(m_prev - m_new)
            for c in range(n_chunks):
                cols = slice(c * CHUNK_COLS, (c + 1) * CHUNK_COLS)
                p_refs[c][...] = jnp.exp2(s_refs[c][...] - m_new[:, cols]).astype(BF16)
                value_stage(c, k_start, alpha)
            return m_new

        acc_ref[...] = jnp.zeros_like(acc_ref)
        set_reference(jnp.zeros((1, both), F32))
        m_run = jnp.full((1, both), MASK_VALUE, F32)
        m_run = lax.fori_loop(0, qi, lambda kj, m: max_tile(kj, m, False), m_run)
        max_tile(qi, m_run, True)

    heads_out = [acc_ref[hh, :HEAD_DIM, :] / acc_ref[hh, HEAD_DIM:HEAD_DIM + 1, :]
                 for hh in range(HEADS_PER_STEP)]
    out_ref[...] = jnp.concatenate(heads_out, axis=0).T.astype(out_ref.dtype)


def _attn_call(proj3, proj_t, key_bias, key_norm_max, bias_max):
    batch, length, _ = proj3.shape
    pairs = N_HEADS // HEADS_PER_STEP
    blocks_per_section = D_MODEL // LANES
    n_chunks = HEADS_PER_STEP * ROW_TILE // CHUNK_COLS
    return pl.pallas_call(
        _attn_kernel,
        grid=(batch, pairs, length // ROW_TILE),
        in_specs=[
            pl.BlockSpec(memory_space=pltpu.SMEM),
            pl.BlockSpec(memory_space=pltpu.SMEM),
            pl.BlockSpec((None, LANES, ROW_TILE), lambda b, p, i: (b, T_Q * blocks_per_section + p, i)),
            pl.BlockSpec((None, length, LANES), lambda b, p, i: (b, 0, COL_K * blocks_per_section + p)),
            pl.BlockSpec((None, length, LANES), lambda b, p, i: (b, 0, 0)),
            pl.BlockSpec((None, LANES, length), lambda b, p, i: (b, T_V * blocks_per_section + p, 0)),
        ],
        out_specs=pl.BlockSpec((None, ROW_TILE, LANES), lambda b, p, i: (b, i, p)),
        out_shape=jax.ShapeDtypeStruct((batch, length, D_MODEL), BF16),
        scratch_shapes=[
            pltpu.VMEM((HEADS_PER_STEP, HEAD_DIM + PACKED_ROWS, ROW_TILE), F32),
            pltpu.VMEM((2 * LANES, HEADS_PER_STEP * ROW_TILE), BF16),
        ] + [pltpu.VMEM((ROW_TILE, CHUNK_COLS), dtype) for dtype in (F32, BF16)
             for _ in range(n_chunks)],
        compiler_params=_params("parallel", "parallel", "arbitrary"),
        name="forgetting_attention",
    )(key_norm_max, bias_max, proj_t, proj3, key_bias, proj_t)


def _mix_kernel(u_ref, halo_ref, gp_ref, ga_ref, ya_ref, h_ref, band_ref, wp_ref, ps_ref, wo_ref,
                gain_post_ref, gain_next_ref, hres_out, hnorm_out):
    li = pl.program_id(1)
    for r0 in range(0, u_ref.shape[0], MIX_BLOCK):
        rows = slice(r0, r0 + MIX_BLOCK)
        head = u_ref[r0:r0 + POOL_HALO, :].astype(F32)
        if r0 == 0:
            halo = jnp.where(li > 0, halo_ref[...].astype(F32), 0.0)
        else:
            halo = u_ref[r0 - POOL_HALO:r0, :].astype(F32)
        ext = jnp.concatenate([halo, head], axis=0)
        pos1 = (li * u_ref.shape[0] + r0 + 1
                + lax.broadcasted_iota(jnp.int32, (POOL_HALO, 1), 0))
        pooled = []
        for g, window in enumerate(POOL_WINDOWS):
            lanes = slice(g * POOL_GROUP_DIM, (g + 1) * POOL_GROUP_DIM)
            run = ext[:, lanes]
            width = 1
            while width < window:
                run = run[width:] + run[:-width]
                width *= 2
            first = POOL_HALO + 1 - window
            count = jnp.minimum(pos1, window).astype(F32)
            diff_head = run[first:first + POOL_HALO] / count - head[:, lanes]
            diff = jnp.dot(band_ref[g], u_ref[rows, lanes], preferred_element_type=F32)
            diff = jnp.concatenate([diff_head, diff[POOL_HALO:]], axis=0)
            pooled.append(jnp.dot(diff.astype(BF16), wp_ref[g], preferred_element_type=F32))
        y_pool = (jnp.concatenate(pooled, axis=1) * ps_ref[...]).astype(BF16)
        merged = (jax.nn.sigmoid(gp_ref[rows, :]) * y_pool
                  + jax.nn.sigmoid(ga_ref[rows, :]) * ya_ref[rows, :])
        mix = jnp.dot(merged, wo_ref[...], preferred_element_type=F32)
        h_new = h_ref[rows, :] + _rms_norm(mix, gain_post_ref[...])
        hres_out[rows, :] = h_new
        hnorm_out[rows, :] = _rms_norm(h_new, gain_next_ref[...]).astype(hnorm_out.dtype)


def _pool_band():
    row = jnp.arange(MIX_BLOCK)[:, None]
    col = jnp.arange(MIX_BLOCK)[None, :]
    bands = [((col <= row) & (col > row - w)).astype(F32) / w - (col == row).astype(F32)
             for w in POOL_WINDOWS]
    return jnp.stack(bands).astype(BF16)


def _mix_call(proj3, y_attn, h_res3, band, w_pool, pool_scale, w_out, gain_post, gain_next):
    batch, length, _ = proj3.shape
    halo_blocks = ROW_TILE // POOL_HALO
    row_spec = lambda col: pl.BlockSpec((None, ROW_TILE, D_MODEL), lambda b, i: (b, i, col))
    const = lambda shape: pl.BlockSpec(shape, lambda b, i: (0,) * len(shape),
                                       pipeline_mode=pl.Buffered(1))
    return pl.pallas_call(
        _mix_kernel,
        grid=(batch, length // ROW_TILE),
        in_specs=[
            row_spec(COL_POOL),
            pl.BlockSpec((None, POOL_HALO, D_MODEL),
                         lambda b, i: (b, jnp.maximum(i * halo_blocks - 1, 0), COL_POOL)),
            row_spec(COL_GPOOL),
            row_spec(COL_GATTN),
            row_spec(0),
            row_spec(0),
            const((len(POOL_WINDOWS), MIX_BLOCK, MIX_BLOCK)),
            const((len(POOL_WINDOWS), POOL_GROUP_DIM, POOL_GROUP_DIM)),
            const((1, D_MODEL)),
            const((D_MODEL, D_MODEL)),
            const((1, D_MODEL)),
            const((1, D_MODEL)),
        ],
        out_specs=[row_spec(0), row_spec(0)],
        out_shape=[jax.ShapeDtypeStruct((batch, length, D_MODEL), F32),
                   jax.ShapeDtypeStruct((batch, length, D_MODEL), BF16)],
        compiler_params=_params("parallel", "parallel"),
        name="pool_merge_out",
    )(proj3, proj3, proj3, proj3, y_attn, h_res3, band, w_pool, pool_scale, w_out, gain_post,
      gain_next)


def _ffn_kernel(hn_ref, h_ref, wg_ref, wu_ref, wd_ref, gain_post_ref, gain_next_ref,
                hres_out, hnorm_out):
    hn = hn_ref[...]
    ff_out = jnp.zeros(h_ref.shape, F32)
    for c in range(D_FF // FF_CHUNK):
        cols = slice(c * FF_CHUNK, (c + 1) * FF_CHUNK)
        gate = jnp.dot(hn, wg_ref[:, cols], preferred_element_type=F32)
        up = jnp.dot(hn, wu_ref[:, cols], preferred_element_type=F32)
        ff = gate * jax.nn.sigmoid(gate) * up
        ff_out = ff_out + jnp.dot(ff.astype(BF16), wd_ref[cols, :], preferred_element_type=F32)
    h_new = h_ref[...] + _rms_norm(ff_out, gain_post_ref[...])
    hres_out[...] = h_new
    hnorm_out[...] = _rms_norm(h_new, gain_next_ref[...]).astype(hnorm_out.dtype)


def _ffn_call(h_norm, h_res, w_gate, w_up, w_down, gain_post, gain_next):
    rows = h_res.shape[0]
    row_spec = pl.BlockSpec((ROW_TILE, D_MODEL), lambda i: (i, 0))
    resident = lambda shape: pl.BlockSpec(shape, lambda i: (0, 0), pipeline_mode=pl.Buffered(1))
    return pl.pallas_call(
        _ffn_kernel,
        grid=(rows // ROW_TILE,),
        in_specs=[row_spec, row_spec,
                  resident((D_MODEL, D_FF)), resident((D_MODEL, D_FF)), resident((D_FF, D_MODEL)),
                  resident((1, D_MODEL)), resident((1, D_MODEL))],
        out_specs=[row_spec, row_spec],
        out_shape=[jax.ShapeDtypeStruct((rows, D_MODEL), F32),
                   jax.ShapeDtypeStruct((rows, D_MODEL), BF16)],
        compiler_params=_params("parallel"),
        name="swiglu_ffn",
    )(h_norm, h_res, w_gate, w_up, w_down, gain_post, gain_next)


def kernel(x, meta_tokens, norm_mix_pre, norm_mix_post, norm_ffn_pre, norm_ffn_post,
           w_in, b_forget, w_pool, pool_scale, w_out, w_ffn_gate, w_ffn_up, w_ffn_down):
    batch, seq, _ = x.shape
    depth = w_in.shape[0]
    tokens = META_TOKENS + seq
    length = -(-tokens // ROW_TILE) * ROW_TILE
    assert (batch * length) % PROJ_ROW_TILE == 0
    meta = jnp.broadcast_to(meta_tokens[None].astype(x.dtype), (batch, META_TOKENS, D_MODEL))
    tail = jnp.zeros((batch, length - tokens, D_MODEL), x.dtype)
    h_res = jnp.concatenate([meta, x, tail], axis=1).reshape(batch * length, D_MODEL)
    gain = lambda g: g.reshape(1, D_MODEL).astype(F32)
    sec = lambda w, n: w[:, n * D_MODEL:(n + 1) * D_MODEL]
    gates_at = 4 * D_MODEL + N_HEADS
    lane_head = jnp.minimum(jnp.arange(LANES) // BIAS_PIECES, N_HEADS - 1)
    head_indicator = (jnp.arange(D_MODEL)[:, None] // HEAD_DIM == jnp.arange(LANES)[None, :]).astype(BF16)

    pool_band = _pool_band()

    h_norm = _norm_call(h_res, gain(norm_mix_pre[0]))
    for l in range(depth):
        w = w_in[l]
        w_rows = jnp.concatenate([sec(w, 0), sec(w, 2), w[:, gates_at:]], axis=1).astype(BF16)
        w_t = jnp.concatenate([sec(w, 1), sec(w, 3)], axis=1).T.astype(BF16)
        w_forget = w[:, 4 * D_MODEL:gates_at]
        wf_rep = w_forget[:, lane_head].astype(BF16)
        bias_rep = b_forget[l][lane_head].reshape(1, LANES).astype(F32)

        h_norm3 = h_norm.reshape(batch, length, D_MODEL)
        proj3 = _proj_call(h_norm, w_rows).reshape(batch, length, N_ROW_BLOCKS * D_MODEL)
        proj_t = _proj_t_call(h_norm3, w_t)
        key_bias, bias_end = _forget_call(h_norm3, wf_rep, bias_rep)
        key_norms = _key_norm_call(proj3, head_indicator)
        key_norm_max = key_norms[:, :, 0, :N_HEADS].reshape(-1)
        bias_max = bias_end[:, :, 0, 0:ONES_LANE:BIAS_PIECES].reshape(-1)
        y_attn = _attn_call(proj3, proj_t, key_bias, key_norm_max, bias_max)
        h_res3, h_norm3 = _mix_call(
            proj3, y_attn, h_res.reshape(batch, length, D_MODEL), pool_band,
            w_pool[l].astype(BF16), gain(pool_scale[l]), w_out[l].astype(BF16),
            gain(norm_mix_post[l]), gain(norm_ffn_pre[l]))
        next_gain = norm_mix_pre[(l + 1) % depth]
        h_res, h_norm = _ffn_call(
            h_norm3.reshape(batch * length, D_MODEL), h_res3.reshape(batch * length, D_MODEL),
            w_ffn_gate[l].astype(BF16), w_ffn_up[l].astype(BF16), w_ffn_down[l].astype(BF16),
            gain(norm_ffn_post[l]), gain(next_gain))
    return h_res.reshape(batch, length, D_MODEL)[:, META_TOKENS:tokens]
```

```python
import math

import jax
import jax.numpy as jnp
from jax import lax
from jax.experimental import pallas as pl
from jax.experimental.pallas import tpu as pltpu

D_MODEL = 1024
N_HEADS = 16
HEAD_DIM = 64
META_TOKENS = 16
POOL_WINDOWS = (2, 4, 8, 16)
POOL_GROUP_DIM = D_MODEL // len(POOL_WINDOWS)
POOL_HALO = max(POOL_WINDOWS)
D_FF = 2816
FF_CHUNK = 1408
RMS_EPS = 1e-6
MASK_VALUE = -1e30
LOG2_E = math.log2(math.e)

LANES = 128
SUBLANES = 8
MXU_DIM = 256
ROW_TILE = 3 * MXU_DIM
CHUNK_COLS = 2 * MXU_DIM
TILES_PER_TRIP = 4
MIX_BLOCK = MXU_DIM
PROJ_ROW_TILE = 2 * ROW_TILE
N_ROW_BLOCKS = 4
COL_POOL, COL_K, COL_GPOOL, COL_GATTN = range(N_ROW_BLOCKS)
N_T_BLOCKS = 2
T_Q, T_V = range(N_T_BLOCKS)
HEADS_PER_STEP = LANES // HEAD_DIM
BIAS_PIECES = 3
ONES_LANE = BIAS_PIECES * N_HEADS
REF_ROW = LANES + ONES_LANE
PACKED_ROWS = 16
NORM_SLACK = 1.01
BIAS_SLACK = 0.05
DENOM_FLOOR = 2.0 ** -80
VMEM_LIMIT_BYTES = 56 * 1024 * 1024

F32 = jnp.float32
BF16 = jnp.bfloat16


def _rms_norm(x, gain):
    return x * lax.rsqrt(jnp.mean(x * x, axis=-1, keepdims=True) + RMS_EPS) * gain


def _params(*semantics):
    return pltpu.CompilerParams(dimension_semantics=semantics, vmem_limit_bytes=VMEM_LIMIT_BYTES)


def _split_bf16(x):
    hi = x.astype(BF16)
    rest = x - hi.astype(F32)
    mid = rest.astype(BF16)
    lo = (rest - mid.astype(F32)).astype(BF16)
    return hi, mid, lo


def _norm_kernel(h_ref, gain_ref, out_ref):
    out_ref[...] = _rms_norm(h_ref[...], gain_ref[...]).astype(out_ref.dtype)


def _norm_call(h_res, gain):
    rows = h_res.shape[0]
    return pl.pallas_call(
        _norm_kernel,
        grid=(rows // ROW_TILE,),
        in_specs=[pl.BlockSpec((ROW_TILE, D_MODEL), lambda i: (i, 0)),
                  pl.BlockSpec((1, D_MODEL), lambda i: (0, 0))],
        out_specs=pl.BlockSpec((ROW_TILE, D_MODEL), lambda i: (i, 0)),
        out_shape=jax.ShapeDtypeStruct((rows, D_MODEL), BF16),
        compiler_params=_params("parallel"),
        name="pre_norm",
    )(h_res, gain)


def _proj_kernel(h_ref, w_ref, out_ref):
    out_ref[...] = jnp.dot(h_ref[...], w_ref[...], preferred_element_type=F32).astype(out_ref.dtype)


def _proj_call(h_norm, w_rows):
    rows = h_norm.shape[0]
    return pl.pallas_call(
        _proj_kernel,
        grid=(rows // PROJ_ROW_TILE, N_ROW_BLOCKS),
        in_specs=[pl.BlockSpec((PROJ_ROW_TILE, D_MODEL), lambda i, j: (i, 0)),
                  pl.BlockSpec((D_MODEL, D_MODEL), lambda i, j: (0, j))],
        out_specs=pl.BlockSpec((PROJ_ROW_TILE, D_MODEL), lambda i, j: (i, j)),
        out_shape=jax.ShapeDtypeStruct((rows, N_ROW_BLOCKS * D_MODEL), BF16),
        compiler_params=_params("parallel", "arbitrary"),
        name="in_proj",
    )(h_norm, w_rows)


def _proj_t_kernel(h_ref, wt_ref, out_ref):
    acc = lax.dot_general(wt_ref[...], h_ref[...], (((1,), (1,)), ((), ())),
                          preferred_element_type=F32)
    scale = jnp.where(pl.program_id(2) == T_Q, LOG2_E * HEAD_DIM ** -0.5, 1.0).astype(F32)
    out_ref[...] = (acc * scale).astype(out_ref.dtype)


def _proj_t_call(h_norm3, w_t):
    batch, length, _ = h_norm3.shape
    return pl.pallas_call(
        _proj_t_kernel,
        grid=(batch, length // ROW_TILE, N_T_BLOCKS),
        in_specs=[pl.BlockSpec((None, ROW_TILE, D_MODEL), lambda b, i, j: (b, i, 0)),
                  pl.BlockSpec((D_MODEL, D_MODEL), lambda b, i, j: (j, 0))],
        out_specs=pl.BlockSpec((None, D_MODEL, ROW_TILE), lambda b, i, j: (b, j, i)),
        out_shape=jax.ShapeDtypeStruct((batch, N_T_BLOCKS * D_MODEL, length), BF16),
        compiler_params=_params("parallel", "parallel", "arbitrary"),
        name="in_proj_t",
    )(h_norm3, w_t)


def _forget_kernel(h_ref, wf_ref, bias_ref, out_ref, end_ref, t_ref, carry_ref):
    @pl.when(pl.program_id(1) == 0)
    def _():
        carry_ref[...] = jnp.zeros_like(carry_ref)

    logit = jnp.dot(h_ref[...], wf_ref[...], preferred_element_type=F32) + bias_ref[...]
    log_f = jnp.minimum(logit, 0.0) - jnp.log(1.0 + jnp.exp(-jnp.abs(logit)))
    tile = log_f.shape[0]
    dst = lax.broadcasted_iota(jnp.int32, (tile, tile), 0)
    src = lax.broadcasted_iota(jnp.int32, (tile, tile), 1)
    prefix = (src <= dst).astype(BF16)
    total = jnp.broadcast_to(carry_ref[...], log_f.shape)
    for piece in _split_bf16(log_f):
        total = total + jnp.dot(prefix, piece, preferred_element_type=F32)
    carry_ref[...] = total[tile - 1:tile, :]
    bias = -LOG2_E * total
    end_ref[...] = jnp.broadcast_to(bias[tile - 1:tile, :], end_ref.shape)
    hi, mid, lo = _split_bf16(bias)
    lane = lax.broadcasted_iota(jnp.int32, total.shape, 1)
    piece = lane % BIAS_PIECES
    out = jnp.where(piece == 0, hi, jnp.where(piece == 1, mid, lo))
    tail = jnp.where(lane < ONES_LANE + BIAS_PIECES, 1.0, 0.0).astype(BF16)
    out_ref[...] = jnp.where(lane < ONES_LANE, out, tail)
    head = lax.broadcasted_iota(jnp.int32, (N_HEADS, LANES), 0)
    lane_h = lax.broadcasted_iota(jnp.int32, (N_HEADS, LANES), 1)
    pick = (lane_h >= head * BIAS_PIECES) & (lane_h < (head + 1) * BIAS_PIECES)
    t_ref[...] = lax.dot_general(jnp.where(pick, 1.0, 0.0).astype(BF16), out,
                                 (((1,), (1,)), ((), ())), preferred_element_type=F32)


def _forget_call(h_norm3, wf_rep, bias_rep):
    batch, length, _ = h_norm3.shape
    n_tiles = length // ROW_TILE
    return pl.pallas_call(
        _forget_kernel,
        grid=(batch, n_tiles),
        in_specs=[pl.BlockSpec((None, ROW_TILE, D_MODEL), lambda b, i: (b, i, 0)),
                  pl.BlockSpec((D_MODEL, LANES), lambda b, i: (0, 0)),
                  pl.BlockSpec((1, LANES), lambda b, i: (0, 0))],
        out_specs=[pl.BlockSpec((None, ROW_TILE, LANES), lambda b, i: (b, i, 0)),
                   pl.BlockSpec((None, None, SUBLANES, LANES), lambda b, i: (b, i, 0, 0)),
                   pl.BlockSpec((None, N_HEADS, ROW_TILE), lambda b, i: (b, 0, i))],
        out_shape=[jax.ShapeDtypeStruct((batch, length, LANES), BF16),
                   jax.ShapeDtypeStruct((batch, n_tiles, SUBLANES, LANES), F32),
                   jax.ShapeDtypeStruct((batch, N_HEADS, length), F32)],
        scratch_shapes=[pltpu.VMEM((1, LANES), F32)],
        compiler_params=_params("parallel", "arbitrary"),
        name="forget_bias",
    )(h_norm3, wf_rep, bias_rep)


def _key_norm_kernel(k_ref, head_ref, out_ref):
    k = k_ref[...].astype(F32)
    sums = jnp.dot((k * k).astype(BF16), head_ref[...], preferred_element_type=F32)
    out_ref[...] = jnp.broadcast_to(jnp.sqrt(jnp.max(sums, axis=0, keepdims=True)), out_ref.shape)


def _key_norm_call(proj3, head_indicator):
    batch, length, _ = proj3.shape
    n_tiles = length // ROW_TILE
    return pl.pallas_call(
        _key_norm_kernel,
        grid=(batch, n_tiles),
        in_specs=[pl.BlockSpec((None, ROW_TILE, D_MODEL), lambda b, i: (b, i, COL_K)),
                  pl.BlockSpec((D_MODEL, LANES), lambda b, i: (0, 0))],
        out_specs=pl.BlockSpec((None, None, SUBLANES, LANES), lambda b, i: (b, i, 0, 0)),
        out_shape=jax.ShapeDtypeStruct((batch, n_tiles, SUBLANES, LANES), F32),
        compiler_params=_params("parallel", "parallel"),
        name="key_norms",
    )(proj3, head_indicator)


def _attn_kernel(kmax_ref, gend_ref, gt_ref, qt_ref, k_ref, fb_ref, vt_ref, out_ref,
                 acc_ref, qaug_ref, *chunk_refs):
    bi = pl.program_id(0)
    pair = pl.program_id(1)
    qi = pl.program_id(2)
    n_tiles = pl.num_programs(2)
    tile = qt_ref.shape[1]
    both = HEADS_PER_STEP * tile
    n_chunks = both // CHUNK_COLS
    s_refs, p_refs = chunk_refs[:n_chunks], chunk_refs[n_chunks:]

    feat = lax.broadcasted_iota(jnp.int32, (LANES, tile), 0)
    q_norms = []
    for hh in range(HEADS_PER_STEP):
        head = pair * HEADS_PER_STEP + hh
        in_head = (feat >= hh * HEAD_DIM) & (feat < (hh + 1) * HEAD_DIM)
        q_top = jnp.where(in_head, qt_ref[...], jnp.zeros_like(qt_ref[...]))
        is_piece = (feat >= head * BIAS_PIECES) & (feat < (head + 1) * BIAS_PIECES)
        qaug_ref[:, hh * tile:(hh + 1) * tile] = jnp.concatenate(
            [q_top, is_piece.astype(BF16)], axis=0)
        q_head = qt_ref[hh * HEAD_DIM:(hh + 1) * HEAD_DIM, :].astype(F32)
        q_norms.append(jnp.sqrt(jnp.sum(q_head * q_head, axis=0, keepdims=True)))
    q_norm = jnp.concatenate(q_norms, axis=1)
    second_head = lax.broadcasted_iota(jnp.int32, (1, both), 1) >= tile
    g_diagonal = jnp.concatenate([gt_ref[hh:hh + 1, :] for hh in range(HEADS_PER_STEP)], axis=1)

    def head_segments(c):
        segments = []
        start = c * CHUNK_COLS
        while start < (c + 1) * CHUNK_COLS:
            hh = start // tile
            stop = min((c + 1) * CHUNK_COLS, (hh + 1) * tile)
            segments.append((hh, slice(start - hh * tile, stop - hh * tile),
                             slice(start - c * CHUNK_COLS, stop - c * CHUNK_COLS)))
            start = stop
        return segments

    def scores(c, k_start, on_diagonal):
        keys = jnp.concatenate([k_ref[pl.ds(k_start, tile), :],
                                fb_ref[pl.ds(k_start, tile), :]], axis=1)
        s_t = jnp.dot(keys, qaug_ref[:, c * CHUNK_COLS:(c + 1) * CHUNK_COLS],
                      preferred_element_type=F32)
        if on_diagonal:
            key_row = lax.broadcasted_iota(jnp.int32, s_t.shape, 0)
            col = lax.broadcasted_iota(jnp.int32, s_t.shape, 1) + c * CHUNK_COLS
            query_col = jnp.where(col >= tile, col - tile, col)
            s_t = jnp.where(key_row <= query_col, s_t, MASK_VALUE)
        return s_t

    def value_stage(c, k_start, alpha):
        for hh, head_cols, chunk_cols in head_segments(c):
            v_aug = jnp.concatenate(
                [vt_ref[hh * HEAD_DIM:(hh + 1) * HEAD_DIM, pl.ds(k_start, tile)],
                 jnp.ones((PACKED_ROWS, tile), BF16)], axis=0)
            acc_ref[hh, :, head_cols] = (
                alpha[:, c * CHUNK_COLS:(c + 1) * CHUNK_COLS][:, chunk_cols]
                * acc_ref[hh, :, head_cols]
                + jnp.dot(v_aug, p_refs[c][:, chunk_cols], preferred_element_type=F32))

    def set_reference(r):
        hi, mid, lo = (piece.astype(F32) for piece in _split_bf16(-r))
        row = lax.broadcasted_iota(jnp.int32, (PACKED_ROWS, both), 0)
        pieces = jnp.where(row == 0, hi, jnp.where(row == 1, mid, jnp.where(row == 2, lo, 0.0)))
        qaug_ref[REF_ROW:REF_ROW + PACKED_ROWS, :] = pieces.astype(BF16)

    def bound_tile(kj, r_prev, on_diagonal):
        k_start = pl.multiple_of(kj * tile, LANES)
        base = (bi * n_tiles + kj) * N_HEADS + pair * HEADS_PER_STEP
        k_max = jnp.where(second_head, kmax_ref[base + 1], kmax_ref[base])
        if on_diagonal:
            g_max = g_diagonal
        else:
            g_max = jnp.where(second_head, gend_ref[base + 1], gend_ref[base])
        r_new = jnp.maximum(r_prev, q_norm * k_max * NORM_SLACK + g_max + BIAS_SLACK)
        alpha = jnp.exp2(r_prev - r_new)
        set_reference(r_new)

        if on_diagonal:
            diagonal_groups(k_start, alpha)
            return r_new

        def probabilities(c):
            p_refs[c][...] = jnp.exp2(scores(c, k_start, False)).astype(BF16)

        probabilities(0)
        probabilities(1)
        value_stage(0, k_start, alpha)
        probabilities(2)
        value_stage(1, k_start, alpha)
        value_stage(2, k_start, alpha)
        return r_new

    def diagonal_groups(k_start, alpha):
        n_groups = tile // MXU_DIM
        assert n_groups == n_chunks

        def group_cols(hh, g):
            return slice(hh * tile + g * MXU_DIM, hh * tile + (g + 1) * MXU_DIM)

        def probabilities(g):
            n_keys = (g + 1) * MXU_DIM
            keys = jnp.concatenate([k_ref[pl.ds(k_start, n_keys), :],
                                    fb_ref[pl.ds(k_start, n_keys), :]], axis=1)
            queries = jnp.concatenate(
                [qaug_ref[:, group_cols(hh, g)] for hh in range(HEADS_PER_STEP)], axis=1)
            s_t = jnp.dot(keys, queries, preferred_element_type=F32)
            key_row = lax.broadcasted_iota(jnp.int32, s_t.shape, 0)
            col = lax.broadcasted_iota(jnp.int32, s_t.shape, 1)
            query_col = jnp.where(col >= MXU_DIM, col - MXU_DIM, col) + g * MXU_DIM
            s_t = jnp.where(key_row <= query_col, s_t, MASK_VALUE)
            p_refs[g][0:n_keys, :] = jnp.exp2(s_t).astype(BF16)

        def values(g):
            n_keys = (g + 1) * MXU_DIM
            for hh in range(HEADS_PER_STEP):
                head_cols = slice(g * MXU_DIM, (g + 1) * MXU_DIM)
                v_aug = jnp.concatenate(
                    [vt_ref[hh * HEAD_DIM:(hh + 1) * HEAD_DIM, pl.ds(k_start, n_keys)],
                     jnp.ones((PACKED_ROWS, n_keys), BF16)], axis=0)
                acc_ref[hh, :, head_cols] = (
                    alpha[:, group_cols(hh, g)] * acc_ref[hh, :, head_cols]
                    + jnp.dot(v_aug, p_refs[g][0:n_keys, hh * MXU_DIM:(hh + 1) * MXU_DIM],
                              preferred_element_type=F32))

        probabilities(2)
        probabilities(1)
        values(2)
        probabilities(0)
        values(1)
        values(0)

    assert n_chunks == 3
    acc_ref[...] = jnp.zeros_like(acc_ref)
    r_run = jnp.full((1, both), MASK_VALUE, F32)
    def tiles(first, count, r):
        for t in range(count):
            r = bound_tile(first + t, r, False)
        return r

    r_run = lax.fori_loop(0, qi // TILES_PER_TRIP,
                          lambda kk, r: tiles(kk * TILES_PER_TRIP, TILES_PER_TRIP, r), r_run)
    done = qi // TILES_PER_TRIP * TILES_PER_TRIP
    count = TILES_PER_TRIP // 2
    while count:
        has = ((qi - done) & count) != 0
        r_run = lax.cond(has, lambda r, at=done, n=count: tiles(at, n, r), lambda r: r, r_run)
        done = done + jnp.where(has, count, 0)
        count //= 2
    bound_tile(qi, r_run, True)

    def write_output():
        heads_out = [acc_ref[hh, :HEAD_DIM, :] / acc_ref[hh, HEAD_DIM:HEAD_DIM + 1, :]
                     for hh in range(HEADS_PER_STEP)]
        out_ref[...] = jnp.concatenate(heads_out, axis=0).T.astype(out_ref.dtype)

    write_output()
    smallest_denominator = jnp.min(jnp.minimum(acc_ref[0, HEAD_DIM:HEAD_DIM + 1, :],
                                               acc_ref[1, HEAD_DIM:HEAD_DIM + 1, :]))

    @pl.when(jnp.logical_not(smallest_denominator > DENOM_FLOOR))
    def _():
        def max_tile(kj, m_prev, on_diagonal):
            k_start = pl.multiple_of(kj * tile, LANES)
            m_tile = []
            for c in range(n_chunks):
                s_t = scores(c, k_start, on_diagonal)
                s_refs[c][...] = s_t
                m_tile.append(jnp.max(s_t, axis=0, keepdims=True))
            m_new = jnp.maximum(m_prev, jnp.concatenate(m_tile, axis=1))
            alpha = jnp.exp2(m_prev - m_new)
            for c in range(n_chunks):
                cols = slice(c * CHUNK_COLS, (c + 1) * CHUNK_COLS)
                p_refs[c][...] = jnp.exp2(s_refs[c][...] - m_new[:, cols]).astype(BF16)
                value_stage(c, k_start, alpha)
            return m_new

        acc_ref[...] = jnp.zeros_like(acc_ref)
        set_reference(jnp.zeros((1, both), F32))
        m_run = jnp.full((1, both), MASK_VALUE, F32)
        m_run = lax.fori_loop(0, qi, lambda kj, m: max_tile(kj, m, False), m_run)
        max_tile(qi, m_run, True)
        write_output()


def _attn_call(proj3, proj_t, key_bias, bias_t, key_norm_max, bias_max):
    batch, length, _ = proj3.shape
    pairs = N_HEADS // HEADS_PER_STEP
    bias_t = bias_t.reshape(batch, pairs, HEADS_PER_STEP, length)
    blocks_per_section = D_MODEL // LANES
    n_chunks = HEADS_PER_STEP * ROW_TILE // CHUNK_COLS
    return pl.pallas_call(
        _attn_kernel,
        grid=(batch, pairs, length // ROW_TILE),
        in_specs=[
            pl.BlockSpec(memory_space=pltpu.SMEM),
            pl.BlockSpec(memory_space=pltpu.SMEM),
            pl.BlockSpec((None, None, HEADS_PER_STEP, ROW_TILE), lambda b, p, i: (b, p, 0, i)),
            pl.BlockSpec((None, LANES, ROW_TILE), lambda b, p, i: (b, T_Q * blocks_per_section + p, i)),
            pl.BlockSpec((None, length, LANES), lambda b, p, i: (b, 0, COL_K * blocks_per_section + p)),
            pl.BlockSpec((None, length, LANES), lambda b, p, i: (b, 0, 0)),
            pl.BlockSpec((None, LANES, length), lambda b, p, i: (b, T_V * blocks_per_section + p, 0)),
        ],
        out_specs=pl.BlockSpec((None, ROW_TILE, LANES), lambda b, p, i: (b, i, p)),
        out_shape=jax.ShapeDtypeStruct((batch, length, D_MODEL), BF16),
        scratch_shapes=[
            pltpu.VMEM((HEADS_PER_STEP, HEAD_DIM + PACKED_ROWS, ROW_TILE), F32),
            pltpu.VMEM((2 * LANES, HEADS_PER_STEP * ROW_TILE), BF16),
        ] + [pltpu.VMEM((ROW_TILE, CHUNK_COLS), dtype) for dtype in (F32, BF16)
             for _ in range(n_chunks)],
        compiler_params=_params("parallel", "parallel", "arbitrary"),
        name="forgetting_attention",
    )(key_norm_max, bias_max, bias_t, proj_t, proj3, key_bias, proj_t)


def _mix_kernel(u_ref, halo_ref, gp_ref, ga_ref, ya_ref, h_ref, band_ref, wp_ref, ps_ref, wo_ref,
                gain_post_ref, gain_next_ref, hres_out, hnorm_out):
    li = pl.program_id(1)
    for r0 in range(0, u_ref.shape[0], MIX_BLOCK):
        rows = slice(r0, r0 + MIX_BLOCK)
        head = u_ref[r0:r0 + POOL_HALO, :].astype(F32)
        if r0 == 0:
            halo = jnp.where(li > 0, halo_ref[...].astype(F32), 0.0)
        else:
            halo = u_ref[r0 - POOL_HALO:r0, :].astype(F32)
        ext = jnp.concatenate([halo, head], axis=0)
        pos1 = (li * u_ref.shape[0] + r0 + 1
                + lax.broadcasted_iota(jnp.int32, (POOL_HALO, 1), 0))
        pooled = []
        for g, window in enumerate(POOL_WINDOWS):
            lanes = slice(g * POOL_GROUP_DIM, (g + 1) * POOL_GROUP_DIM)
            run = ext[:, lanes]
            width = 1
            while width < window:
                run = run[width:] + run[:-width]
                width *= 2
            first = POOL_HALO + 1 - window
            count = jnp.minimum(pos1, window).astype(F32)
            diff_head = run[first:first + POOL_HALO] / count - head[:, lanes]
            diff = jnp.dot(band_ref[g], u_ref[rows, lanes], preferred_element_type=F32)
            diff = jnp.concatenate([diff_head, diff[POOL_HALO:]], axis=0)
            pooled.append(jnp.dot(diff.astype(BF16), wp_ref[g], preferred_element_type=F32))
        y_pool = jnp.concatenate(pooled, axis=1) * ps_ref[...]
        merged = (jax.nn.sigmoid(gp_ref[rows, :].astype(F32)) * y_pool
                  + jax.nn.sigmoid(ga_ref[rows, :].astype(F32)) * ya_ref[rows, :].astype(F32))
        mix = jnp.dot(merged.astype(BF16), wo_ref[...], preferred_element_type=F32)
        h_new = h_ref[rows, :] + _rms_norm(mix, gain_post_ref[...])
        hres_out[rows, :] = h_new
        hnorm_out[rows, :] = _rms_norm(h_new, gain_next_ref[...]).astype(hnorm_out.dtype)


def _pool_band():
    row = jnp.arange(MIX_BLOCK)[:, None]
    col = jnp.arange(MIX_BLOCK)[None, :]
    bands = [((col <= row) & (col > row - w)).astype(F32) / w - (col == row).astype(F32)
             for w in POOL_WINDOWS]
    return jnp.stack(bands).astype(BF16)


def _mix_call(proj3, y_attn, h_res3, band, w_pool, pool_scale, w_out, gain_post, gain_next):
    batch, length, _ = proj3.shape
    halo_blocks = ROW_TILE // POOL_HALO
    row_spec = lambda col: pl.BlockSpec((None, ROW_TILE, D_MODEL), lambda b, i: (b, i, col))
    const = lambda shape: pl.BlockSpec(shape, lambda b, i: (0,) * len(shape),
                                       pipeline_mode=pl.Buffered(1))
    return pl.pallas_call(
        _mix_kernel,
        grid=(batch, length // ROW_TILE),
        in_specs=[
            row_spec(COL_POOL),
            pl.BlockSpec((None, POOL_HALO, D_MODEL),
                         lambda b, i: (b, jnp.maximum(i * halo_blocks - 1, 0), COL_POOL)),
            row_spec(COL_GPOOL),
            row_spec(COL_GATTN),
            row_spec(0),
            row_spec(0),
            const((len(POOL_WINDOWS), MIX_BLOCK, MIX_BLOCK)),
            const((len(POOL_WINDOWS), POOL_GROUP_DIM, POOL_GROUP_DIM)),
            const((1, D_MODEL)),
            const((D_MODEL, D_MODEL)),
            const((1, D_MODEL)),
            const((1, D_MODEL)),
        ],
        out_specs=[row_spec(0), row_spec(0)],
        out_shape=[jax.ShapeDtypeStruct((batch, length, D_MODEL), F32),
                   jax.ShapeDtypeStruct((batch, length, D_MODEL), BF16)],
        compiler_params=_params("parallel", "parallel"),
        name="pool_merge_out",
    )(proj3, proj3, proj3, proj3, y_attn, h_res3, band, w_pool, pool_scale, w_out, gain_post,
      gain_next)


def _ffn_kernel(hn_ref, h_ref, wg_ref, wu_ref, wd_ref, gain_post_ref, gain_next_ref,
                hres_out, hnorm_out):
    hn = hn_ref[...]
    ff_out = jnp.zeros(h_ref.shape, F32)
    for c in range(D_FF // FF_CHUNK):
        cols = slice(c * FF_CHUNK, (c + 1) * FF_CHUNK)
        gate = jnp.dot(hn, wg_ref[:, cols], preferred_element_type=F32)
        up = jnp.dot(hn, wu_ref[:, cols], preferred_element_type=F32)
        ff = gate * jax.nn.sigmoid(gate) * up
        ff_out = ff_out + jnp.dot(ff.astype(BF16), wd_ref[cols, :], preferred_element_type=F32)
    h_new = h_ref[...] + _rms_norm(ff_out, gain_post_ref[...])
    hres_out[...] = h_new
    hnorm_out[...] = _rms_norm(h_new, gain_next_ref[...]).astype(hnorm_out.dtype)


def _ffn_call(h_norm, h_res, w_gate, w_up, w_down, gain_post, gain_next):
    rows = h_res.shape[0]
    row_spec = pl.BlockSpec((ROW_TILE, D_MODEL), lambda i: (i, 0))
    resident = lambda shape: pl.BlockSpec(shape, lambda i: (0, 0), pipeline_mode=pl.Buffered(1))
    return pl.pallas_call(
        _ffn_kernel,
        grid=(rows // ROW_TILE,),
        in_specs=[row_spec, row_spec,
                  resident((D_MODEL, D_FF)), resident((D_MODEL, D_FF)), resident((D_FF, D_MODEL)),
                  resident((1, D_MODEL)), resident((1, D_MODEL))],
        out_specs=[row_spec, row_spec],
        out_shape=[jax.ShapeDtypeStruct((rows, D_MODEL), F32),
                   jax.ShapeDtypeStruct((rows, D_MODEL), BF16)],
        compiler_params=_params("parallel"),
        name="swiglu_ffn",
    )(h_norm, h_res, w_gate, w_up, w_down, gain_post, gain_next)


def kernel(x, meta_tokens, norm_mix_pre, norm_mix_post, norm_ffn_pre, norm_ffn_post,
           w_in, b_forget, w_pool, pool_scale, w_out, w_ffn_gate, w_ffn_up, w_ffn_down):
    batch, seq, _ = x.shape
    depth = w_in.shape[0]
    tokens = META_TOKENS + seq
    length = -(-tokens // ROW_TILE) * ROW_TILE
    assert (batch * length) % PROJ_ROW_TILE == 0
    meta = jnp.broadcast_to(meta_tokens[None].astype(x.dtype), (batch, META_TOKENS, D_MODEL))
    tail = jnp.zeros((batch, length - tokens, D_MODEL), x.dtype)
    h_res = jnp.concatenate([meta, x, tail], axis=1).reshape(batch * length, D_MODEL)
    gain = lambda g: g.reshape(1, D_MODEL).astype(F32)
    sec = lambda w, n: w[:, n * D_MODEL:(n + 1) * D_MODEL]
    gates_at = 4 * D_MODEL + N_HEADS
    lane_head = jnp.minimum(jnp.arange(LANES) // BIAS_PIECES, N_HEADS - 1)
    head_indicator = (jnp.arange(D_MODEL)[:, None] // HEAD_DIM == jnp.arange(LANES)[None, :]).astype(BF16)

    pool_band = _pool_band()

    h_norm = _norm_call(h_res, gain(norm_mix_pre[0]))
    for l in range(depth):
        w = w_in[l]
        w_rows = jnp.concatenate([sec(w, 0), sec(w, 2), w[:, gates_at:]], axis=1).astype(BF16)
        w_t = jnp.concatenate([sec(w, 1), sec(w, 3)], axis=1).T.astype(BF16)
        w_forget = w[:, 4 * D_MODEL:gates_at]
        wf_rep = w_forget[:, lane_head].astype(BF16)
        bias_rep = b_forget[l][lane_head].reshape(1, LANES).astype(F32)

        h_norm3 = h_norm.reshape(batch, length, D_MODEL)
        proj3 = _proj_call(h_norm, w_rows).reshape(batch, length, N_ROW_BLOCKS * D_MODEL)
        proj_t = _proj_t_call(h_norm3, w_t)
        key_bias, bias_end, bias_t = _forget_call(h_norm3, wf_rep, bias_rep)
        key_norms = _key_norm_call(proj3, head_indicator)
        key_norm_max = key_norms[:, :, 0, :N_HEADS].reshape(-1)
        bias_max = bias_end[:, :, 0, 0:ONES_LANE:BIAS_PIECES].reshape(-1)
        y_attn = _attn_call(proj3, proj_t, key_bias, bias_t, key_norm_max, bias_max)
        h_res3, h_norm3 = _mix_call(
            proj3, y_attn, h_res.reshape(batch, length, D_MODEL), pool_band,
            w_pool[l].astype(BF16), gain(pool_scale[l]), w_out[l].astype(BF16),
            gain(norm_mix_post[l]), gain(norm_ffn_pre[l]))
        next_gain = norm_mix_pre[(l + 1) % depth]
        h_res, h_norm = _ffn_call(
            h_norm3.reshape(batch * length, D_MODEL), h_res3.reshape(batch * length, D_MODEL),
            w_ffn_gate[l].astype(BF16), w_ffn_up[l].astype(BF16), w_ffn_down[l].astype(BF16),
            gain(norm_ffn_post[l]), gain(next_gain))
    return h_res.reshape(batch, length, D_MODEL)[:, META_TOKENS:tokens]
```

```python
import math

import jax
import jax.numpy as jnp
from jax import lax
from jax.experimental import pallas as pl
from jax.experimental.pallas import tpu as pltpu

D_MODEL = 1024
N_HEADS = 16
HEAD_DIM = 64
META_TOKENS = 16
POOL_WINDOWS = (2, 4, 8, 16)
POOL_GROUP_DIM = D_MODEL // len(POOL_WINDOWS)
POOL_HALO = max(POOL_WINDOWS)
D_FF = 2816
FF_CHUNK = 1408
RMS_EPS = 1e-6
MASK_VALUE = -1e30
LOG2_E = math.log2(math.e)

LANES = 128
SUBLANES = 8
MXU_DIM = 256
ROW_TILE = 3 * MXU_DIM
CHUNK_COLS = 2 * MXU_DIM
TILES_PER_TRIP = 4
FUSED_ROW_TILE = ROW_TILE
MIX_BLOCK = FUSED_ROW_TILE // 4
PROJ_ROW_TILE = 2 * ROW_TILE
N_ROW_BLOCKS = 4
COL_POOL, COL_K, COL_GPOOL, COL_GATTN = range(N_ROW_BLOCKS)
N_T_BLOCKS = 2
T_Q, T_V = range(N_T_BLOCKS)
HEADS_PER_STEP = LANES // HEAD_DIM
BIAS_PIECES = 3
ONES_LANE = BIAS_PIECES * N_HEADS
REF_ROW = LANES + ONES_LANE
PACKED_ROWS = 16
NORM_SLACK = 1.01
BIAS_SLACK = 0.05
DENOM_FLOOR = 2.0 ** -80
VMEM_LIMIT_BYTES = 56 * 1024 * 1024

F32 = jnp.float32
BF16 = jnp.bfloat16


def _rms_norm(x, gain):
    return x * lax.rsqrt(jnp.mean(x * x, axis=-1, keepdims=True) + RMS_EPS) * gain


def _params(*semantics):
    return pltpu.CompilerParams(dimension_semantics=semantics, vmem_limit_bytes=VMEM_LIMIT_BYTES)


def _split_bf16(x):
    hi = x.astype(BF16)
    rest = x - hi.astype(F32)
    mid = rest.astype(BF16)
    lo = (rest - mid.astype(F32)).astype(BF16)
    return hi, mid, lo


def _norm_kernel(h_ref, gain_ref, out_ref):
    out_ref[...] = _rms_norm(h_ref[...], gain_ref[...]).astype(out_ref.dtype)


def _norm_call(h_res, gain):
    rows = h_res.shape[0]
    return pl.pallas_call(
        _norm_kernel,
        grid=(rows // ROW_TILE,),
        in_specs=[pl.BlockSpec((ROW_TILE, D_MODEL), lambda i: (i, 0)),
                  pl.BlockSpec((1, D_MODEL), lambda i: (0, 0))],
        out_specs=pl.BlockSpec((ROW_TILE, D_MODEL), lambda i: (i, 0)),
        out_shape=jax.ShapeDtypeStruct((rows, D_MODEL), BF16),
        compiler_params=_params("parallel"),
        name="pre_norm",
    )(h_res, gain)


def _proj_kernel(h_ref, w_ref, out_ref):
    out_ref[...] = jnp.dot(h_ref[...], w_ref[...], preferred_element_type=F32).astype(out_ref.dtype)


def _proj_call(h_norm, w_rows):
    rows = h_norm.shape[0]
    return pl.pallas_call(
        _proj_kernel,
        grid=(rows // PROJ_ROW_TILE, N_ROW_BLOCKS),
        in_specs=[pl.BlockSpec((PROJ_ROW_TILE, D_MODEL), lambda i, j: (i, 0)),
                  pl.BlockSpec((D_MODEL, D_MODEL), lambda i, j: (0, j))],
        out_specs=pl.BlockSpec((PROJ_ROW_TILE, D_MODEL), lambda i, j: (i, j)),
        out_shape=jax.ShapeDtypeStruct((rows, N_ROW_BLOCKS * D_MODEL), BF16),
        compiler_params=_params("parallel", "arbitrary"),
        name="in_proj",
    )(h_norm, w_rows)


def _proj_t_kernel(h_ref, wt_ref, out_ref):
    acc = lax.dot_general(wt_ref[...], h_ref[...], (((1,), (1,)), ((), ())),
                          preferred_element_type=F32)
    scale = jnp.where(pl.program_id(2) == T_Q, LOG2_E * HEAD_DIM ** -0.5, 1.0).astype(F32)
    out_ref[...] = (acc * scale).astype(out_ref.dtype)


def _proj_t_call(h_norm3, w_t):
    batch, length, _ = h_norm3.shape
    return pl.pallas_call(
        _proj_t_kernel,
        grid=(batch, length // ROW_TILE, N_T_BLOCKS),
        in_specs=[pl.BlockSpec((None, ROW_TILE, D_MODEL), lambda b, i, j: (b, i, 0)),
                  pl.BlockSpec((D_MODEL, D_MODEL), lambda b, i, j: (j, 0))],
        out_specs=pl.BlockSpec((None, D_MODEL, ROW_TILE), lambda b, i, j: (b, j, i)),
        out_shape=jax.ShapeDtypeStruct((batch, N_T_BLOCKS * D_MODEL, length), BF16),
        compiler_params=_params("parallel", "parallel", "arbitrary"),
        name="in_proj_t",
    )(h_norm3, w_t)


def _forget_kernel(h_ref, wf_ref, bias_ref, out_ref, end_ref, t_ref, carry_ref):
    @pl.when(pl.program_id(1) == 0)
    def _():
        carry_ref[...] = jnp.zeros_like(carry_ref)

    logit = jnp.dot(h_ref[...], wf_ref[...], preferred_element_type=F32) + bias_ref[...]
    log_f = jnp.minimum(logit, 0.0) - jnp.log(1.0 + jnp.exp(-jnp.abs(logit)))
    tile = log_f.shape[0]
    dst = lax.broadcasted_iota(jnp.int32, (MXU_DIM, MXU_DIM), 0)
    src = lax.broadcasted_iota(jnp.int32, (MXU_DIM, MXU_DIM), 1)
    prefix = (src <= dst).astype(BF16)
    pieces = _split_bf16(log_f)
    carry = carry_ref[...]
    blocks = []
    for r0 in range(0, tile, MXU_DIM):
        block = jnp.broadcast_to(carry, (MXU_DIM, LANES))
        for piece in pieces:
            block = block + jnp.dot(prefix, piece[r0:r0 + MXU_DIM, :], preferred_element_type=F32)
        carry = block[MXU_DIM - 1:MXU_DIM, :]
        blocks.append(block)
    total = jnp.concatenate(blocks, axis=0)
    carry_ref[...] = carry
    bias = -LOG2_E * total
    end_ref[...] = jnp.broadcast_to(bias[tile - 1:tile, :], end_ref.shape)
    hi, mid, lo = _split_bf16(bias)
    lane = lax.broadcasted_iota(jnp.int32, total.shape, 1)
    piece = lane % BIAS_PIECES
    out = jnp.where(piece == 0, hi, jnp.where(piece == 1, mid, lo))
    tail = jnp.where(lane < ONES_LANE + BIAS_PIECES, 1.0, 0.0).astype(BF16)
    out_ref[...] = jnp.where(lane < ONES_LANE, out, tail)
    head = lax.broadcasted_iota(jnp.int32, (N_HEADS, LANES), 0)
    lane_h = lax.broadcasted_iota(jnp.int32, (N_HEADS, LANES), 1)
    pick = (lane_h >= head * BIAS_PIECES) & (lane_h < (head + 1) * BIAS_PIECES)
    t_ref[...] = lax.dot_general(jnp.where(pick, 1.0, 0.0).astype(BF16), out,
                                 (((1,), (1,)), ((), ())), preferred_element_type=F32)


def _forget_call(h_norm3, wf_rep, bias_rep):
    batch, length, _ = h_norm3.shape
    n_tiles = length // ROW_TILE
    return pl.pallas_call(
        _forget_kernel,
        grid=(batch, n_tiles),
        in_specs=[pl.BlockSpec((None, ROW_TILE, D_MODEL), lambda b, i: (b, i, 0)),
                  pl.BlockSpec((D_MODEL, LANES), lambda b, i: (0, 0)),
                  pl.BlockSpec((1, LANES), lambda b, i: (0, 0))],
        out_specs=[pl.BlockSpec((None, ROW_TILE, LANES), lambda b, i: (b, i, 0)),
                   pl.BlockSpec((None, None, SUBLANES, LANES), lambda b, i: (b, i, 0, 0)),
                   pl.BlockSpec((None, N_HEADS, ROW_TILE), lambda b, i: (b, 0, i))],
        out_shape=[jax.ShapeDtypeStruct((batch, length, LANES), BF16),
                   jax.ShapeDtypeStruct((batch, n_tiles, SUBLANES, LANES), F32),
                   jax.ShapeDtypeStruct((batch, N_HEADS, length), F32)],
        scratch_shapes=[pltpu.VMEM((1, LANES), F32)],
        compiler_params=_params("parallel", "arbitrary"),
        name="forget_bias",
    )(h_norm3, wf_rep, bias_rep)


def _key_norm_kernel(k_ref, head_ref, out_ref):
    k = k_ref[...].astype(F32)
    sums = jnp.dot((k * k).astype(BF16), head_ref[...], preferred_element_type=F32)
    out_ref[...] = jnp.broadcast_to(jnp.sqrt(jnp.max(sums, axis=0, keepdims=True)), out_ref.shape)


def _key_norm_call(proj3, head_indicator):
    batch, length, _ = proj3.shape
    n_tiles = length // ROW_TILE
    return pl.pallas_call(
        _key_norm_kernel,
        grid=(batch, n_tiles),
        in_specs=[pl.BlockSpec((None, ROW_TILE, D_MODEL), lambda b, i: (b, i, COL_K)),
                  pl.BlockSpec((D_MODEL, LANES), lambda b, i: (0, 0))],
        out_specs=pl.BlockSpec((None, None, SUBLANES, LANES), lambda b, i: (b, i, 0, 0)),
        out_shape=jax.ShapeDtypeStruct((batch, n_tiles, SUBLANES, LANES), F32),
        compiler_params=_params("parallel", "parallel"),
        name="key_norms",
    )(proj3, head_indicator)


def _attn_kernel(kmax_ref, gend_ref, gt_ref, qt_ref, k_ref, fb_ref, vt_ref, out_ref,
                 acc_ref, qaug_ref, *chunk_refs):
    bi = pl.program_id(0)
    pair = pl.program_id(1)
    qi = pl.program_id(2)
    n_tiles = pl.num_programs(2)
    tile = qt_ref.shape[1]
    both = HEADS_PER_STEP * tile
    n_chunks = both // CHUNK_COLS
    s_refs, p_refs = chunk_refs[:n_chunks], chunk_refs[n_chunks:]

    feat = lax.broadcasted_iota(jnp.int32, (LANES, tile), 0)
    q_norms = []
    for hh in range(HEADS_PER_STEP):
        head = pair * HEADS_PER_STEP + hh
        in_head = (feat >= hh * HEAD_DIM) & (feat < (hh + 1) * HEAD_DIM)
        q_top = jnp.where(in_head, qt_ref[...], jnp.zeros_like(qt_ref[...]))
        is_piece = (feat >= head * BIAS_PIECES) & (feat < (head + 1) * BIAS_PIECES)
        qaug_ref[:, hh * tile:(hh + 1) * tile] = jnp.concatenate(
            [q_top, is_piece.astype(BF16)], axis=0)
        q_head = qt_ref[hh * HEAD_DIM:(hh + 1) * HEAD_DIM, :].astype(F32)
        q_norms.append(jnp.sqrt(jnp.sum(q_head * q_head, axis=0, keepdims=True)))
    q_norm = jnp.concatenate(q_norms, axis=1)
    second_head = lax.broadcasted_iota(jnp.int32, (1, both), 1) >= tile
    g_diagonal = jnp.concatenate([gt_ref[hh:hh + 1, :] for hh in range(HEADS_PER_STEP)], axis=1)

    def head_segments(c):
        segments = []
        start = c * CHUNK_COLS
        while start < (c + 1) * CHUNK_COLS:
            hh = start // tile
            stop = min((c + 1) * CHUNK_COLS, (hh + 1) * tile)
            segments.append((hh, slice(start - hh * tile, stop - hh * tile),
                             slice(start - c * CHUNK_COLS, stop - c * CHUNK_COLS)))
            start = stop
        return segments

    def scores(c, k_start, on_diagonal):
        keys = jnp.concatenate([k_ref[pl.ds(k_start, tile), :],
                                fb_ref[pl.ds(k_start, tile), :]], axis=1)
        s_t = jnp.dot(keys, qaug_ref[:, c * CHUNK_COLS:(c + 1) * CHUNK_COLS],
                      preferred_element_type=F32)
        if on_diagonal:
            key_row = lax.broadcasted_iota(jnp.int32, s_t.shape, 0)
            col = lax.broadcasted_iota(jnp.int32, s_t.shape, 1) + c * CHUNK_COLS
            query_col = jnp.where(col >= tile, col - tile, col)
            s_t = jnp.where(key_row <= query_col, s_t, MASK_VALUE)
        return s_t

    def value_stage(c, k_start, alpha):
        for hh, head_cols, chunk_cols in head_segments(c):
            v_aug = jnp.concatenate(
                [vt_ref[hh * HEAD_DIM:(hh + 1) * HEAD_DIM, pl.ds(k_start, tile)],
                 jnp.ones((PACKED_ROWS, tile), BF16)], axis=0)
            acc_ref[hh, :, head_cols] = (
                alpha[:, c * CHUNK_COLS:(c + 1) * CHUNK_COLS][:, chunk_cols]
                * acc_ref[hh, :, head_cols]
                + jnp.dot(v_aug, p_refs[c][:, chunk_cols], preferred_element_type=F32))

    def set_reference(r):
        hi, mid, lo = (piece.astype(F32) for piece in _split_bf16(-r))
        row = lax.broadcasted_iota(jnp.int32, (PACKED_ROWS, both), 0)
        pieces = jnp.where(row == 0, hi, jnp.where(row == 1, mid, jnp.where(row == 2, lo, 0.0)))
        qaug_ref[REF_ROW:REF_ROW + PACKED_ROWS, :] = pieces.astype(BF16)

    def bound_tile(kj, r_prev, on_diagonal):
        k_start = pl.multiple_of(kj * tile, LANES)
        base = (bi * n_tiles + kj) * N_HEADS + pair * HEADS_PER_STEP
        k_max = jnp.where(second_head, kmax_ref[base + 1], kmax_ref[base])
        if on_diagonal:
            g_max = g_diagonal
        else:
            g_max = jnp.where(second_head, gend_ref[base + 1], gend_ref[base])
        r_new = jnp.maximum(r_prev, q_norm * k_max * NORM_SLACK + g_max + BIAS_SLACK)
        alpha = jnp.exp2(r_prev - r_new)
        set_reference(r_new)

        if on_diagonal:
            diagonal_groups(k_start, alpha)
            return r_new

        def probabilities(c):
            p_refs[c][...] = jnp.exp2(scores(c, k_start, False)).astype(BF16)

        probabilities(0)
        probabilities(1)
        value_stage(0, k_start, alpha)
        probabilities(2)
        value_stage(1, k_start, alpha)
        value_stage(2, k_start, alpha)
        return r_new

    def diagonal_groups(k_start, alpha):
        n_groups = tile // MXU_DIM
        assert n_groups == n_chunks

        def group_cols(hh, g):
            return slice(hh * tile + g * MXU_DIM, hh * tile + (g + 1) * MXU_DIM)

        def probabilities(g):
            n_keys = (g + 1) * MXU_DIM
            keys = jnp.concatenate([k_ref[pl.ds(k_start, n_keys), :],
                                    fb_ref[pl.ds(k_start, n_keys), :]], axis=1)
            queries = jnp.concatenate(
                [qaug_ref[:, group_cols(hh, g)] for hh in range(HEADS_PER_STEP)], axis=1)
            s_t = jnp.dot(keys, queries, preferred_element_type=F32)
            key_row = lax.broadcasted_iota(jnp.int32, s_t.shape, 0)
            col = lax.broadcasted_iota(jnp.int32, s_t.shape, 1)
            query_col = jnp.where(col >= MXU_DIM, col - MXU_DIM, col) + g * MXU_DIM
            s_t = jnp.where(key_row <= query_col, s_t, MASK_VALUE)
            p_refs[g][0:n_keys, :] = jnp.exp2(s_t).astype(BF16)

        def values(g):
            n_keys = (g + 1) * MXU_DIM
            for hh in range(HEADS_PER_STEP):
                head_cols = slice(g * MXU_DIM, (g + 1) * MXU_DIM)
                v_aug = jnp.concatenate(
                    [vt_ref[hh * HEAD_DIM:(hh + 1) * HEAD_DIM, pl.ds(k_start, n_keys)],
                     jnp.ones((PACKED_ROWS, n_keys), BF16)], axis=0)
                acc_ref[hh, :, head_cols] = (
                    alpha[:, group_cols(hh, g)] * acc_ref[hh, :, head_cols]
                    + jnp.dot(v_aug, p_refs[g][0:n_keys, hh * MXU_DIM:(hh + 1) * MXU_DIM],
                              preferred_element_type=F32))

        probabilities(2)
        probabilities(1)
        values(2)
        probabilities(0)
        values(1)
        values(0)

    assert n_chunks == 3
    acc_ref[...] = jnp.zeros_like(acc_ref)
    r_run = jnp.full((1, both), MASK_VALUE, F32)
    def tiles(first, count, r):
        for t in range(count):
            r = bound_tile(first + t, r, False)
        return r

    r_run = lax.fori_loop(0, qi // TILES_PER_TRIP,
                          lambda kk, r: tiles(kk * TILES_PER_TRIP, TILES_PER_TRIP, r), r_run)
    done = qi // TILES_PER_TRIP * TILES_PER_TRIP
    count = TILES_PER_TRIP // 2
    while count:
        has = ((qi - done) & count) != 0
        r_run = lax.cond(has, lambda r, at=done, n=count: tiles(at, n, r), lambda r: r, r_run)
        done = done + jnp.where(has, count, 0)
        count //= 2
    bound_tile(qi, r_run, True)

    def write_output():
        heads_out = [acc_ref[hh, :HEAD_DIM, :] / acc_ref[hh, HEAD_DIM:HEAD_DIM + 1, :]
                     for hh in range(HEADS_PER_STEP)]
        out_ref[...] = jnp.concatenate(heads_out, axis=0).T.astype(out_ref.dtype)

    write_output()
    smallest_denominator = jnp.min(jnp.minimum(acc_ref[0, HEAD_DIM:HEAD_DIM + 1, :],
                                               acc_ref[1, HEAD_DIM:HEAD_DIM + 1, :]))

    @pl.when(jnp.logical_not(smallest_denominator > DENOM_FLOOR))
    def _():
        def max_tile(kj, m_prev, on_diagonal):
            k_start = pl.multiple_of(kj * tile, LANES)
            m_tile = []
            for c in range(n_chunks):
                s_t = scores(c, k_start, on_diagonal)
                s_refs[c][...] = s_t
                m_tile.append(jnp.max(s_t, axis=0, keepdims=True))
            m_new = jnp.maximum(m_prev, jnp.concatenate(m_tile, axis=1))
            alpha = jnp.exp2(m_prev - m_new)
            for c in range(n_chunks):
                cols = slice(c * CHUNK_COLS, (c + 1) * CHUNK_COLS)
                p_refs[c][...] = jnp.exp2(s_refs[c][...] - m_new[:, cols]).astype(BF16)
                value_stage(c, k_start, alpha)
            return m_new

        acc_ref[...] = jnp.zeros_like(acc_ref)
        set_reference(jnp.zeros((1, both), F32))
        m_run = jnp.full((1, both), MASK_VALUE, F32)
        m_run = lax.fori_loop(0, qi, lambda kj, m: max_tile(kj, m, False), m_run)
        max_tile(qi, m_run, True)
        write_output()


def _attn_call(proj3, proj_t, key_bias, bias_t, key_norm_max, bias_max):
    batch, length, _ = proj3.shape
    pairs = N_HEADS // HEADS_PER_STEP
    bias_t = bias_t.reshape(batch, pairs, HEADS_PER_STEP, length)
    blocks_per_section = D_MODEL // LANES
    n_chunks = HEADS_PER_STEP * ROW_TILE // CHUNK_COLS
    return pl.pallas_call(
        _attn_kernel,
        grid=(batch, pairs, length // ROW_TILE),
        in_specs=[
            pl.BlockSpec(memory_space=pltpu.SMEM),
            pl.BlockSpec(memory_space=pltpu.SMEM),
            pl.BlockSpec((None, None, HEADS_PER_STEP, ROW_TILE), lambda b, p, i: (b, p, 0, i)),
            pl.BlockSpec((None, LANES, ROW_TILE), lambda b, p, i: (b, T_Q * blocks_per_section + p, i)),
            pl.BlockSpec((None, length, LANES), lambda b, p, i: (b, 0, COL_K * blocks_per_section + p)),
            pl.BlockSpec((None, length, LANES), lambda b, p, i: (b, 0, 0)),
            pl.BlockSpec((None, LANES, length), lambda b, p, i: (b, T_V * blocks_per_section + p, 0)),
        ],
        out_specs=pl.BlockSpec((None, ROW_TILE, LANES), lambda b, p, i: (b, i, p)),
        out_shape=jax.ShapeDtypeStruct((batch, length, D_MODEL), BF16),
        scratch_shapes=[
            pltpu.VMEM((HEADS_PER_STEP, HEAD_DIM + PACKED_ROWS, ROW_TILE), F32),
            pltpu.VMEM((2 * LANES, HEADS_PER_STEP * ROW_TILE), BF16),
        ] + [pltpu.VMEM((ROW_TILE, CHUNK_COLS), dtype) for dtype in (F32, BF16)
             for _ in range(n_chunks)],
        compiler_params=_params("parallel", "parallel", "arbitrary"),
        name="forgetting_attention",
    )(key_norm_max, bias_max, bias_t, proj_t, proj3, key_bias, proj_t)


def _mix_kernel(u_ref, halo_ref, gp_ref, ga_ref, ya_ref, h_ref, band_ref, wp_ref, ps_ref, wo_ref,
                gain_post_ref, gain_next_ref, hres_out, hnorm_out):
    li = pl.program_id(1)
    for r0 in range(0, u_ref.shape[0], MIX_BLOCK):
        rows = slice(r0, r0 + MIX_BLOCK)
        head = u_ref[r0:r0 + POOL_HALO, :].astype(F32)
        if r0 == 0:
            halo = jnp.where(li > 0, halo_ref[...].astype(F32), 0.0)
        else:
            halo = u_ref[r0 - POOL_HALO:r0, :].astype(F32)
        ext = jnp.concatenate([halo, head], axis=0)
        pos1 = (li * u_ref.shape[0] + r0 + 1
                + lax.broadcasted_iota(jnp.int32, (POOL_HALO, 1), 0))
        pooled = []
        for g, window in enumerate(POOL_WINDOWS):
            lanes = slice(g * POOL_GROUP_DIM, (g + 1) * POOL_GROUP_DIM)
            run = ext[:, lanes]
            width = 1
            while width < window:
                run = run[width:] + run[:-width]
                width *= 2
            first = POOL_HALO + 1 - window
            count = jnp.minimum(pos1, window).astype(F32)
            diff_head = run[first:first + POOL_HALO] / count - head[:, lanes]
            diff = jnp.dot(band_ref[g], u_ref[rows, lanes], preferred_element_type=F32)
            diff = jnp.concatenate([diff_head, diff[POOL_HALO:]], axis=0)
            pooled.append(jnp.dot(diff.astype(BF16), wp_ref[g], preferred_element_type=F32))
        y_pool = jnp.concatenate(pooled, axis=1) * ps_ref[...]
        merged = (jax.nn.sigmoid(gp_ref[rows, :].astype(F32)) * y_pool
                  + jax.nn.sigmoid(ga_ref[rows, :].astype(F32)) * ya_ref[rows, :].astype(F32))
        mix = jnp.dot(merged.astype(BF16), wo_ref[...], preferred_element_type=F32)
        h_new = h_ref[rows, :] + _rms_norm(mix, gain_post_ref[...])
        hres_out[rows, :] = h_new
        hnorm_out[rows, :] = _rms_norm(h_new, gain_next_ref[...]).astype(hnorm_out.dtype)


def _pool_band():
    row = jnp.arange(MIX_BLOCK)[:, None]
    col = jnp.arange(MIX_BLOCK)[None, :]
    bands = [((col <= row) & (col > row - w)).astype(F32) / w - (col == row).astype(F32)
             for w in POOL_WINDOWS]
    return jnp.stack(bands).astype(BF16)


def _mix_call(proj3, y_attn, h_res3, band, w_pool, pool_scale, w_out, gain_post, gain_next):
    batch, length, _ = proj3.shape
    halo_blocks = ROW_TILE // POOL_HALO
    row_spec = lambda col: pl.BlockSpec((None, ROW_TILE, D_MODEL), lambda b, i: (b, i, col))
    const = lambda shape: pl.BlockSpec(shape, lambda b, i: (0,) * len(shape),
                                       pipeline_mode=pl.Buffered(1))
    return pl.pallas_call(
        _mix_kernel,
        grid=(batch, length // ROW_TILE),
        in_specs=[
            row_spec(COL_POOL),
            pl.BlockSpec((None, POOL_HALO, D_MODEL),
                         lambda b, i: (b, jnp.maximum(i * halo_blocks - 1, 0), COL_POOL)),
            row_spec(COL_GPOOL),
            row_spec(COL_GATTN),
            row_spec(0),
            row_spec(0),
            const((len(POOL_WINDOWS), MIX_BLOCK, MIX_BLOCK)),
            const((len(POOL_WINDOWS), POOL_GROUP_DIM, POOL_GROUP_DIM)),
            const((1, D_MODEL)),
            const((D_MODEL, D_MODEL)),
            const((1, D_MODEL)),
            const((1, D_MODEL)),
        ],
        out_specs=[row_spec(0), row_spec(0)],
        out_shape=[jax.ShapeDtypeStruct((batch, length, D_MODEL), F32),
                   jax.ShapeDtypeStruct((batch, length, D_MODEL), BF16)],
        compiler_params=_params("parallel", "parallel"),
        name="pool_merge_out",
    )(proj3, proj3, proj3, proj3, y_attn, h_res3, band, w_pool, pool_scale, w_out, gain_post,
      gain_next)


def _ffn_kernel(hn_ref, h_ref, wg_ref, wu_ref, wd_ref, gain_post_ref, gain_next_ref,
                hres_out, hnorm_out):
    hn = hn_ref[...]
    ff_out = jnp.zeros(h_ref.shape, F32)
    for c in range(D_FF // FF_CHUNK):
        cols = slice(c * FF_CHUNK, (c + 1) * FF_CHUNK)
        gate = jnp.dot(hn, wg_ref[:, cols], preferred_element_type=F32)
        up = jnp.dot(hn, wu_ref[:, cols], preferred_element_type=F32)
        ff = gate * jax.nn.sigmoid(gate) * up
        ff_out = ff_out + jnp.dot(ff.astype(BF16), wd_ref[cols, :], preferred_element_type=F32)
    h_new = h_ref[...] + _rms_norm(ff_out, gain_post_ref[...])
    hres_out[...] = h_new
    hnorm_out[...] = _rms_norm(h_new, gain_next_ref[...]).astype(hnorm_out.dtype)


def _ffn_call(h_norm, h_res, w_gate, w_up, w_down, gain_post, gain_next):
    rows = h_res.shape[0]
    row_spec = pl.BlockSpec((ROW_TILE, D_MODEL), lambda i: (i, 0))
    resident = lambda shape: pl.BlockSpec(shape, lambda i: (0, 0), pipeline_mode=pl.Buffered(1))
    return pl.pallas_call(
        _ffn_kernel,
        grid=(rows // ROW_TILE,),
        in_specs=[row_spec, row_spec,
                  resident((D_MODEL, D_FF)), resident((D_MODEL, D_FF)), resident((D_FF, D_MODEL)),
                  resident((1, D_MODEL)), resident((1, D_MODEL))],
        out_specs=[row_spec, row_spec],
        out_shape=[jax.ShapeDtypeStruct((rows, D_MODEL), F32),
                   jax.ShapeDtypeStruct((rows, D_MODEL), BF16)],
        compiler_params=_params("parallel"),
        name="swiglu_ffn",
    )(h_norm, h_res, w_gate, w_up, w_down, gain_post, gain_next)


def _mix_ffn_kernel(u_ref, halo_ref, gp_ref, ga_ref, ya_ref, h_ref, band_ref, wp_ref, ps_ref,
                    wo_ref, gain_mix_ref, gain_pre_ref, wg_ref, wu_ref, wd_ref, gain_ffn_ref,
                    gain_next_ref, hres_out, hnorm_out):
    li = pl.program_id(1)
    mixed = []
    for r0 in range(0, u_ref.shape[0], MIX_BLOCK):
        rows = slice(r0, r0 + MIX_BLOCK)
        head = u_ref[r0:r0 + POOL_HALO, :].astype(F32)
        if r0 == 0:
            halo = jnp.where(li > 0, halo_ref[...].astype(F32), 0.0)
        else:
            halo = u_ref[r0 - POOL_HALO:r0, :].astype(F32)
        ext = jnp.concatenate([halo, head], axis=0)
        pos1 = (li * u_ref.shape[0] + r0 + 1
                + lax.broadcasted_iota(jnp.int32, (POOL_HALO, 1), 0))
        pooled = []
        for g, window in enumerate(POOL_WINDOWS):
            lanes = slice(g * POOL_GROUP_DIM, (g + 1) * POOL_GROUP_DIM)
            run = ext[:, lanes]
            width = 1
            while width < window:
                run = run[width:] + run[:-width]
                width *= 2
            first = POOL_HALO + 1 - window
            count = jnp.minimum(pos1, window).astype(F32)
            diff_head = run[first:first + POOL_HALO] / count - head[:, lanes]
            diff = jnp.dot(band_ref[g], u_ref[rows, lanes], preferred_element_type=F32)
            diff = jnp.concatenate([diff_head, diff[POOL_HALO:]], axis=0)
            pooled.append(jnp.dot(diff.astype(BF16), wp_ref[g], preferred_element_type=F32))
        y_pool = jnp.concatenate(pooled, axis=1) * ps_ref[...]
        merged = (jax.nn.sigmoid(gp_ref[rows, :].astype(F32)) * y_pool
                  + jax.nn.sigmoid(ga_ref[rows, :].astype(F32)) * ya_ref[rows, :].astype(F32))
        mix = jnp.dot(merged.astype(BF16), wo_ref[...], preferred_element_type=F32)
        mixed.append(h_ref[rows, :] + _rms_norm(mix, gain_mix_ref[...]))
    for block, r0 in enumerate(range(0, u_ref.shape[0], MIX_BLOCK)):
        rows = slice(r0, r0 + MIX_BLOCK)
        h_mid = mixed[block]
        hn = _rms_norm(h_mid, gain_pre_ref[...]).astype(BF16)
        ff_out = jnp.zeros(h_mid.shape, F32)
        for c in range(D_FF // FF_CHUNK):
            cols = slice(c * FF_CHUNK, (c + 1) * FF_CHUNK)
            gate = jnp.dot(hn, wg_ref[:, cols], preferred_element_type=F32)
            up = jnp.dot(hn, wu_ref[:, cols], preferred_element_type=F32)
            ff = gate * jax.nn.sigmoid(gate) * up
            ff_out = ff_out + jnp.dot(ff.astype(BF16), wd_ref[cols, :], preferred_element_type=F32)
        h_new = h_mid + _rms_norm(ff_out, gain_ffn_ref[...])
        hres_out[rows, :] = h_new
        hnorm_out[rows, :] = _rms_norm(h_new, gain_next_ref[...]).astype(hnorm_out.dtype)


def _mix_ffn_call(proj3, y_attn, h_res3, band, w_pool, pool_scale, w_out, gain_mix, gain_pre,
                  w_gate, w_up, w_down, gain_ffn, gain_next):
    batch, length, _ = proj3.shape
    halo_blocks = FUSED_ROW_TILE // POOL_HALO
    row_spec = lambda col: pl.BlockSpec((None, FUSED_ROW_TILE, D_MODEL), lambda b, i: (b, i, col))
    const = lambda shape: pl.BlockSpec(shape, lambda b, i: (0,) * len(shape),
                                       pipeline_mode=pl.Buffered(1))
    return pl.pallas_call(
        _mix_ffn_kernel,
        grid=(batch, length // FUSED_ROW_TILE),
        in_specs=[
            row_spec(COL_POOL),
            pl.BlockSpec((None, POOL_HALO, D_MODEL),
                         lambda b, i: (b, jnp.maximum(i * halo_blocks - 1, 0), COL_POOL)),
            row_spec(COL_GPOOL),
            row_spec(COL_GATTN),
            row_spec(0),
            row_spec(0),
            const((len(POOL_WINDOWS), MIX_BLOCK, MIX_BLOCK)),
            const((len(POOL_WINDOWS), POOL_GROUP_DIM, POOL_GROUP_DIM)),
            const((1, D_MODEL)),
            const((D_MODEL, D_MODEL)),
            const((1, D_MODEL)),
            const((1, D_MODEL)),
            const((D_MODEL, D_FF)), const((D_MODEL, D_FF)), const((D_FF, D_MODEL)),
            const((1, D_MODEL)),
            const((1, D_MODEL)),
        ],
        out_specs=[row_spec(0), row_spec(0)],
        out_shape=[jax.ShapeDtypeStruct((batch, length, D_MODEL), F32),
                   jax.ShapeDtypeStruct((batch, length, D_MODEL), BF16)],
        compiler_params=_params("parallel", "parallel"),
        name="mix_ffn",
    )(proj3, proj3, proj3, proj3, y_attn, h_res3, band, w_pool, pool_scale, w_out, gain_mix,
      gain_pre, w_gate, w_up, w_down, gain_ffn, gain_next)


def kernel(x, meta_tokens, norm_mix_pre, norm_mix_post, norm_ffn_pre, norm_ffn_post,
           w_in, b_forget, w_pool, pool_scale, w_out, w_ffn_gate, w_ffn_up, w_ffn_down):
    batch, seq, _ = x.shape
    depth = w_in.shape[0]
    tokens = META_TOKENS + seq
    length = -(-tokens // ROW_TILE) * ROW_TILE
    assert (batch * length) % PROJ_ROW_TILE == 0
    meta = jnp.broadcast_to(meta_tokens[None].astype(x.dtype), (batch, META_TOKENS, D_MODEL))
    tail = jnp.zeros((batch, length - tokens, D_MODEL), x.dtype)
    h_res = jnp.concatenate([meta, x, tail], axis=1).reshape(batch * length, D_MODEL)
    gain = lambda g: g.reshape(1, D_MODEL).astype(F32)
    sec = lambda w, n: w[:, n * D_MODEL:(n + 1) * D_MODEL]
    gates_at = 4 * D_MODEL + N_HEADS
    lane_head = jnp.minimum(jnp.arange(LANES) // BIAS_PIECES, N_HEADS - 1)
    head_indicator = (jnp.arange(D_MODEL)[:, None] // HEAD_DIM == jnp.arange(LANES)[None, :]).astype(BF16)

    pool_band = _pool_band()

    h_norm = _norm_call(h_res, gain(norm_mix_pre[0]))
    for l in range(depth):
        w = w_in[l]
        w_rows = jnp.concatenate([sec(w, 0), sec(w, 2), w[:, gates_at:]], axis=1).astype(BF16)
        w_t = jnp.concatenate([sec(w, 1), sec(w, 3)], axis=1).T.astype(BF16)
        w_forget = w[:, 4 * D_MODEL:gates_at]
        wf_rep = w_forget[:, lane_head].astype(BF16)
        bias_rep = b_forget[l][lane_head].reshape(1, LANES).astype(F32)

        h_norm3 = h_norm.reshape(batch, length, D_MODEL)
        proj3 = _proj_call(h_norm, w_rows).reshape(batch, length, N_ROW_BLOCKS * D_MODEL)
        proj_t = _proj_t_call(h_norm3, w_t)
        key_bias, bias_end, bias_t = _forget_call(h_norm3, wf_rep, bias_rep)
        key_norms = _key_norm_call(proj3, head_indicator)
        key_norm_max = key_norms[:, :, 0, :N_HEADS].reshape(-1)
        bias_max = bias_end[:, :, 0, 0:ONES_LANE:BIAS_PIECES].reshape(-1)
        y_attn = _attn_call(proj3, proj_t, key_bias, bias_t, key_norm_max, bias_max)
        next_gain = norm_mix_pre[(l + 1) % depth]
        h_res3, h_norm3 = _mix_ffn_call(
            proj3, y_attn, h_res.reshape(batch, length, D_MODEL), pool_band,
            w_pool[l].astype(BF16), gain(pool_scale[l]), w_out[l].astype(BF16),
            gain(norm_mix_post[l]), gain(norm_ffn_pre[l]),
            w_ffn_gate[l].astype(BF16), w_ffn_up[l].astype(BF16), w_ffn_down[l].astype(BF16),
            gain(norm_ffn_post[l]), gain(next_gain))
        h_res = h_res3.reshape(batch * length, D_MODEL)
        h_norm = h_norm3.reshape(batch * length, D_MODEL)
    return h_res.reshape(batch, length, D_MODEL)[:, META_TOKENS:tokens]
```

```python
import math

import jax
import jax.numpy as jnp
from jax import lax
from jax.experimental import pallas as pl
from jax.experimental.pallas import tpu as pltpu

D_MODEL = 1024
N_HEADS = 16
HEAD_DIM = 64
META_TOKENS = 16
POOL_WINDOWS = (2, 4, 8, 16)
POOL_GROUP_DIM = D_MODEL // len(POOL_WINDOWS)
POOL_HALO = max(POOL_WINDOWS)
D_FF = 2816
FF_CHUNK = 1408
RMS_EPS = 1e-6
MASK_VALUE = -1e30
LOG2_E = math.log2(math.e)

LANES = 128
SUBLANES = 8
MXU_DIM = 256
ROW_TILE = 3 * MXU_DIM
CHUNK_COLS = 2 * MXU_DIM
TILES_PER_TRIP = 4
FUSED_ROW_TILE = ROW_TILE
MIX_BLOCK = FUSED_ROW_TILE // 4
MIX_LEAD = 2
PROJ_ROW_TILE = 2 * ROW_TILE
PROJ_T_STEPS = 3
N_ROW_BLOCKS = 4
COL_POOL, COL_K, COL_GPOOL, COL_GATTN = range(N_ROW_BLOCKS)
N_T_BLOCKS = 2
T_Q, T_V = range(N_T_BLOCKS)
HEADS_PER_STEP = LANES // HEAD_DIM
BIAS_PIECES = 3
ONES_LANE = BIAS_PIECES * N_HEADS
REF_ROW = LANES + ONES_LANE
PACKED_ROWS = 16
NORM_SLACK = 1.01
BIAS_SLACK = 0.05
REFERENCE_HEADROOM = 32.0
DENOM_FLOOR = 2.0 ** -80
VMEM_LIMIT_BYTES = 56 * 1024 * 1024

F32 = jnp.float32
BF16 = jnp.bfloat16


def _rms_norm(x, gain):
    return x * lax.rsqrt(jnp.mean(x * x, axis=-1, keepdims=True) + RMS_EPS) * gain


def _params(*semantics):
    return pltpu.CompilerParams(dimension_semantics=semantics, vmem_limit_bytes=VMEM_LIMIT_BYTES)


def _split_bf16(x):
    hi = x.astype(BF16)
    rest = x - hi.astype(F32)
    mid = rest.astype(BF16)
    lo = (rest - mid.astype(F32)).astype(BF16)
    return hi, mid, lo


def _norm_kernel(h_ref, gain_ref, out_ref):
    out_ref[...] = _rms_norm(h_ref[...], gain_ref[...]).astype(out_ref.dtype)


def _norm_call(h_res, gain):
    rows = h_res.shape[0]
    return pl.pallas_call(
        _norm_kernel,
        grid=(rows // ROW_TILE,),
        in_specs=[pl.BlockSpec((ROW_TILE, D_MODEL), lambda i: (i, 0)),
                  pl.BlockSpec((1, D_MODEL), lambda i: (0, 0))],
        out_specs=pl.BlockSpec((ROW_TILE, D_MODEL), lambda i: (i, 0)),
        out_shape=jax.ShapeDtypeStruct((rows, D_MODEL), BF16),
        compiler_params=_params("parallel"),
        name="pre_norm",
    )(h_res, gain)


def _proj_kernel(h_ref, w_ref, out_ref):
    out_ref[...] = jnp.dot(h_ref[...], w_ref[...], preferred_element_type=F32).astype(out_ref.dtype)


def _proj_call(h_norm, w_rows):
    rows = h_norm.shape[0]
    return pl.pallas_call(
        _proj_kernel,
        grid=(rows // PROJ_ROW_TILE, N_ROW_BLOCKS),
        in_specs=[pl.BlockSpec((PROJ_ROW_TILE, D_MODEL), lambda i, j: (i, 0)),
                  pl.BlockSpec((D_MODEL, D_MODEL), lambda i, j: (0, j))],
        out_specs=pl.BlockSpec((PROJ_ROW_TILE, D_MODEL), lambda i, j: (i, j)),
        out_shape=jax.ShapeDtypeStruct((rows, N_ROW_BLOCKS * D_MODEL), BF16),
        compiler_params=_params("parallel", "arbitrary"),
        name="in_proj",
    )(h_norm, w_rows)


def _proj_t_kernel(h_ref, wt_ref, out_ref):
    acc = lax.dot_general(wt_ref[...], h_ref[...], (((1,), (1,)), ((), ())),
                          preferred_element_type=F32)
    scale = jnp.where(pl.program_id(2) == T_Q, LOG2_E * HEAD_DIM ** -0.5, 1.0).astype(F32)
    out_ref[...] = (acc * scale).astype(out_ref.dtype)


def _proj_t_call(h_norm3, w_t):
    batch, length, _ = h_norm3.shape
    tile = length // PROJ_T_STEPS
    assert tile * PROJ_T_STEPS == length and tile % LANES == 0
    return pl.pallas_call(
        _proj_t_kernel,
        grid=(batch, PROJ_T_STEPS, N_T_BLOCKS),
        in_specs=[pl.BlockSpec((None, tile, D_MODEL), lambda b, i, j: (b, i, 0)),
                  pl.BlockSpec((D_MODEL, D_MODEL), lambda b, i, j: (j, 0))],
        out_specs=pl.BlockSpec((None, D_MODEL, tile), lambda b, i, j: (b, j, i)),
        out_shape=jax.ShapeDtypeStruct((batch, N_T_BLOCKS * D_MODEL, length), BF16),
        compiler_params=_params("parallel", "parallel", "arbitrary"),
        name="in_proj_t",
    )(h_norm3, w_t)


def _forget_kernel(h_ref, wf_ref, bias_ref, out_ref, end_ref, t_ref, carry_ref):
    @pl.when(pl.program_id(1) == 0)
    def _():
        carry_ref[...] = jnp.zeros_like(carry_ref)

    logit = jnp.dot(h_ref[...], wf_ref[...], preferred_element_type=F32) + bias_ref[...]
    log_f = jnp.minimum(logit, 0.0) - jnp.log(1.0 + jnp.exp(-jnp.abs(logit)))
    tile = log_f.shape[0]
    dst = lax.broadcasted_iota(jnp.int32, (MXU_DIM, MXU_DIM), 0)
    src = lax.broadcasted_iota(jnp.int32, (MXU_DIM, MXU_DIM), 1)
    prefix = (src <= dst).astype(BF16)
    pieces = _split_bf16(log_f)
    carry = carry_ref[...]
    blocks = []
    for r0 in range(0, tile, MXU_DIM):
        block = jnp.broadcast_to(carry, (MXU_DIM, LANES))
        for piece in pieces:
            block = block + jnp.dot(prefix, piece[r0:r0 + MXU_DIM, :], preferred_element_type=F32)
        carry = block[MXU_DIM - 1:MXU_DIM, :]
        blocks.append(block)
    total = jnp.concatenate(blocks, axis=0)
    carry_ref[...] = carry
    bias = -LOG2_E * total
    end_ref[...] = jnp.broadcast_to(bias[tile - 1:tile, :], end_ref.shape)
    hi, mid, lo = _split_bf16(bias)
    lane = lax.broadcasted_iota(jnp.int32, total.shape, 1)
    piece = lane % BIAS_PIECES
    out = jnp.where(piece == 0, hi, jnp.where(piece == 1, mid, lo))
    tail = jnp.where(lane < ONES_LANE + BIAS_PIECES, 1.0, 0.0).astype(BF16)
    out_ref[...] = jnp.where(lane < ONES_LANE, out, tail)
    head = lax.broadcasted_iota(jnp.int32, (N_HEADS, LANES), 0)
    lane_h = lax.broadcasted_iota(jnp.int32, (N_HEADS, LANES), 1)
    pick = (lane_h >= head * BIAS_PIECES) & (lane_h < (head + 1) * BIAS_PIECES)
    t_ref[...] = lax.dot_general(jnp.where(pick, 1.0, 0.0).astype(BF16), out,
                                 (((1,), (1,)), ((), ())), preferred_element_type=F32)


def _forget_call(h_norm3, wf_rep, bias_rep):
    batch, length, _ = h_norm3.shape
    n_tiles = length // ROW_TILE
    return pl.pallas_call(
        _forget_kernel,
        grid=(batch, n_tiles),
        in_specs=[pl.BlockSpec((None, ROW_TILE, D_MODEL), lambda b, i: (b, i, 0)),
                  pl.BlockSpec((D_MODEL, LANES), lambda b, i: (0, 0)),
                  pl.BlockSpec((1, LANES), lambda b, i: (0, 0))],
        out_specs=[pl.BlockSpec((None, ROW_TILE, LANES), lambda b, i: (b, i, 0)),
                   pl.BlockSpec((None, None, SUBLANES, LANES), lambda b, i: (b, i, 0, 0)),
                   pl.BlockSpec((None, N_HEADS, ROW_TILE), lambda b, i: (b, 0, i))],
        out_shape=[jax.ShapeDtypeStruct((batch, length, LANES), BF16),
                   jax.ShapeDtypeStruct((batch, n_tiles, SUBLANES, LANES), F32),
                   jax.ShapeDtypeStruct((batch, N_HEADS, length), F32)],
        scratch_shapes=[pltpu.VMEM((1, LANES), F32)],
        compiler_params=_params("parallel", "arbitrary"),
        name="forget_bias",
    )(h_norm3, wf_rep, bias_rep)


def _key_norm_kernel(k_ref, head_ref, out_ref):
    k = k_ref[...].astype(F32)
    sums = jnp.dot((k * k).astype(BF16), head_ref[...], preferred_element_type=F32)
    out_ref[...] = jnp.broadcast_to(jnp.sqrt(jnp.max(sums, axis=0, keepdims=True)), out_ref.shape)


def _key_norm_call(proj3, head_indicator):
    batch, length, _ = proj3.shape
    n_tiles = length // ROW_TILE
    return pl.pallas_call(
        _key_norm_kernel,
        grid=(batch, n_tiles),
        in_specs=[pl.BlockSpec((None, ROW_TILE, D_MODEL), lambda b, i: (b, i, COL_K)),
                  pl.BlockSpec((D_MODEL, LANES), lambda b, i: (0, 0))],
        out_specs=pl.BlockSpec((None, None, SUBLANES, LANES), lambda b, i: (b, i, 0, 0)),
        out_shape=jax.ShapeDtypeStruct((batch, n_tiles, SUBLANES, LANES), F32),
        compiler_params=_params("parallel", "parallel"),
        name="key_norms",
    )(proj3, head_indicator)


def _attn_kernel(kmax_ref, gend_ref, gt_ref, qt_ref, k_ref, fb_ref, vt_ref, out_ref,
                 acc_ref, qaug_ref, *chunk_refs):
    bi = pl.program_id(0)
    pair = pl.program_id(1)
    qi = pl.program_id(2)
    n_tiles = pl.num_programs(2)
    tile = qt_ref.shape[1]
    both = HEADS_PER_STEP * tile
    n_chunks = both // CHUNK_COLS
    s_refs, p_refs = chunk_refs[:n_chunks], chunk_refs[n_chunks:]

    feat = lax.broadcasted_iota(jnp.int32, (LANES, tile), 0)
    q_norms = []
    for hh in range(HEADS_PER_STEP):
        head = pair * HEADS_PER_STEP + hh
        in_head = (feat >= hh * HEAD_DIM) & (feat < (hh + 1) * HEAD_DIM)
        q_top = jnp.where(in_head, qt_ref[...], jnp.zeros_like(qt_ref[...]))
        is_piece = (feat >= head * BIAS_PIECES) & (feat < (head + 1) * BIAS_PIECES)
        qaug_ref[:, hh * tile:(hh + 1) * tile] = jnp.concatenate(
            [q_top, is_piece.astype(BF16)], axis=0)
        q_head = qt_ref[hh * HEAD_DIM:(hh + 1) * HEAD_DIM, :].astype(F32)
        q_norms.append(jnp.sqrt(jnp.sum(q_head * q_head, axis=0, keepdims=True)))
    q_norm = jnp.concatenate(q_norms, axis=1)
    second_head = lax.broadcasted_iota(jnp.int32, (1, both), 1) >= tile
    g_diagonal = jnp.concatenate([gt_ref[hh:hh + 1, :] for hh in range(HEADS_PER_STEP)], axis=1)

    def head_segments(c):
        segments = []
        start = c * CHUNK_COLS
        while start < (c + 1) * CHUNK_COLS:
            hh = start // tile
            stop = min((c + 1) * CHUNK_COLS, (hh + 1) * tile)
            segments.append((hh, slice(start - hh * tile, stop - hh * tile),
                             slice(start - c * CHUNK_COLS, stop - c * CHUNK_COLS)))
            start = stop
        return segments

    def scores(c, k_start, on_diagonal):
        keys = jnp.concatenate([k_ref[pl.ds(k_start, tile), :],
                                fb_ref[pl.ds(k_start, tile), :]], axis=1)
        s_t = jnp.dot(keys, qaug_ref[:, c * CHUNK_COLS:(c + 1) * CHUNK_COLS],
                      preferred_element_type=F32)
        if on_diagonal:
            key_row = lax.broadcasted_iota(jnp.int32, s_t.shape, 0)
            col = lax.broadcasted_iota(jnp.int32, s_t.shape, 1) + c * CHUNK_COLS
            query_col = jnp.where(col >= tile, col - tile, col)
            s_t = jnp.where(key_row <= query_col, s_t, MASK_VALUE)
        return s_t

    def value_stage(c, k_start, alpha):
        for hh, head_cols, chunk_cols in head_segments(c):
            v_aug = jnp.concatenate(
                [vt_ref[hh * HEAD_DIM:(hh + 1) * HEAD_DIM, pl.ds(k_start, tile)],
                 jnp.ones((PACKED_ROWS, tile), BF16)], axis=0)
            acc_ref[hh, :, head_cols] = (
                alpha[:, c * CHUNK_COLS:(c + 1) * CHUNK_COLS][:, chunk_cols]
                * acc_ref[hh, :, head_cols]
                + jnp.dot(v_aug, p_refs[c][:, chunk_cols], preferred_element_type=F32))

    def set_reference(r):
        hi, mid, lo = (piece.astype(F32) for piece in _split_bf16(-r))
        row = lax.broadcasted_iota(jnp.int32, (PACKED_ROWS, both), 0)
        pieces = jnp.where(row == 0, hi, jnp.where(row == 1, mid, jnp.where(row == 2, lo, 0.0)))
        qaug_ref[REF_ROW:REF_ROW + PACKED_ROWS, :] = pieces.astype(BF16)

    def bound_tile(kj, r_prev, on_diagonal):
        k_start = pl.multiple_of(kj * tile, LANES)
        base = (bi * n_tiles + kj) * N_HEADS + pair * HEADS_PER_STEP
        k_max = jnp.where(second_head, kmax_ref[base + 1], kmax_ref[base])
        if on_diagonal:
            g_max = g_diagonal
        else:
            g_max = jnp.where(second_head, gend_ref[base + 1], gend_ref[base])
        bound = q_norm * k_max * NORM_SLACK + g_max + BIAS_SLACK
        r_new = jnp.maximum(r_prev, bound - REFERENCE_HEADROOM)
        alpha = jnp.exp2(r_prev - r_new)
        set_reference(r_new)

        if on_diagonal:
            diagonal_groups(k_start, alpha)
            return r_new

        def probabilities(c):
            p_refs[c][...] = jnp.exp2(scores(c, k_start, False)).astype(BF16)

        probabilities(0)
        probabilities(1)
        value_stage(0, k_start, alpha)
        probabilities(2)
        value_stage(1, k_start, alpha)
        value_stage(2, k_start, alpha)
        return r_new

    def diagonal_groups(k_start, alpha):
        n_groups = tile // MXU_DIM
        assert n_groups == n_chunks

        def group_cols(hh, g):
            return slice(hh * tile + g * MXU_DIM, hh * tile + (g + 1) * MXU_DIM)

        def probabilities(g):
            n_keys = (g + 1) * MXU_DIM
            keys = jnp.concatenate([k_ref[pl.ds(k_start, n_keys), :],
                                    fb_ref[pl.ds(k_start, n_keys), :]], axis=1)
            queries = jnp.concatenate(
                [qaug_ref[:, group_cols(hh, g)] for hh in range(HEADS_PER_STEP)], axis=1)
            s_t = jnp.dot(keys, queries, preferred_element_type=F32)
            key_row = lax.broadcasted_iota(jnp.int32, s_t.shape, 0)
            col = lax.broadcasted_iota(jnp.int32, s_t.shape, 1)
            query_col = jnp.where(col >= MXU_DIM, col - MXU_DIM, col) + g * MXU_DIM
            s_t = jnp.where(key_row <= query_col, s_t, MASK_VALUE)
            p_refs[g][0:n_keys, :] = jnp.exp2(s_t).astype(BF16)

        def values(g):
            n_keys = (g + 1) * MXU_DIM
            for hh in range(HEADS_PER_STEP):
                head_cols = slice(g * MXU_DIM, (g + 1) * MXU_DIM)
                v_aug = jnp.concatenate(
                    [vt_ref[hh * HEAD_DIM:(hh + 1) * HEAD_DIM, pl.ds(k_start, n_keys)],
                     jnp.ones((PACKED_ROWS, n_keys), BF16)], axis=0)
                acc_ref[hh, :, head_cols] = (
                    alpha[:, group_cols(hh, g)] * acc_ref[hh, :, head_cols]
                    + jnp.dot(v_aug, p_refs[g][0:n_keys, hh * MXU_DIM:(hh + 1) * MXU_DIM],
                              preferred_element_type=F32))

        probabilities(2)
        probabilities(1)
        values(2)
        probabilities(0)
        values(1)
        values(0)

    assert n_chunks == 3
    acc_ref[...] = jnp.zeros_like(acc_ref)
    r_run = jnp.full((1, both), MASK_VALUE, F32)
    def tiles(first, count, r):
        for t in range(count):
            r = bound_tile(first + t, r, False)
        return r

    r_run = lax.fori_loop(0, qi // TILES_PER_TRIP,
                          lambda kk, r: tiles(kk * TILES_PER_TRIP, TILES_PER_TRIP, r), r_run)
    done = qi // TILES_PER_TRIP * TILES_PER_TRIP
    count = TILES_PER_TRIP // 2
    while count:
        has = ((qi - done) & count) != 0
        r_run = lax.cond(has, lambda r, at=done, n=count: tiles(at, n, r), lambda r: r, r_run)
        done = done + jnp.where(has, count, 0)
        count //= 2
    bound_tile(qi, r_run, True)

    def write_output():
        heads_out = [acc_ref[hh, :HEAD_DIM, :] / acc_ref[hh, HEAD_DIM:HEAD_DIM + 1, :]
                     for hh in range(HEADS_PER_STEP)]
        out_ref[...] = jnp.concatenate(heads_out, axis=0).T.astype(out_ref.dtype)

    write_output()
    smallest_denominator = jnp.min(jnp.minimum(acc_ref[0, HEAD_DIM:HEAD_DIM + 1, :],
                                               acc_ref[1, HEAD_DIM:HEAD_DIM + 1, :]))

    @pl.when(jnp.logical_not(smallest_denominator > DENOM_FLOOR))
    def _():
        def max_tile(kj, m_prev, on_diagonal):
            k_start = pl.multiple_of(kj * tile, LANES)
            m_tile = []
            for c in range(n_chunks):
                s_t = scores(c, k_start, on_diagonal)
                s_refs[c][...] = s_t
                m_tile.append(jnp.max(s_t, axis=0, keepdims=True))
            m_new = jnp.maximum(m_prev, jnp.concatenate(m_tile, axis=1))
            alpha = jnp.exp2(m_prev - m_new)
            for c in range(n_chunks):
                cols = slice(c * CHUNK_COLS, (c + 1) * CHUNK_COLS)
                p_refs[c][...] = jnp.exp2(s_refs[c][...] - m_new[:, cols]).astype(BF16)
                value_stage(c, k_start, alpha)
            return m_new

        acc_ref[...] = jnp.zeros_like(acc_ref)
        set_reference(jnp.zeros((1, both), F32))
        m_run = jnp.full((1, both), MASK_VALUE, F32)
        m_run = lax.fori_loop(0, qi, lambda kj, m: max_tile(kj, m, False), m_run)
        max_tile(qi, m_run, True)
        write_output()


def _attn_call(proj3, proj_t, key_bias, bias_t, key_norm_max, bias_max):
    batch, length, _ = proj3.shape
    pairs = N_HEADS // HEADS_PER_STEP
    bias_t = bias_t.reshape(batch, pairs, HEADS_PER_STEP, length)
    blocks_per_section = D_MODEL // LANES
    n_chunks = HEADS_PER_STEP * ROW_TILE // CHUNK_COLS
    return pl.pallas_call(
        _attn_kernel,
        grid=(batch, pairs, length // ROW_TILE),
        in_specs=[
            pl.BlockSpec(memory_space=pltpu.SMEM),
            pl.BlockSpec(memory_space=pltpu.SMEM),
            pl.BlockSpec((None, None, HEADS_PER_STEP, ROW_TILE), lambda b, p, i: (b, p, 0, i)),
            pl.BlockSpec((None, LANES, ROW_TILE), lambda b, p, i: (b, T_Q * blocks_per_section + p, i)),
            pl.BlockSpec((None, length, LANES), lambda b, p, i: (b, 0, COL_K * blocks_per_section + p)),
            pl.BlockSpec((None, length, LANES), lambda b, p, i: (b, 0, 0)),
            pl.BlockSpec((None, LANES, length), lambda b, p, i: (b, T_V * blocks_per_section + p, 0)),
        ],
        out_specs=pl.BlockSpec((None, ROW_TILE, LANES), lambda b, p, i: (b, i, p)),
        out_shape=jax.ShapeDtypeStruct((batch, length, D_MODEL), BF16),
        scratch_shapes=[
            pltpu.VMEM((HEADS_PER_STEP, HEAD_DIM + PACKED_ROWS, ROW_TILE), F32),
            pltpu.VMEM((2 * LANES, HEADS_PER_STEP * ROW_TILE), BF16),
        ] + [pltpu.VMEM((ROW_TILE, CHUNK_COLS), dtype) for dtype in (F32, BF16)
             for _ in range(n_chunks)],
        compiler_params=_params("parallel", "parallel", "arbitrary"),
        name="forgetting_attention",
    )(key_norm_max, bias_max, bias_t, proj_t, proj3, key_bias, proj_t)


def _mix_kernel(u_ref, halo_ref, gp_ref, ga_ref, ya_ref, h_ref, band_ref, wp_ref, ps_ref, wo_ref,
                gain_post_ref, gain_next_ref, hres_out, hnorm_out):
    li = pl.program_id(1)
    for r0 in range(0, u_ref.shape[0], MIX_BLOCK):
        rows = slice(r0, r0 + MIX_BLOCK)
        head = u_ref[r0:r0 + POOL_HALO, :].astype(F32)
        if r0 == 0:
            halo = jnp.where(li > 0, halo_ref[...].astype(F32), 0.0)
        else:
            halo = u_ref[r0 - POOL_HALO:r0, :].astype(F32)
        ext = jnp.concatenate([halo, head], axis=0)
        pos1 = (li * u_ref.shape[0] + r0 + 1
                + lax.broadcasted_iota(jnp.int32, (POOL_HALO, 1), 0))
        pooled = []
        for g, window in enumerate(POOL_WINDOWS):
            lanes = slice(g * POOL_GROUP_DIM, (g + 1) * POOL_GROUP_DIM)
            run = ext[:, lanes]
            width = 1
            while width < window:
                run = run[width:] + run[:-width]
                width *= 2
            first = POOL_HALO + 1 - window
            count = jnp.minimum(pos1, window).astype(F32)
            diff_head = run[first:first + POOL_HALO] / count - head[:, lanes]
            diff = jnp.dot(band_ref[g], u_ref[rows, lanes], preferred_element_type=F32)
            diff = jnp.concatenate([diff_head, diff[POOL_HALO:]], axis=0)
            pooled.append(jnp.dot(diff.astype(BF16), wp_ref[g], preferred_element_type=F32))
        y_pool = jnp.concatenate(pooled, axis=1) * ps_ref[...]
        merged = (jax.nn.sigmoid(gp_ref[rows, :].astype(F32)) * y_pool
                  + jax.nn.sigmoid(ga_ref[rows, :].astype(F32)) * ya_ref[rows, :].astype(F32))
        mix = jnp.dot(merged.astype(BF16), wo_ref[...], preferred_element_type=F32)
        h_new = h_ref[rows, :] + _rms_norm(mix, gain_post_ref[...])
        hres_out[rows, :] = h_new
        hnorm_out[rows, :] = _rms_norm(h_new, gain_next_ref[...]).astype(hnorm_out.dtype)


def _pool_band():
    row = jnp.arange(MIX_BLOCK)[:, None]
    col = jnp.arange(MIX_BLOCK)[None, :]
    bands = [((col <= row) & (col > row - w)).astype(F32) / w - (col == row).astype(F32)
             for w in POOL_WINDOWS]
    return jnp.stack(bands).astype(BF16)


def _mix_call(proj3, y_attn, h_res3, band, w_pool, pool_scale, w_out, gain_post, gain_next):
    batch, length, _ = proj3.shape
    halo_blocks = ROW_TILE // POOL_HALO
    row_spec = lambda col: pl.BlockSpec((None, ROW_TILE, D_MODEL), lambda b, i: (b, i, col))
    const = lambda shape: pl.BlockSpec(shape, lambda b, i: (0,) * len(shape),
                                       pipeline_mode=pl.Buffered(1))
    return pl.pallas_call(
        _mix_kernel,
        grid=(batch, length // ROW_TILE),
        in_specs=[
            row_spec(COL_POOL),
            pl.BlockSpec((None, POOL_HALO, D_MODEL),
                         lambda b, i: (b, jnp.maximum(i * halo_blocks - 1, 0), COL_POOL)),
            row_spec(COL_GPOOL),
            row_spec(COL_GATTN),
            row_spec(0),
            row_spec(0),
            const((len(POOL_WINDOWS), MIX_BLOCK, MIX_BLOCK)),
            const((len(POOL_WINDOWS), POOL_GROUP_DIM, POOL_GROUP_DIM)),
            const((1, D_MODEL)),
            const((D_MODEL, D_MODEL)),
            const((1, D_MODEL)),
            const((1, D_MODEL)),
        ],
        out_specs=[row_spec(0), row_spec(0)],
        out_shape=[jax.ShapeDtypeStruct((batch, length, D_MODEL), F32),
                   jax.ShapeDtypeStruct((batch, length, D_MODEL), BF16)],
        compiler_params=_params("parallel", "parallel"),
        name="pool_merge_out",
    )(proj3, proj3, proj3, proj3, y_attn, h_res3, band, w_pool, pool_scale, w_out, gain_post,
      gain_next)


def _ffn_kernel(hn_ref, h_ref, wg_ref, wu_ref, wd_ref, gain_post_ref, gain_next_ref,
                hres_out, hnorm_out):
    hn = hn_ref[...]
    ff_out = jnp.zeros(h_ref.shape, F32)
    for c in range(D_FF // FF_CHUNK):
        cols = slice(c * FF_CHUNK, (c + 1) * FF_CHUNK)
        gate = jnp.dot(hn, wg_ref[:, cols], preferred_element_type=F32)
        up = jnp.dot(hn, wu_ref[:, cols], preferred_element_type=F32)
        ff = gate * jax.nn.sigmoid(gate) * up
        ff_out = ff_out + jnp.dot(ff.astype(BF16), wd_ref[cols, :], preferred_element_type=F32)
    h_new = h_ref[...] + _rms_norm(ff_out, gain_post_ref[...])
    hres_out[...] = h_new
    hnorm_out[...] = _rms_norm(h_new, gain_next_ref[...]).astype(hnorm_out.dtype)


def _ffn_call(h_norm, h_res, w_gate, w_up, w_down, gain_post, gain_next):
    rows = h_res.shape[0]
    row_spec = pl.BlockSpec((ROW_TILE, D_MODEL), lambda i: (i, 0))
    resident = lambda shape: pl.BlockSpec(shape, lambda i: (0, 0), pipeline_mode=pl.Buffered(1))
    return pl.pallas_call(
        _ffn_kernel,
        grid=(rows // ROW_TILE,),
        in_specs=[row_spec, row_spec,
                  resident((D_MODEL, D_FF)), resident((D_MODEL, D_FF)), resident((D_FF, D_MODEL)),
                  resident((1, D_MODEL)), resident((1, D_MODEL))],
        out_specs=[row_spec, row_spec],
        out_shape=[jax.ShapeDtypeStruct((rows, D_MODEL), F32),
                   jax.ShapeDtypeStruct((rows, D_MODEL), BF16)],
        compiler_params=_params("parallel"),
        name="swiglu_ffn",
    )(h_norm, h_res, w_gate, w_up, w_down, gain_post, gain_next)


def _mix_ffn_kernel(u_ref, halo_ref, gp_ref, ga_ref, ya_ref, h_ref, band_ref, wp_ref, ps_ref,
                    wo_ref, gain_mix_ref, gain_pre_ref, wg_ref, wu_ref, wd_ref, gain_ffn_ref,
                    gain_next_ref, hres_out, hnorm_out):
    li = pl.program_id(1)

    def mix_block(r0):
        rows = slice(r0, r0 + MIX_BLOCK)
        head = u_ref[r0:r0 + POOL_HALO, :].astype(F32)
        if r0 == 0:
            halo = jnp.where(li > 0, halo_ref[...].astype(F32), 0.0)
        else:
            halo = u_ref[r0 - POOL_HALO:r0, :].astype(F32)
        ext = jnp.concatenate([halo, head], axis=0)
        pos1 = (li * u_ref.shape[0] + r0 + 1
                + lax.broadcasted_iota(jnp.int32, (POOL_HALO, 1), 0))
        pooled = []
        for g, window in enumerate(POOL_WINDOWS):
            lanes = slice(g * POOL_GROUP_DIM, (g + 1) * POOL_GROUP_DIM)
            run = ext[:, lanes]
            width = 1
            while width < window:
                run = run[width:] + run[:-width]
                width *= 2
            first = POOL_HALO + 1 - window
            count = jnp.minimum(pos1, window).astype(F32)
            diff_head = run[first:first + POOL_HALO] / count - head[:, lanes]
            diff = jnp.dot(band_ref[g], u_ref[rows, lanes], preferred_element_type=F32)
            diff = jnp.concatenate([diff_head, diff[POOL_HALO:]], axis=0)
            pooled.append(jnp.dot(diff.astype(BF16), wp_ref[g], preferred_element_type=F32))
        y_pool = jnp.concatenate(pooled, axis=1) * ps_ref[...]
        merged = (jax.nn.sigmoid(gp_ref[rows, :].astype(F32)) * y_pool
                  + jax.nn.sigmoid(ga_ref[rows, :].astype(F32)) * ya_ref[rows, :].astype(F32))
        mix = jnp.dot(merged.astype(BF16), wo_ref[...], preferred_element_type=F32)
        return h_ref[rows, :] + _rms_norm(mix, gain_mix_ref[...])

    def ffn_block(r0, h_mid):
        rows = slice(r0, r0 + MIX_BLOCK)
        hn = _rms_norm(h_mid, gain_pre_ref[...]).astype(BF16)
        ff_out = jnp.zeros(h_mid.shape, F32)
        for c in range(D_FF // FF_CHUNK):
            cols = slice(c * FF_CHUNK, (c + 1) * FF_CHUNK)
            gate = jnp.dot(hn, wg_ref[:, cols], preferred_element_type=F32)
            up = jnp.dot(hn, wu_ref[:, cols], preferred_element_type=F32)
            ff = gate * jax.nn.sigmoid(gate) * up
            ff_out = ff_out + jnp.dot(ff.astype(BF16), wd_ref[cols, :], preferred_element_type=F32)
        h_new = h_mid + _rms_norm(ff_out, gain_ffn_ref[...])
        hres_out[rows, :] = h_new
        hnorm_out[rows, :] = _rms_norm(h_new, gain_next_ref[...]).astype(hnorm_out.dtype)

    n_blocks = u_ref.shape[0] // MIX_BLOCK
    pending = {}
    for step in range(n_blocks + MIX_LEAD):
        if step < n_blocks:
            pending[step] = mix_block(step * MIX_BLOCK)
        if step >= MIX_LEAD:
            ffn_block((step - MIX_LEAD) * MIX_BLOCK, pending.pop(step - MIX_LEAD))


def _mix_ffn_call(proj3, y_attn, h_res3, band, w_pool, pool_scale, w_out, gain_mix, gain_pre,
                  w_gate, w_up, w_down, gain_ffn, gain_next):
    batch, length, _ = proj3.shape
    halo_blocks = FUSED_ROW_TILE // POOL_HALO
    row_spec = lambda col: pl.BlockSpec((None, FUSED_ROW_TILE, D_MODEL), lambda b, i: (b, i, col))
    const = lambda shape: pl.BlockSpec(shape, lambda b, i: (0,) * len(shape),
                                       pipeline_mode=pl.Buffered(1))
    return pl.pallas_call(
        _mix_ffn_kernel,
        grid=(batch, length // FUSED_ROW_TILE),
        in_specs=[
            row_spec(COL_POOL),
            pl.BlockSpec((None, POOL_HALO, D_MODEL),
                         lambda b, i: (b, jnp.maximum(i * halo_blocks - 1, 0), COL_POOL)),
            row_spec(COL_GPOOL),
            row_spec(COL_GATTN),
            row_spec(0),
            row_spec(0),
            const((len(POOL_WINDOWS), MIX_BLOCK, MIX_BLOCK)),
            const((len(POOL_WINDOWS), POOL_GROUP_DIM, POOL_GROUP_DIM)),
            const((1, D_MODEL)),
            const((D_MODEL, D_MODEL)),
            const((1, D_MODEL)),
            const((1, D_MODEL)),
            const((D_MODEL, D_FF)), const((D_MODEL, D_FF)), const((D_FF, D_MODEL)),
            const((1, D_MODEL)),
            const((1, D_MODEL)),
        ],
        out_specs=[row_spec(0), row_spec(0)],
        out_shape=[jax.ShapeDtypeStruct((batch, length, D_MODEL), F32),
                   jax.ShapeDtypeStruct((batch, length, D_MODEL), BF16)],
        compiler_params=_params("parallel", "parallel"),
        name="mix_ffn",
    )(proj3, proj3, proj3, proj3, y_attn, h_res3, band, w_pool, pool_scale, w_out, gain_mix,
      gain_pre, w_gate, w_up, w_down, gain_ffn, gain_next)


def kernel(x, meta_tokens, norm_mix_pre, norm_mix_post, norm_ffn_pre, norm_ffn_post,
           w_in, b_forget, w_pool, pool_scale, w_out, w_ffn_gate, w_ffn_up, w_ffn_down):
    batch, seq, _ = x.shape
    depth = w_in.shape[0]
    tokens = META_TOKENS + seq
    length = -(-tokens // ROW_TILE) * ROW_TILE
    assert (batch * length) % PROJ_ROW_TILE == 0
    meta = jnp.broadcast_to(meta_tokens[None].astype(x.dtype), (batch, META_TOKENS, D_MODEL))
    tail = jnp.zeros((batch, length - tokens, D_MODEL), x.dtype)
    h_res = jnp.concatenate([meta, x, tail], axis=1).reshape(batch * length, D_MODEL)
    gain = lambda g: g.reshape(1, D_MODEL).astype(F32)
    sec = lambda w, n: w[:, n * D_MODEL:(n + 1) * D_MODEL]
    gates_at = 4 * D_MODEL + N_HEADS
    lane_head = jnp.minimum(jnp.arange(LANES) // BIAS_PIECES, N_HEADS - 1)
    head_indicator = (jnp.arange(D_MODEL)[:, None] // HEAD_DIM == jnp.arange(LANES)[None, :]).astype(BF16)

    pool_band = _pool_band()

    h_norm = _norm_call(h_res, gain(norm_mix_pre[0]))
    for l in range(depth):
        w = w_in[l]
        w_rows = jnp.concatenate([sec(w, 0), sec(w, 2), w[:, gates_at:]], axis=1).astype(BF16)
        w_t = jnp.concatenate([sec(w, 1), sec(w, 3)], axis=1).T.astype(BF16)
        w_forget = w[:, 4 * D_MODEL:gates_at]
        wf_rep = w_forget[:, lane_head].astype(BF16)
        bias_rep = b_forget[l][lane_head].reshape(1, LANES).astype(F32)

        h_norm3 = h_norm.reshape(batch, length, D_MODEL)
        proj3 = _proj_call(h_norm, w_rows).reshape(batch, length, N_ROW_BLOCKS * D_MODEL)
        proj_t = _proj_t_call(h_norm3, w_t)
        key_bias, bias_end, bias_t = _forget_call(h_norm3, wf_rep, bias_rep)
        key_norms = _key_norm_call(proj3, head_indicator)
        key_norm_max = key_norms[:, :, 0, :N_HEADS].reshape(-1)
        bias_max = bias_end[:, :, 0, 0:ONES_LANE:BIAS_PIECES].reshape(-1)
        y_attn = _attn_call(proj3, proj_t, key_bias, bias_t, key_norm_max, bias_max)
        next_gain = norm_mix_pre[(l + 1) % depth]
        h_res3, h_norm3 = _mix_ffn_call(
            proj3, y_attn, h_res.reshape(batch, length, D_MODEL), pool_band,
            w_pool[l].astype(BF16), gain(pool_scale[l]), w_out[l].astype(BF16),
            gain(norm_mix_post[l]), gain(norm_ffn_pre[l]),
            w_ffn_gate[l].astype(BF16), w_ffn_up[l].astype(BF16), w_ffn_down[l].astype(BF16),
            gain(norm_ffn_post[l]), gain(next_gain))
        h_res = h_res3.reshape(batch * length, D_MODEL)
        h_norm = h_norm3.reshape(batch * length, D_MODEL)
    return h_res.reshape(batch, length, D_MODEL)[:, META_TOKENS:tokens]
```

```python
import math

import jax
import jax.numpy as jnp
from jax import lax
from jax.experimental import pallas as pl
from jax.experimental.pallas import tpu as pltpu

D_MODEL = 1024
N_HEADS = 16
HEAD_DIM = 64
META_TOKENS = 16
POOL_WINDOWS = (2, 4, 8, 16)
POOL_GROUP_DIM = D_MODEL // len(POOL_WINDOWS)
POOL_HALO = max(POOL_WINDOWS)
D_FF = 2816
FF_CHUNK = 1408
RMS_EPS = 1e-6
MASK_VALUE = -1e30
LOG2_E = math.log2(math.e)

LANES = 128
SUBLANES = 8
MXU_DIM = 256
ROW_TILE = 3 * MXU_DIM
CHUNK_COLS = 2 * MXU_DIM
TILES_PER_TRIP = 4
FUSED_ROW_TILE = ROW_TILE
MIX_BLOCK = FUSED_ROW_TILE // 4
MIX_LEAD = 2
PROJ_ROW_TILES = (11 * MXU_DIM, 2 * ROW_TILE, ROW_TILE)
PROJ_T_STEPS = 3
N_ROW_BLOCKS = 4
COL_POOL, COL_K, COL_GPOOL, COL_GATTN = range(N_ROW_BLOCKS)
N_T_BLOCKS = 2
T_Q, T_V = range(N_T_BLOCKS)
HEADS_PER_STEP = LANES // HEAD_DIM
BIAS_PIECES = 3
ONES_LANE = BIAS_PIECES * N_HEADS
REF_ROW = LANES + ONES_LANE
PACKED_ROWS = 16
NORM_SLACK = 1.01
BIAS_SLACK = 0.05
REFERENCE_HEADROOM = 32.0
DENOM_FLOOR = 2.0 ** -80
VMEM_LIMIT_BYTES = 56 * 1024 * 1024

F32 = jnp.float32
BF16 = jnp.bfloat16


def _rms_norm(x, gain):
    return x * lax.rsqrt(jnp.mean(x * x, axis=-1, keepdims=True) + RMS_EPS) * gain


def _params(*semantics):
    return pltpu.CompilerParams(dimension_semantics=semantics, vmem_limit_bytes=VMEM_LIMIT_BYTES)


def _split_bf16(x):
    hi = x.astype(BF16)
    rest = x - hi.astype(F32)
    mid = rest.astype(BF16)
    lo = (rest - mid.astype(F32)).astype(BF16)
    return hi, mid, lo


def _norm_kernel(h_ref, gain_ref, out_ref):
    out_ref[...] = _rms_norm(h_ref[...], gain_ref[...]).astype(out_ref.dtype)


def _norm_call(h_res, gain):
    rows = h_res.shape[0]
    return pl.pallas_call(
        _norm_kernel,
        grid=(rows // ROW_TILE,),
        in_specs=[pl.BlockSpec((ROW_TILE, D_MODEL), lambda i: (i, 0)),
                  pl.BlockSpec((1, D_MODEL), lambda i: (0, 0))],
        out_specs=pl.BlockSpec((ROW_TILE, D_MODEL), lambda i: (i, 0)),
        out_shape=jax.ShapeDtypeStruct((rows, D_MODEL), BF16),
        compiler_params=_params("parallel"),
        name="pre_norm",
    )(h_res, gain)


def _proj_kernel(h_ref, w_ref, out_ref):
    out_ref[...] = jnp.dot(h_ref[...], w_ref[...], preferred_element_type=F32).astype(out_ref.dtype)


def _proj_call(h_norm, w_rows):
    rows = h_norm.shape[0]
    tile = next(t for t in PROJ_ROW_TILES if rows % t == 0)
    return pl.pallas_call(
        _proj_kernel,
        grid=(rows // tile, N_ROW_BLOCKS),
        in_specs=[pl.BlockSpec((tile, D_MODEL), lambda i, j: (i, 0)),
                  pl.BlockSpec((D_MODEL, D_MODEL), lambda i, j: (0, j))],
        out_specs=pl.BlockSpec((tile, D_MODEL), lambda i, j: (i, j)),
        out_shape=jax.ShapeDtypeStruct((rows, N_ROW_BLOCKS * D_MODEL), BF16),
        compiler_params=_params("parallel", "arbitrary"),
        name="in_proj",
    )(h_norm, w_rows)


def _proj_t_kernel(h_ref, wt_ref, out_ref):
    acc = lax.dot_general(wt_ref[...], h_ref[...], (((1,), (1,)), ((), ())),
                          preferred_element_type=F32)
    scale = jnp.where(pl.program_id(2) == T_Q, LOG2_E * HEAD_DIM ** -0.5, 1.0).astype(F32)
    out_ref[...] = (acc * scale).astype(out_ref.dtype)


def _proj_t_call(h_norm3, w_t):
    batch, length, _ = h_norm3.shape
    tile = length // PROJ_T_STEPS
    assert tile * PROJ_T_STEPS == length and tile % LANES == 0
    return pl.pallas_call(
        _proj_t_kernel,
        grid=(batch, PROJ_T_STEPS, N_T_BLOCKS),
        in_specs=[pl.BlockSpec((None, tile, D_MODEL), lambda b, i, j: (b, i, 0)),
                  pl.BlockSpec((D_MODEL, D_MODEL), lambda b, i, j: (j, 0))],
        out_specs=pl.BlockSpec((None, D_MODEL, tile), lambda b, i, j: (b, j, i)),
        out_shape=jax.ShapeDtypeStruct((batch, N_T_BLOCKS * D_MODEL, length), BF16),
        compiler_params=_params("parallel", "parallel", "arbitrary"),
        name="in_proj_t",
    )(h_norm3, w_t)


def _forget_kernel(h_ref, wf_ref, bias_ref, out_ref, end_ref, t_ref, carry_ref):
    @pl.when(pl.program_id(1) == 0)
    def _():
        carry_ref[...] = jnp.zeros_like(carry_ref)

    logit = jnp.dot(h_ref[...], wf_ref[...], preferred_element_type=F32) + bias_ref[...]
    log_f = jnp.minimum(logit, 0.0) - jnp.log(1.0 + jnp.exp(-jnp.abs(logit)))
    tile = log_f.shape[0]
    dst = lax.broadcasted_iota(jnp.int32, (MXU_DIM, MXU_DIM), 0)
    src = lax.broadcasted_iota(jnp.int32, (MXU_DIM, MXU_DIM), 1)
    prefix = (src <= dst).astype(BF16)
    pieces = _split_bf16(log_f)
    carry = carry_ref[...]
    blocks = []
    for r0 in range(0, tile, MXU_DIM):
        block = jnp.broadcast_to(carry, (MXU_DIM, LANES))
        for piece in pieces:
            block = block + jnp.dot(prefix, piece[r0:r0 + MXU_DIM, :], preferred_element_type=F32)
        carry = block[MXU_DIM - 1:MXU_DIM, :]
        blocks.append(block)
    total = jnp.concatenate(blocks, axis=0)
    carry_ref[...] = carry
    bias = -LOG2_E * total
    end_ref[...] = jnp.broadcast_to(bias[tile - 1:tile, :], end_ref.shape)
    hi, mid, lo = _split_bf16(bias)
    lane = lax.broadcasted_iota(jnp.int32, total.shape, 1)
    piece = lane % BIAS_PIECES
    out = jnp.where(piece == 0, hi, jnp.where(piece == 1, mid, lo))
    tail = jnp.where(lane < ONES_LANE + BIAS_PIECES, 1.0, 0.0).astype(BF16)
    out_ref[...] = jnp.where(lane < ONES_LANE, out, tail)
    head = lax.broadcasted_iota(jnp.int32, (N_HEADS, LANES), 0)
    lane_h = lax.broadcasted_iota(jnp.int32, (N_HEADS, LANES), 1)
    pick = (lane_h >= head * BIAS_PIECES) & (lane_h < (head + 1) * BIAS_PIECES)
    t_ref[...] = lax.dot_general(jnp.where(pick, 1.0, 0.0).astype(BF16), out,
                                 (((1,), (1,)), ((), ())), preferred_element_type=F32)


def _forget_call(h_norm3, wf_rep, bias_rep):
    batch, length, _ = h_norm3.shape
    n_tiles = length // ROW_TILE
    return pl.pallas_call(
        _forget_kernel,
        grid=(batch, n_tiles),
        in_specs=[pl.BlockSpec((None, ROW_TILE, D_MODEL), lambda b, i: (b, i, 0)),
                  pl.BlockSpec((D_MODEL, LANES), lambda b, i: (0, 0)),
                  pl.BlockSpec((1, LANES), lambda b, i: (0, 0))],
        out_specs=[pl.BlockSpec((None, ROW_TILE, LANES), lambda b, i: (b, i, 0)),
                   pl.BlockSpec((None, None, SUBLANES, LANES), lambda b, i: (b, i, 0, 0)),
                   pl.BlockSpec((None, N_HEADS, ROW_TILE), lambda b, i: (b, 0, i))],
        out_shape=[jax.ShapeDtypeStruct((batch, length, LANES), BF16),
                   jax.ShapeDtypeStruct((batch, n_tiles, SUBLANES, LANES), F32),
                   jax.ShapeDtypeStruct((batch, N_HEADS, length), F32)],
        scratch_shapes=[pltpu.VMEM((1, LANES), F32)],
        compiler_params=_params("parallel", "arbitrary"),
        name="forget_bias",
    )(h_norm3, wf_rep, bias_rep)


def _key_norm_kernel(k_ref, head_ref, out_ref):
    k = k_ref[...].astype(F32)
    sums = jnp.dot((k * k).astype(BF16), head_ref[...], preferred_element_type=F32)
    out_ref[...] = jnp.broadcast_to(jnp.sqrt(jnp.max(sums, axis=0, keepdims=True)), out_ref.shape)


def _key_norm_call(proj3, head_indicator):
    batch, length, _ = proj3.shape
    n_tiles = length // ROW_TILE
    return pl.pallas_call(
        _key_norm_kernel,
        grid=(batch, n_tiles),
        in_specs=[pl.BlockSpec((None, ROW_TILE, D_MODEL), lambda b, i: (b, i, COL_K)),
                  pl.BlockSpec((D_MODEL, LANES), lambda b, i: (0, 0))],
        out_specs=pl.BlockSpec((None, None, SUBLANES, LANES), lambda b, i: (b, i, 0, 0)),
        out_shape=jax.ShapeDtypeStruct((batch, n_tiles, SUBLANES, LANES), F32),
        compiler_params=_params("parallel", "parallel"),
        name="key_norms",
    )(proj3, head_indicator)


def _attn_kernel(kmax_ref, gend_ref, gt_ref, qt_ref, k_ref, fb_ref, vt_ref, out_ref,
                 acc_ref, qaug_ref, *chunk_refs):
    bi = pl.program_id(0)
    pair = pl.program_id(1)
    qi = pl.program_id(2)
    n_tiles = pl.num_programs(2)
    tile = qt_ref.shape[1]
    both = HEADS_PER_STEP * tile
    n_chunks = both // CHUNK_COLS
    s_refs, p_refs = chunk_refs[:n_chunks], chunk_refs[n_chunks:]

    feat = lax.broadcasted_iota(jnp.int32, (LANES, tile), 0)
    q_norms = []
    for hh in range(HEADS_PER_STEP):
        head = pair * HEADS_PER_STEP + hh
        in_head = (feat >= hh * HEAD_DIM) & (feat < (hh + 1) * HEAD_DIM)
        q_top = jnp.where(in_head, qt_ref[...], jnp.zeros_like(qt_ref[...]))
        is_piece = (feat >= head * BIAS_PIECES) & (feat < (head + 1) * BIAS_PIECES)
        qaug_ref[:, hh * tile:(hh + 1) * tile] = jnp.concatenate(
            [q_top, is_piece.astype(BF16)], axis=0)
        q_head = qt_ref[hh * HEAD_DIM:(hh + 1) * HEAD_DIM, :].astype(F32)
        q_norms.append(jnp.sqrt(jnp.sum(q_head * q_head, axis=0, keepdims=True)))
    q_norm = jnp.concatenate(q_norms, axis=1)
    second_head = lax.broadcasted_iota(jnp.int32, (1, both), 1) >= tile
    g_diagonal = jnp.concatenate([gt_ref[hh:hh + 1, :] for hh in range(HEADS_PER_STEP)], axis=1)

    def head_segments(c):
        segments = []
        start = c * CHUNK_COLS
        while start < (c + 1) * CHUNK_COLS:
            hh = start // tile
            stop = min((c + 1) * CHUNK_COLS, (hh + 1) * tile)
            segments.append((hh, slice(start - hh * tile, stop - hh * tile),
                             slice(start - c * CHUNK_COLS, stop - c * CHUNK_COLS)))
            start = stop
        return segments

    def scores(c, k_start, on_diagonal):
        keys = jnp.concatenate([k_ref[pl.ds(k_start, tile), :],
                                fb_ref[pl.ds(k_start, tile), :]], axis=1)
        s_t = jnp.dot(keys, qaug_ref[:, c * CHUNK_COLS:(c + 1) * CHUNK_COLS],
                      preferred_element_type=F32)
        if on_diagonal:
            key_row = lax.broadcasted_iota(jnp.int32, s_t.shape, 0)
            col = lax.broadcasted_iota(jnp.int32, s_t.shape, 1) + c * CHUNK_COLS
            query_col = jnp.where(col >= tile, col - tile, col)
            s_t = jnp.where(key_row <= query_col, s_t, MASK_VALUE)
        return s_t

    def value_stage(c, k_start, alpha):
        for hh, head_cols, chunk_cols in head_segments(c):
            v_aug = jnp.concatenate(
                [vt_ref[hh * HEAD_DIM:(hh + 1) * HEAD_DIM, pl.ds(k_start, tile)],
                 jnp.ones((PACKED_ROWS, tile), BF16)], axis=0)
            acc_ref[hh, :, head_cols] = (
                alpha[:, c * CHUNK_COLS:(c + 1) * CHUNK_COLS][:, chunk_cols]
                * acc_ref[hh, :, head_cols]
                + jnp.dot(v_aug, p_refs[c][:, chunk_cols], preferred_element_type=F32))

    def set_reference(r):
        hi, mid, lo = (piece.astype(F32) for piece in _split_bf16(-r))
        row = lax.broadcasted_iota(jnp.int32, (PACKED_ROWS, both), 0)
        pieces = jnp.where(row == 0, hi, jnp.where(row == 1, mid, jnp.where(row == 2, lo, 0.0)))
        qaug_ref[REF_ROW:REF_ROW + PACKED_ROWS, :] = pieces.astype(BF16)

    def bound_tile(kj, r_prev, on_diagonal):
        k_start = pl.multiple_of(kj * tile, LANES)
        base = (bi * n_tiles + kj) * N_HEADS + pair * HEADS_PER_STEP
        k_max = jnp.where(second_head, kmax_ref[base + 1], kmax_ref[base])
        if on_diagonal:
            g_max = g_diagonal
        else:
            g_max = jnp.where(second_head, gend_ref[base + 1], gend_ref[base])
        bound = q_norm * k_max * NORM_SLACK + g_max + BIAS_SLACK
        r_new = jnp.maximum(r_prev, bound - REFERENCE_HEADROOM)
        alpha = jnp.exp2(r_prev - r_new)
        set_reference(r_new)

        if on_diagonal:
            diagonal_groups(k_start, alpha)
            return r_new

        def probabilities(c):
            p_refs[c][...] = jnp.exp2(scores(c, k_start, False)).astype(BF16)

        probabilities(0)
        probabilities(1)
        value_stage(0, k_start, alpha)
        probabilities(2)
        value_stage(1, k_start, alpha)
        value_stage(2, k_start, alpha)
        return r_new

    def diagonal_groups(k_start, alpha):
        n_groups = tile // MXU_DIM
        assert n_groups == n_chunks

        def group_cols(hh, g):
            return slice(hh * tile + g * MXU_DIM, hh * tile + (g + 1) * MXU_DIM)

        def probabilities(g):
            n_keys = (g + 1) * MXU_DIM
            keys = jnp.concatenate([k_ref[pl.ds(k_start, n_keys), :],
                                    fb_ref[pl.ds(k_start, n_keys), :]], axis=1)
            queries = jnp.concatenate(
                [qaug_ref[:, group_cols(hh, g)] for hh in range(HEADS_PER_STEP)], axis=1)
            s_t = jnp.dot(keys, queries, preferred_element_type=F32)
            key_row = lax.broadcasted_iota(jnp.int32, s_t.shape, 0)
            col = lax.broadcasted_iota(jnp.int32, s_t.shape, 1)
            query_col = jnp.where(col >= MXU_DIM, col - MXU_DIM, col) + g * MXU_DIM
            s_t = jnp.where(key_row <= query_col, s_t, MASK_VALUE)
            p_refs[g][0:n_keys, :] = jnp.exp2(s_t).astype(BF16)

        def values(g):
            n_keys = (g + 1) * MXU_DIM
            for hh in range(HEADS_PER_STEP):
                head_cols = slice(g * MXU_DIM, (g + 1) * MXU_DIM)
                v_aug = jnp.concatenate(
                    [vt_ref[hh * HEAD_DIM:(hh + 1) * HEAD_DIM, pl.ds(k_start, n_keys)],
                     jnp.ones((PACKED_ROWS, n_keys), BF16)], axis=0)
                acc_ref[hh, :, head_cols] = (
                    alpha[:, group_cols(hh, g)] * acc_ref[hh, :, head_cols]
                    + jnp.dot(v_aug, p_refs[g][0:n_keys, hh * MXU_DIM:(hh + 1) * MXU_DIM],
                              preferred_element_type=F32))

        probabilities(2)
        probabilities(1)
        values(2)
        probabilities(0)
        values(1)
        values(0)

    assert n_chunks == 3
    acc_ref[...] = jnp.zeros_like(acc_ref)
    r_run = jnp.full((1, both), MASK_VALUE, F32)
    def tiles(first, count, r):
        for t in range(count):
            r = bound_tile(first + t, r, False)
        return r

    r_run = lax.fori_loop(0, qi // TILES_PER_TRIP,
                          lambda kk, r: tiles(kk * TILES_PER_TRIP, TILES_PER_TRIP, r), r_run)
    done = qi // TILES_PER_TRIP * TILES_PER_TRIP
    count = TILES_PER_TRIP // 2
    while count:
        has = ((qi - done) & count) != 0
        r_run = lax.cond(has, lambda r, at=done, n=count: tiles(at, n, r), lambda r: r, r_run)
        done = done + jnp.where(has, count, 0)
        count //= 2
    bound_tile(qi, r_run, True)

    def write_output():
        heads_out = [acc_ref[hh, :HEAD_DIM, :] / acc_ref[hh, HEAD_DIM:HEAD_DIM + 1, :]
                     for hh in range(HEADS_PER_STEP)]
        out_ref[...] = jnp.concatenate(heads_out, axis=0).T.astype(out_ref.dtype)

    write_output()
    smallest_denominator = jnp.min(jnp.minimum(acc_ref[0, HEAD_DIM:HEAD_DIM + 1, :],
                                               acc_ref[1, HEAD_DIM:HEAD_DIM + 1, :]))

    @pl.when(jnp.logical_not(smallest_denominator > DENOM_FLOOR))
    def _():
        def max_tile(kj, m_prev, on_diagonal):
            k_start = pl.multiple_of(kj * tile, LANES)
            m_tile = []
            for c in range(n_chunks):
                s_t = scores(c, k_start, on_diagonal)
                s_refs[c][...] = s_t
                m_tile.append(jnp.max(s_t, axis=0, keepdims=True))
            m_new = jnp.maximum(m_prev, jnp.concatenate(m_tile, axis=1))
            alpha = jnp.exp2(m_prev - m_new)
            for c in range(n_chunks):
                cols = slice(c * CHUNK_COLS, (c + 1) * CHUNK_COLS)
                p_refs[c][...] = jnp.exp2(s_refs[c][...] - m_new[:, cols]).astype(BF16)
                value_stage(c, k_start, alpha)
            return m_new

        acc_ref[...] = jnp.zeros_like(acc_ref)
        set_reference(jnp.zeros((1, both), F32))
        m_run = jnp.full((1, both), MASK_VALUE, F32)
        m_run = lax.fori_loop(0, qi, lambda kj, m: max_tile(kj, m, False), m_run)
        max_tile(qi, m_run, True)
        write_output()


def _attn_call(proj3, proj_t, key_bias, bias_t, key_norm_max, bias_max):
    batch, length, _ = proj3.shape
    pairs = N_HEADS // HEADS_PER_STEP
    bias_t = bias_t.reshape(batch, pairs, HEADS_PER_STEP, length)
    blocks_per_section = D_MODEL // LANES
    n_chunks = HEADS_PER_STEP * ROW_TILE // CHUNK_COLS
    return pl.pallas_call(
        _attn_kernel,
        grid=(batch, pairs, length // ROW_TILE),
        in_specs=[
            pl.BlockSpec(memory_space=pltpu.SMEM),
            pl.BlockSpec(memory_space=pltpu.SMEM),
            pl.BlockSpec((None, None, HEADS_PER_STEP, ROW_TILE), lambda b, p, i: (b, p, 0, i)),
            pl.BlockSpec((None, LANES, ROW_TILE), lambda b, p, i: (b, T_Q * blocks_per_section + p, i)),
            pl.BlockSpec((None, length, LANES), lambda b, p, i: (b, 0, COL_K * blocks_per_section + p)),
            pl.BlockSpec((None, length, LANES), lambda b, p, i: (b, 0, 0)),
            pl.BlockSpec((None, LANES, length), lambda b, p, i: (b, T_V * blocks_per_section + p, 0)),
        ],
        out_specs=pl.BlockSpec((None, ROW_TILE, LANES), lambda b, p, i: (b, i, p)),
        out_shape=jax.ShapeDtypeStruct((batch, length, D_MODEL), BF16),
        scratch_shapes=[
            pltpu.VMEM((HEADS_PER_STEP, HEAD_DIM + PACKED_ROWS, ROW_TILE), F32),
            pltpu.VMEM((2 * LANES, HEADS_PER_STEP * ROW_TILE), BF16),
        ] + [pltpu.VMEM((ROW_TILE, CHUNK_COLS), dtype) for dtype in (F32, BF16)
             for _ in range(n_chunks)],
        compiler_params=_params("parallel", "parallel", "arbitrary"),
        name="forgetting_attention",
    )(key_norm_max, bias_max, bias_t, proj_t, proj3, key_bias, proj_t)


def _pool_band():
    row = jnp.arange(MIX_BLOCK)[:, None]
    col = jnp.arange(MIX_BLOCK)[None, :]
    bands = [((col <= row) & (col > row - w)).astype(F32) / w - (col == row).astype(F32)
             for w in POOL_WINDOWS]
    return jnp.stack(bands).astype(BF16)


def _mix_ffn_kernel(u_ref, halo_ref, gp_ref, ga_ref, ya_ref, h_ref, band_ref, wp_ref, ps_ref,
                    wo_ref, gain_mix_ref, gain_pre_ref, wg_ref, wu_ref, wd_ref, gain_ffn_ref,
                    gain_next_ref, hres_out, hnorm_out):
    li = pl.program_id(1)

    def mix_block(r0):
        rows = slice(r0, r0 + MIX_BLOCK)
        head = u_ref[r0:r0 + POOL_HALO, :].astype(F32)
        if r0 == 0:
            halo = jnp.where(li > 0, halo_ref[...].astype(F32), 0.0)
        else:
            halo = u_ref[r0 - POOL_HALO:r0, :].astype(F32)
        ext = jnp.concatenate([halo, head], axis=0)
        pos1 = (li * u_ref.shape[0] + r0 + 1
                + lax.broadcasted_iota(jnp.int32, (POOL_HALO, 1), 0))
        pooled = []
        for g, window in enumerate(POOL_WINDOWS):
            lanes = slice(g * POOL_GROUP_DIM, (g + 1) * POOL_GROUP_DIM)
            run = ext[:, lanes]
            width = 1
            while width < window:
                run = run[width:] + run[:-width]
                width *= 2
            first = POOL_HALO + 1 - window
            count = jnp.minimum(pos1, window).astype(F32)
            diff_head = run[first:first + POOL_HALO] / count - head[:, lanes]
            diff = jnp.dot(band_ref[g], u_ref[rows, lanes], preferred_element_type=F32)
            diff = jnp.concatenate([diff_head, diff[POOL_HALO:]], axis=0)
            pooled.append(jnp.dot(diff.astype(BF16), wp_ref[g], preferred_element_type=F32))
        y_pool = jnp.concatenate(pooled, axis=1) * ps_ref[...]
        merged = (jax.nn.sigmoid(gp_ref[rows, :].astype(F32)) * y_pool
                  + jax.nn.sigmoid(ga_ref[rows, :].astype(F32)) * ya_ref[rows, :].astype(F32))
        mix = jnp.dot(merged.astype(BF16), wo_ref[...], preferred_element_type=F32)
        return h_ref[rows, :] + _rms_norm(mix, gain_mix_ref[...])

    def ffn_block(r0, h_mid):
        rows = slice(r0, r0 + MIX_BLOCK)
        hn = _rms_norm(h_mid, gain_pre_ref[...]).astype(BF16)
        ff_out = jnp.zeros(h_mid.shape, F32)
        for c in range(D_FF // FF_CHUNK):
            cols = slice(c * FF_CHUNK, (c + 1) * FF_CHUNK)
            gate = jnp.dot(hn, wg_ref[:, cols], preferred_element_type=F32)
            up = jnp.dot(hn, wu_ref[:, cols], preferred_element_type=F32)
            ff = gate * jax.nn.sigmoid(gate) * up
            ff_out = ff_out + jnp.dot(ff.astype(BF16), wd_ref[cols, :], preferred_element_type=F32)
        h_new = h_mid + _rms_norm(ff_out, gain_ffn_ref[...])
        hres_out[rows, :] = h_new
        hnorm_out[rows, :] = _rms_norm(h_new, gain_next_ref[...]).astype(hnorm_out.dtype)

    n_blocks = u_ref.shape[0] // MIX_BLOCK
    pending = {}
    for step in range(n_blocks + MIX_LEAD):
        if step < n_blocks:
            pending[step] = mix_block(step * MIX_BLOCK)
        if step >= MIX_LEAD:
            ffn_block((step - MIX_LEAD) * MIX_BLOCK, pending.pop(step - MIX_LEAD))


def _mix_ffn_call(proj3, y_attn, h_res3, band, w_pool, pool_scale, w_out, gain_mix, gain_pre,
                  w_gate, w_up, w_down, gain_ffn, gain_next):
    batch, length, _ = proj3.shape
    halo_blocks = FUSED_ROW_TILE // POOL_HALO
    row_spec = lambda col: pl.BlockSpec((None, FUSED_ROW_TILE, D_MODEL), lambda b, i: (b, i, col))
    const = lambda shape: pl.BlockSpec(shape, lambda b, i: (0,) * len(shape),
                                       pipeline_mode=pl.Buffered(1))
    return pl.pallas_call(
        _mix_ffn_kernel,
        grid=(batch, length // FUSED_ROW_TILE),
        in_specs=[
            row_spec(COL_POOL),
            pl.BlockSpec((None, POOL_HALO, D_MODEL),
                         lambda b, i: (b, jnp.maximum(i * halo_blocks - 1, 0), COL_POOL)),
            row_spec(COL_GPOOL),
            row_spec(COL_GATTN),
            row_spec(0),
            row_spec(0),
            const((len(POOL_WINDOWS), MIX_BLOCK, MIX_BLOCK)),
            const((len(POOL_WINDOWS), POOL_GROUP_DIM, POOL_GROUP_DIM)),
            const((1, D_MODEL)),
            const((D_MODEL, D_MODEL)),
            const((1, D_MODEL)),
            const((1, D_MODEL)),
            const((D_MODEL, D_FF)), const((D_MODEL, D_FF)), const((D_FF, D_MODEL)),
            const((1, D_MODEL)),
            const((1, D_MODEL)),
        ],
        out_specs=[row_spec(0), row_spec(0)],
        out_shape=[jax.ShapeDtypeStruct((batch, length, D_MODEL), F32),
                   jax.ShapeDtypeStruct((batch, length, D_MODEL), BF16)],
        compiler_params=_params("parallel", "parallel"),
        name="mix_ffn",
    )(proj3, proj3, proj3, proj3, y_attn, h_res3, band, w_pool, pool_scale, w_out, gain_mix,
      gain_pre, w_gate, w_up, w_down, gain_ffn, gain_next)


def kernel(x, meta_tokens, norm_mix_pre, norm_mix_post, norm_ffn_pre, norm_ffn_post,
           w_in, b_forget, w_pool, pool_scale, w_out, w_ffn_gate, w_ffn_up, w_ffn_down):
    batch, seq, _ = x.shape
    depth = w_in.shape[0]
    tokens = META_TOKENS + seq
    length = -(-tokens // ROW_TILE) * ROW_TILE
    meta = jnp.broadcast_to(meta_tokens[None].astype(x.dtype), (batch, META_TOKENS, D_MODEL))
    tail = jnp.zeros((batch, length - tokens, D_MODEL), x.dtype)
    h_res = jnp.concatenate([meta, x, tail], axis=1).reshape(batch * length, D_MODEL)
    gain = lambda g: g.reshape(1, D_MODEL).astype(F32)
    sec = lambda w, n: w[:, n * D_MODEL:(n + 1) * D_MODEL]
    gates_at = 4 * D_MODEL + N_HEADS
    lane_head = jnp.minimum(jnp.arange(LANES) // BIAS_PIECES, N_HEADS - 1)
    head_indicator = (jnp.arange(D_MODEL)[:, None] // HEAD_DIM == jnp.arange(LANES)[None, :]).astype(BF16)

    pool_band = _pool_band()

    h_norm = _norm_call(h_res, gain(norm_mix_pre[0]))
    for l in range(depth):
        w = w_in[l]
        w_rows = jnp.concatenate([sec(w, 0), sec(w, 2), w[:, gates_at:]], axis=1).astype(BF16)
        w_t = jnp.concatenate([sec(w, 1), sec(w, 3)], axis=1).T.astype(BF16)
        w_forget = w[:, 4 * D_MODEL:gates_at]
        wf_rep = w_forget[:, lane_head].astype(BF16)
        bias_rep = b_forget[l][lane_head].reshape(1, LANES).astype(F32)

        h_norm3 = h_norm.reshape(batch, length, D_MODEL)
        proj3 = _proj_call(h_norm, w_rows).reshape(batch, length, N_ROW_BLOCKS * D_MODEL)
        proj_t = _proj_t_call(h_norm3, w_t)
        key_bias, bias_end, bias_t = _forget_call(h_norm3, wf_rep, bias_rep)
        key_norms = _key_norm_call(proj3, head_indicator)
        key_norm_max = key_norms[:, :, 0, :N_HEADS].reshape(-1)
        bias_max = bias_end[:, :, 0, 0:ONES_LANE:BIAS_PIECES].reshape(-1)
        y_attn = _attn_call(proj3, proj_t, key_bias, bias_t, key_norm_max, bias_max)
        next_gain = norm_mix_pre[(l + 1) % depth]
        h_res3, h_norm3 = _mix_ffn_call(
            proj3, y_attn, h_res.reshape(batch, length, D_MODEL), pool_band,
            w_pool[l].astype(BF16), gain(pool_scale[l]), w_out[l].astype(BF16),
            gain(norm_mix_post[l]), gain(norm_ffn_pre[l]),
            w_ffn_gate[l].astype(BF16), w_ffn_up[l].astype(BF16), w_ffn_down[l].astype(BF16),
            gain(norm_ffn_post[l]), gain(next_gain))
        h_res = h_res3.reshape(batch * length, D_MODEL)
        h_norm = h_norm3.reshape(batch * length, D_MODEL)
    return h_res.reshape(batch, length, D_MODEL)[:, META_TOKENS:tokens]
```

```python
import math

import jax
import jax.numpy as jnp
from jax import lax
from jax.experimental import pallas as pl
from jax.experimental.pallas import tpu as pltpu

D_MODEL = 1024
N_HEADS = 16
HEAD_DIM = 64
META_TOKENS = 16
POOL_WINDOWS = (2, 4, 8, 16)
POOL_GROUP_DIM = D_MODEL // len(POOL_WINDOWS)
POOL_HALO = max(POOL_WINDOWS)
D_FF = 2816
FF_CHUNK = 1408
RMS_EPS = 1e-6
MASK_VALUE = -1e30
LOG2_E = math.log2(math.e)

LANES = 128
SUBLANES = 8
MXU_DIM = 256
ROW_TILE = 3 * MXU_DIM
CHUNK_COLS = 2 * MXU_DIM
TILES_PER_TRIP = 4
FUSED_ROW_TILE = ROW_TILE
MIX_BLOCK = FUSED_ROW_TILE // 4
MIX_LEAD = 2
PROJ_ROW_TILES = (11 * MXU_DIM, 2 * ROW_TILE, ROW_TILE)
PROJ_T_STEPS = 3
N_ROW_BLOCKS = 4
COL_POOL, COL_K, COL_GPOOL, COL_GATTN = range(N_ROW_BLOCKS)
N_T_BLOCKS = 2
T_Q, T_V = range(N_T_BLOCKS)
HEADS_PER_STEP = LANES // HEAD_DIM
BIAS_PIECES = 3
ONES_LANE = BIAS_PIECES * N_HEADS
REF_ROW = LANES + ONES_LANE
PACKED_ROWS = 16
NORM_SLACK = 1.01
BIAS_SLACK = 0.05
REFERENCE_HEADROOM = 32.0
DENOM_FLOOR = 2.0 ** -80
VMEM_LIMIT_BYTES = 56 * 1024 * 1024

F32 = jnp.float32
BF16 = jnp.bfloat16


def _rms_norm(x, gain):
    return x * lax.rsqrt(jnp.mean(x * x, axis=-1, keepdims=True) + RMS_EPS) * gain


def _params(*semantics):
    return pltpu.CompilerParams(dimension_semantics=semantics, vmem_limit_bytes=VMEM_LIMIT_BYTES)


def _split_bf16(x):
    hi = x.astype(BF16)
    rest = x - hi.astype(F32)
    mid = rest.astype(BF16)
    lo = (rest - mid.astype(F32)).astype(BF16)
    return hi, mid, lo


def _norm_kernel(h_ref, gain_ref, out_ref):
    out_ref[...] = _rms_norm(h_ref[...], gain_ref[...]).astype(out_ref.dtype)


def _norm_call(h_res, gain):
    rows = h_res.shape[0]
    return pl.pallas_call(
        _norm_kernel,
        grid=(rows // ROW_TILE,),
        in_specs=[pl.BlockSpec((ROW_TILE, D_MODEL), lambda i: (i, 0)),
                  pl.BlockSpec((1, D_MODEL), lambda i: (0, 0))],
        out_specs=pl.BlockSpec((ROW_TILE, D_MODEL), lambda i: (i, 0)),
        out_shape=jax.ShapeDtypeStruct((rows, D_MODEL), BF16),
        compiler_params=_params("parallel"),
        name="pre_norm",
    )(h_res, gain)


def _proj_kernel(h_ref, w_ref, out_ref):
    out_ref[...] = jnp.dot(h_ref[...], w_ref[...], preferred_element_type=F32).astype(out_ref.dtype)


def _proj_call(h_norm, w_rows):
    rows = h_norm.shape[0]
    tile = next(t for t in PROJ_ROW_TILES if rows % t == 0)
    return pl.pallas_call(
        _proj_kernel,
        grid=(rows // tile, N_ROW_BLOCKS),
        in_specs=[pl.BlockSpec((tile, D_MODEL), lambda i, j: (i, 0)),
                  pl.BlockSpec((D_MODEL, D_MODEL), lambda i, j: (0, j))],
        out_specs=pl.BlockSpec((tile, D_MODEL), lambda i, j: (i, j)),
        out_shape=jax.ShapeDtypeStruct((rows, N_ROW_BLOCKS * D_MODEL), BF16),
        compiler_params=_params("parallel", "arbitrary"),
        name="in_proj",
    )(h_norm, w_rows)


def _proj_t_kernel(h_ref, wt_ref, out_ref):
    acc = lax.dot_general(wt_ref[...], h_ref[...], (((1,), (1,)), ((), ())),
                          preferred_element_type=F32)
    scale = jnp.where(pl.program_id(2) == T_Q, LOG2_E * HEAD_DIM ** -0.5, 1.0).astype(F32)
    out_ref[...] = (acc * scale).astype(out_ref.dtype)


def _proj_t_call(h_norm3, w_t):
    batch, length, _ = h_norm3.shape
    tile = length // PROJ_T_STEPS
    assert tile * PROJ_T_STEPS == length and tile % LANES == 0
    return pl.pallas_call(
        _proj_t_kernel,
        grid=(batch, PROJ_T_STEPS, N_T_BLOCKS),
        in_specs=[pl.BlockSpec((None, tile, D_MODEL), lambda b, i, j: (b, i, 0)),
                  pl.BlockSpec((D_MODEL, D_MODEL), lambda b, i, j: (j, 0))],
        out_specs=pl.BlockSpec((None, D_MODEL, tile), lambda b, i, j: (b, j, i)),
        out_shape=jax.ShapeDtypeStruct((batch, N_T_BLOCKS * D_MODEL, length), BF16),
        compiler_params=_params("parallel", "parallel", "arbitrary"),
        name="in_proj_t",
    )(h_norm3, w_t)


def _forget_kernel(h_ref, wf_ref, bias_ref, out_ref, end_ref, t_ref, carry_ref):
    @pl.when(pl.program_id(1) == 0)
    def _():
        carry_ref[...] = jnp.zeros_like(carry_ref)

    logit = jnp.dot(h_ref[...], wf_ref[...], preferred_element_type=F32) + bias_ref[...]
    log_f = jnp.minimum(logit, 0.0) - jnp.log(1.0 + jnp.exp(-jnp.abs(logit)))
    tile = log_f.shape[0]
    dst = lax.broadcasted_iota(jnp.int32, (MXU_DIM, MXU_DIM), 0)
    src = lax.broadcasted_iota(jnp.int32, (MXU_DIM, MXU_DIM), 1)
    prefix = (src <= dst).astype(BF16)
    pieces = _split_bf16(log_f)
    carry = carry_ref[...]
    blocks = []
    for r0 in range(0, tile, MXU_DIM):
        block = jnp.broadcast_to(carry, (MXU_DIM, LANES))
        for piece in pieces:
            block = block + jnp.dot(prefix, piece[r0:r0 + MXU_DIM, :], preferred_element_type=F32)
        carry = block[MXU_DIM - 1:MXU_DIM, :]
        blocks.append(block)
    total = jnp.concatenate(blocks, axis=0)
    carry_ref[...] = carry
    bias = -LOG2_E * total
    end_ref[...] = jnp.broadcast_to(bias[tile - 1:tile, :], end_ref.shape)
    hi, mid, lo = _split_bf16(bias)
    lane = lax.broadcasted_iota(jnp.int32, total.shape, 1)
    piece = lane % BIAS_PIECES
    out = jnp.where(piece == 0, hi, jnp.where(piece == 1, mid, lo))
    tail = jnp.where(lane < ONES_LANE + BIAS_PIECES, 1.0, 0.0).astype(BF16)
    out_ref[...] = jnp.where(lane < ONES_LANE, out, tail)
    head = lax.broadcasted_iota(jnp.int32, (N_HEADS, LANES), 0)
    lane_h = lax.broadcasted_iota(jnp.int32, (N_HEADS, LANES), 1)
    pick = (lane_h >= head * BIAS_PIECES) & (lane_h < (head + 1) * BIAS_PIECES)
    t_ref[...] = lax.dot_general(jnp.where(pick, 1.0, 0.0).astype(BF16), out,
                                 (((1,), (1,)), ((), ())), preferred_element_type=F32)


def _forget_call(h_norm3, wf_rep, bias_rep):
    batch, length, _ = h_norm3.shape
    n_tiles = length // ROW_TILE
    return pl.pallas_call(
        _forget_kernel,
        grid=(batch, n_tiles),
        in_specs=[pl.BlockSpec((None, ROW_TILE, D_MODEL), lambda b, i: (b, i, 0)),
                  pl.BlockSpec((D_MODEL, LANES), lambda b, i: (0, 0)),
                  pl.BlockSpec((1, LANES), lambda b, i: (0, 0))],
        out_specs=[pl.BlockSpec((None, ROW_TILE, LANES), lambda b, i: (b, i, 0)),
                   pl.BlockSpec((None, None, SUBLANES, LANES), lambda b, i: (b, i, 0, 0)),
                   pl.BlockSpec((None, N_HEADS, ROW_TILE), lambda b, i: (b, 0, i))],
        out_shape=[jax.ShapeDtypeStruct((batch, length, LANES), BF16),
                   jax.ShapeDtypeStruct((batch, n_tiles, SUBLANES, LANES), F32),
                   jax.ShapeDtypeStruct((batch, N_HEADS, length), F32)],
        scratch_shapes=[pltpu.VMEM((1, LANES), F32)],
        compiler_params=_params("parallel", "arbitrary"),
        name="forget_bias",
    )(h_norm3, wf_rep, bias_rep)


def _key_norm_kernel(k_ref, head_ref, out_ref):
    k = k_ref[...].astype(F32)
    sums = jnp.dot((k * k).astype(BF16), head_ref[...], preferred_element_type=F32)
    out_ref[...] = jnp.broadcast_to(jnp.sqrt(jnp.max(sums, axis=0, keepdims=True)), out_ref.shape)


def _key_norm_call(proj3, head_indicator):
    batch, length, _ = proj3.shape
    n_tiles = length // ROW_TILE
    return pl.pallas_call(
        _key_norm_kernel,
        grid=(batch, n_tiles),
        in_specs=[pl.BlockSpec((None, ROW_TILE, D_MODEL), lambda b, i: (b, i, COL_K)),
                  pl.BlockSpec((D_MODEL, LANES), lambda b, i: (0, 0))],
        out_specs=pl.BlockSpec((None, None, SUBLANES, LANES), lambda b, i: (b, i, 0, 0)),
        out_shape=jax.ShapeDtypeStruct((batch, n_tiles, SUBLANES, LANES), F32),
        compiler_params=_params("parallel", "parallel"),
        name="key_norms",
    )(proj3, head_indicator)


def _attn_kernel(kmax_ref, gend_ref, gt_ref, qt_ref, k_ref, fb_ref, vt_ref, out_ref, *scratch):
    n_tiles = k_ref.shape[0] // ROW_TILE

    def one_tile(qi, carry):
        _attn_query_tile(qi, n_tiles, kmax_ref, gend_ref, gt_ref, qt_ref, k_ref, fb_ref, vt_ref,
                         out_ref, *scratch)
        return carry

    lax.fori_loop(0, n_tiles, one_tile, 0)


def _attn_query_tile(qi, n_tiles, kmax_ref, gend_ref, gt_ref, qt_ref, k_ref, fb_ref, vt_ref,
                     out_ref, acc_ref, qaug_ref, *chunk_refs):
    bi = pl.program_id(0)
    pair = pl.program_id(1)
    tile = ROW_TILE
    q_at = pl.ds(pl.multiple_of(qi * tile, LANES), tile)
    q_tile = qt_ref[:, q_at]
    both = HEADS_PER_STEP * tile
    n_chunks = both // CHUNK_COLS
    s_refs, p_refs = chunk_refs[:n_chunks], chunk_refs[n_chunks:]

    feat = lax.broadcasted_iota(jnp.int32, (LANES, tile), 0)
    q_norms = []
    for hh in range(HEADS_PER_STEP):
        head = pair * HEADS_PER_STEP + hh
        in_head = (feat >= hh * HEAD_DIM) & (feat < (hh + 1) * HEAD_DIM)
        q_top = jnp.where(in_head, q_tile, jnp.zeros_like(q_tile))
        is_piece = (feat >= head * BIAS_PIECES) & (feat < (head + 1) * BIAS_PIECES)
        qaug_ref[:, hh * tile:(hh + 1) * tile] = jnp.concatenate(
            [q_top, is_piece.astype(BF16)], axis=0)
        q_head = q_tile[hh * HEAD_DIM:(hh + 1) * HEAD_DIM, :].astype(F32)
        q_norms.append(jnp.sqrt(jnp.sum(q_head * q_head, axis=0, keepdims=True)))
    q_norm = jnp.concatenate(q_norms, axis=1)
    second_head = lax.broadcasted_iota(jnp.int32, (1, both), 1) >= tile
    g_diagonal = jnp.concatenate([gt_ref[hh:hh + 1, q_at] for hh in range(HEADS_PER_STEP)], axis=1)

    def head_segments(c):
        segments = []
        start = c * CHUNK_COLS
        while start < (c + 1) * CHUNK_COLS:
            hh = start // tile
            stop = min((c + 1) * CHUNK_COLS, (hh + 1) * tile)
            segments.append((hh, slice(start - hh * tile, stop - hh * tile),
                             slice(start - c * CHUNK_COLS, stop - c * CHUNK_COLS)))
            start = stop
        return segments

    def scores(c, k_start, on_diagonal):
        keys = jnp.concatenate([k_ref[pl.ds(k_start, tile), :],
                                fb_ref[pl.ds(k_start, tile), :]], axis=1)
        s_t = jnp.dot(keys, qaug_ref[:, c * CHUNK_COLS:(c + 1) * CHUNK_COLS],
                      preferred_element_type=F32)
        if on_diagonal:
            key_row = lax.broadcasted_iota(jnp.int32, s_t.shape, 0)
            col = lax.broadcasted_iota(jnp.int32, s_t.shape, 1) + c * CHUNK_COLS
            query_col = jnp.where(col >= tile, col - tile, col)
            s_t = jnp.where(key_row <= query_col, s_t, MASK_VALUE)
        return s_t

    def value_stage(c, k_start, alpha):
        for hh, head_cols, chunk_cols in head_segments(c):
            v_aug = jnp.concatenate(
                [vt_ref[hh * HEAD_DIM:(hh + 1) * HEAD_DIM, pl.ds(k_start, tile)],
                 jnp.ones((PACKED_ROWS, tile), BF16)], axis=0)
            acc_ref[hh, :, head_cols] = (
                alpha[:, c * CHUNK_COLS:(c + 1) * CHUNK_COLS][:, chunk_cols]
                * acc_ref[hh, :, head_cols]
                + jnp.dot(v_aug, p_refs[c][:, chunk_cols], preferred_element_type=F32))

    def set_reference(r):
        hi, mid, lo = (piece.astype(F32) for piece in _split_bf16(-r))
        row = lax.broadcasted_iota(jnp.int32, (PACKED_ROWS, both), 0)
        pieces = jnp.where(row == 0, hi, jnp.where(row == 1, mid, jnp.where(row == 2, lo, 0.0)))
        qaug_ref[REF_ROW:REF_ROW + PACKED_ROWS, :] = pieces.astype(BF16)

    def bound_tile(kj, r_prev, on_diagonal):
        k_start = pl.multiple_of(kj * tile, LANES)
        base = (bi * n_tiles + kj) * N_HEADS + pair * HEADS_PER_STEP
        k_max = jnp.where(second_head, kmax_ref[base + 1], kmax_ref[base])
        if on_diagonal:
            g_max = g_diagonal
        else:
            g_max = jnp.where(second_head, gend_ref[base + 1], gend_ref[base])
        bound = q_norm * k_max * NORM_SLACK + g_max + BIAS_SLACK
        r_new = jnp.maximum(r_prev, bound - REFERENCE_HEADROOM)
        alpha = jnp.exp2(r_prev - r_new)
        set_reference(r_new)

        if on_diagonal:
            diagonal_groups(k_start, alpha)
            return r_new

        def probabilities(c):
            p_refs[c][...] = jnp.exp2(scores(c, k_start, False)).astype(BF16)

        probabilities(0)
        probabilities(1)
        value_stage(0, k_start, alpha)
        probabilities(2)
        value_stage(1, k_start, alpha)
        value_stage(2, k_start, alpha)
        return r_new

    def diagonal_groups(k_start, alpha):
        n_groups = tile // MXU_DIM
        assert n_groups == n_chunks

        def group_cols(hh, g):
            return slice(hh * tile + g * MXU_DIM, hh * tile + (g + 1) * MXU_DIM)

        def probabilities(g):
            n_keys = (g + 1) * MXU_DIM
            keys = jnp.concatenate([k_ref[pl.ds(k_start, n_keys), :],
                                    fb_ref[pl.ds(k_start, n_keys), :]], axis=1)
            queries = jnp.concatenate(
                [qaug_ref[:, group_cols(hh, g)] for hh in range(HEADS_PER_STEP)], axis=1)
            s_t = jnp.dot(keys, queries, preferred_element_type=F32)
            key_row = lax.broadcasted_iota(jnp.int32, s_t.shape, 0)
            col = lax.broadcasted_iota(jnp.int32, s_t.shape, 1)
            query_col = jnp.where(col >= MXU_DIM, col - MXU_DIM, col) + g * MXU_DIM
            s_t = jnp.where(key_row <= query_col, s_t, MASK_VALUE)
            p_refs[g][0:n_keys, :] = jnp.exp2(s_t).astype(BF16)

        def values(g):
            n_keys = (g + 1) * MXU_DIM
            for hh in range(HEADS_PER_STEP):
                head_cols = slice(g * MXU_DIM, (g + 1) * MXU_DIM)
                v_aug = jnp.concatenate(
                    [vt_ref[hh * HEAD_DIM:(hh + 1) * HEAD_DIM, pl.ds(k_start, n_keys)],
                     jnp.ones((PACKED_ROWS, n_keys), BF16)], axis=0)
                acc_ref[hh, :, head_cols] = (
                    alpha[:, group_cols(hh, g)] * acc_ref[hh, :, head_cols]
                    + jnp.dot(v_aug, p_refs[g][0:n_keys, hh * MXU_DIM:(hh + 1) * MXU_DIM],
                              preferred_element_type=F32))

        probabilities(2)
        probabilities(1)
        values(2)
        probabilities(0)
        values(1)
        values(0)

    assert n_chunks == 3
    acc_ref[...] = jnp.zeros_like(acc_ref)
    r_run = jnp.full((1, both), MASK_VALUE, F32)
    def tiles(first, count, r):
        for t in range(count):
            r = bound_tile(first + t, r, False)
        return r

    r_run = lax.fori_loop(0, qi // TILES_PER_TRIP,
                          lambda kk, r: tiles(kk * TILES_PER_TRIP, TILES_PER_TRIP, r), r_run)
    done = qi // TILES_PER_TRIP * TILES_PER_TRIP
    count = TILES_PER_TRIP // 2
    while count:
        has = ((qi - done) & count) != 0
        r_run = lax.cond(has, lambda r, at=done, n=count: tiles(at, n, r), lambda r: r, r_run)
        done = done + jnp.where(has, count, 0)
        count //= 2
    bound_tile(qi, r_run, True)

    def write_output():
        heads_out = [acc_ref[hh, :HEAD_DIM, :] / acc_ref[hh, HEAD_DIM:HEAD_DIM + 1, :]
                     for hh in range(HEADS_PER_STEP)]
        out_ref[q_at, :] = jnp.concatenate(heads_out, axis=0).T.astype(out_ref.dtype)

    write_output()
    smallest_denominator = jnp.min(jnp.minimum(acc_ref[0, HEAD_DIM:HEAD_DIM + 1, :],
                                               acc_ref[1, HEAD_DIM:HEAD_DIM + 1, :]))

    @pl.when(jnp.logical_not(smallest_denominator > DENOM_FLOOR))
    def _():
        def max_tile(kj, m_prev, on_diagonal):
            k_start = pl.multiple_of(kj * tile, LANES)
            m_tile = []
            for c in range(n_chunks):
                s_t = scores(c, k_start, on_diagonal)
                s_refs[c][...] = s_t
                m_tile.append(jnp.max(s_t, axis=0, keepdims=True))
            m_new = jnp.maximum(m_prev, jnp.concatenate(m_tile, axis=1))
            alpha = jnp.exp2(m_prev - m_new)
            for c in range(n_chunks):
                cols = slice(c * CHUNK_COLS, (c + 1) * CHUNK_COLS)
                p_refs[c][...] = jnp.exp2(s_refs[c][...] - m_new[:, cols]).astype(BF16)
                value_stage(c, k_start, alpha)
            return m_new

        acc_ref[...] = jnp.zeros_like(acc_ref)
        set_reference(jnp.zeros((1, both), F32))
        m_run = jnp.full((1, both), MASK_VALUE, F32)
        m_run = lax.fori_loop(0, qi, lambda kj, m: max_tile(kj, m, False), m_run)
        max_tile(qi, m_run, True)
        write_output()


def _attn_call(proj3, proj_t, key_bias, bias_t, key_norm_max, bias_max):
    batch, length, _ = proj3.shape
    pairs = N_HEADS // HEADS_PER_STEP
    bias_t = bias_t.reshape(batch, pairs, HEADS_PER_STEP, length)
    blocks_per_section = D_MODEL // LANES
    n_chunks = HEADS_PER_STEP * ROW_TILE // CHUNK_COLS
    return pl.pallas_call(
        _attn_kernel,
        grid=(batch, pairs),
        in_specs=[
            pl.BlockSpec(memory_space=pltpu.SMEM),
            pl.BlockSpec(memory_space=pltpu.SMEM),
            pl.BlockSpec((None, None, HEADS_PER_STEP, length), lambda b, p: (b, p, 0, 0)),
            pl.BlockSpec((None, LANES, length), lambda b, p: (b, T_Q * blocks_per_section + p, 0)),
            pl.BlockSpec((None, length, LANES), lambda b, p: (b, 0, COL_K * blocks_per_section + p)),
            pl.BlockSpec((None, length, LANES), lambda b, p: (b, 0, 0)),
            pl.BlockSpec((None, LANES, length), lambda b, p: (b, T_V * blocks_per_section + p, 0)),
        ],
        out_specs=pl.BlockSpec((None, length, LANES), lambda b, p: (b, 0, p)),
        out_shape=jax.ShapeDtypeStruct((batch, length, D_MODEL), BF16),
        scratch_shapes=[
            pltpu.VMEM((HEADS_PER_STEP, HEAD_DIM + PACKED_ROWS, ROW_TILE), F32),
            pltpu.VMEM((2 * LANES, HEADS_PER_STEP * ROW_TILE), BF16),
        ] + [pltpu.VMEM((ROW_TILE, CHUNK_COLS), dtype) for dtype in (F32, BF16)
             for _ in range(n_chunks)],
        compiler_params=_params("parallel", "parallel"),
        name="forgetting_attention",
    )(key_norm_max, bias_max, bias_t, proj_t, proj3, key_bias, proj_t)


def _pool_band():
    row = jnp.arange(MIX_BLOCK)[:, None]
    col = jnp.arange(MIX_BLOCK)[None, :]
    bands = [((col <= row) & (col > row - w)).astype(F32) / w - (col == row).astype(F32)
             for w in POOL_WINDOWS]
    return jnp.stack(bands).astype(BF16)


def _mix_ffn_kernel(u_ref, halo_ref, gp_ref, ga_ref, ya_ref, h_ref, band_ref, wp_ref, ps_ref,
                    wo_ref, gain_mix_ref, gain_pre_ref, wg_ref, wu_ref, wd_ref, gain_ffn_ref,
                    gain_next_ref, hres_out, hnorm_out):
    li = pl.program_id(1)

    def mix_block(r0):
        rows = slice(r0, r0 + MIX_BLOCK)
        head = u_ref[r0:r0 + POOL_HALO, :].astype(F32)
        if r0 == 0:
            halo = jnp.where(li > 0, halo_ref[...].astype(F32), 0.0)
        else:
            halo = u_ref[r0 - POOL_HALO:r0, :].astype(F32)
        ext = jnp.concatenate([halo, head], axis=0)
        pos1 = (li * u_ref.shape[0] + r0 + 1
                + lax.broadcasted_iota(jnp.int32, (POOL_HALO, 1), 0))
        pooled = []
        for g, window in enumerate(POOL_WINDOWS):
            lanes = slice(g * POOL_GROUP_DIM, (g + 1) * POOL_GROUP_DIM)
            run = ext[:, lanes]
            width = 1
            while width < window:
                run = run[width:] + run[:-width]
                width *= 2
            first = POOL_HALO + 1 - window
            count = jnp.minimum(pos1, window).astype(F32)
            diff_head = run[first:first + POOL_HALO] / count - head[:, lanes]
            diff = jnp.dot(band_ref[g], u_ref[rows, lanes], preferred_element_type=F32)
            diff = jnp.concatenate([diff_head, diff[POOL_HALO:]], axis=0)
            pooled.append(jnp.dot(diff.astype(BF16), wp_ref[g], preferred_element_type=F32))
        y_pool = jnp.concatenate(pooled, axis=1) * ps_ref[...]
        merged = (jax.nn.sigmoid(gp_ref[rows, :].astype(F32)) * y_pool
                  + jax.nn.sigmoid(ga_ref[rows, :].astype(F32)) * ya_ref[rows, :].astype(F32))
        mix = jnp.dot(merged.astype(BF16), wo_ref[...], preferred_element_type=F32)
        return h_ref[rows, :] + _rms_norm(mix, gain_mix_ref[...])

    def ffn_block(r0, h_mid):
        rows = slice(r0, r0 + MIX_BLOCK)
        hn = _rms_norm(h_mid, gain_pre_ref[...]).astype(BF16)
        ff_out = jnp.zeros(h_mid.shape, F32)
        for c in range(D_FF // FF_CHUNK):
            cols = slice(c * FF_CHUNK, (c + 1) * FF_CHUNK)
            gate = jnp.dot(hn, wg_ref[:, cols], preferred_element_type=F32)
            up = jnp.dot(hn, wu_ref[:, cols], preferred_element_type=F32)
            ff = gate * jax.nn.sigmoid(gate) * up
            ff_out = ff_out + jnp.dot(ff.astype(BF16), wd_ref[cols, :], preferred_element_type=F32)
        h_new = h_mid + _rms_norm(ff_out, gain_ffn_ref[...])
        hres_out[rows, :] = h_new
        hnorm_out[rows, :] = _rms_norm(h_new, gain_next_ref[...]).astype(hnorm_out.dtype)

    n_blocks = u_ref.shape[0] // MIX_BLOCK
    pending = {}
    for step in range(n_blocks + MIX_LEAD):
        if step < n_blocks:
            pending[step] = mix_block(step * MIX_BLOCK)
        if step >= MIX_LEAD:
            ffn_block((step - MIX_LEAD) * MIX_BLOCK, pending.pop(step - MIX_LEAD))


def _mix_ffn_call(proj3, y_attn, h_res3, band, w_pool, pool_scale, w_out, gain_mix, gain_pre,
                  w_gate, w_up, w_down, gain_ffn, gain_next):
    batch, length, _ = proj3.shape
    halo_blocks = FUSED_ROW_TILE // POOL_HALO
    row_spec = lambda col: pl.BlockSpec((None, FUSED_ROW_TILE, D_MODEL), lambda b, i: (b, i, col))
    const = lambda shape: pl.BlockSpec(shape, lambda b, i: (0,) * len(shape),
                                       pipeline_mode=pl.Buffered(1))
    return pl.pallas_call(
        _mix_ffn_kernel,
        grid=(batch, length // FUSED_ROW_TILE),
        in_specs=[
            row_spec(COL_POOL),
            pl.BlockSpec((None, POOL_HALO, D_MODEL),
                         lambda b, i: (b, jnp.maximum(i * halo_blocks - 1, 0), COL_POOL)),
            row_spec(COL_GPOOL),
            row_spec(COL_GATTN),
            row_spec(0),
            row_spec(0),
            const((len(POOL_WINDOWS), MIX_BLOCK, MIX_BLOCK)),
            const((len(POOL_WINDOWS), POOL_GROUP_DIM, POOL_GROUP_DIM)),
            const((1, D_MODEL)),
            const((D_MODEL, D_MODEL)),
            const((1, D_MODEL)),
            const((1, D_MODEL)),
            const((D_MODEL, D_FF)), const((D_MODEL, D_FF)), const((D_FF, D_MODEL)),
            const((1, D_MODEL)),
            const((1, D_MODEL)),
        ],
        out_specs=[row_spec(0), row_spec(0)],
        out_shape=[jax.ShapeDtypeStruct((batch, length, D_MODEL), F32),
                   jax.ShapeDtypeStruct((batch, length, D_MODEL), BF16)],
        compiler_params=_params("parallel", "parallel"),
        name="mix_ffn",
    )(proj3, proj3, proj3, proj3, y_attn, h_res3, band, w_pool, pool_scale, w_out, gain_mix,
      gain_pre, w_gate, w_up, w_down, gain_ffn, gain_next)


def kernel(x, meta_tokens, norm_mix_pre, norm_mix_post, norm_ffn_pre, norm_ffn_post,
           w_in, b_forget, w_pool, pool_scale, w_out, w_ffn_gate, w_ffn_up, w_ffn_down):
    batch, seq, _ = x.shape
    depth = w_in.shape[0]
    tokens = META_TOKENS + seq
    length = -(-tokens // ROW_TILE) * ROW_TILE
    meta = jnp.broadcast_to(meta_tokens[None].astype(x.dtype), (batch, META_TOKENS, D_MODEL))
    tail = jnp.zeros((batch, length - tokens, D_MODEL), x.dtype)
    h_res = jnp.concatenate([meta, x, tail], axis=1).reshape(batch * length, D_MODEL)
    gain = lambda g: g.reshape(1, D_MODEL).astype(F32)
    sec = lambda w, n: w[:, n * D_MODEL:(n + 1) * D_MODEL]
    gates_at = 4 * D_MODEL + N_HEADS
    lane_head = jnp.minimum(jnp.arange(LANES) // BIAS_PIECES, N_HEADS - 1)
    head_indicator = (jnp.arange(D_MODEL)[:, None] // HEAD_DIM == jnp.arange(LANES)[None, :]).astype(BF16)

    pool_band = _pool_band()

    h_norm = _norm_call(h_res, gain(norm_mix_pre[0]))
    for l in range(depth):
        w = w_in[l]
        w_rows = jnp.concatenate([sec(w, 0), sec(w, 2), w[:, gates_at:]], axis=1).astype(BF16)
        w_t = jnp.concatenate([sec(w, 1), sec(w, 3)], axis=1).T.astype(BF16)
        w_forget = w[:, 4 * D_MODEL:gates_at]
        wf_rep = w_forget[:, lane_head].astype(BF16)
        bias_rep = b_forget[l][lane_head].reshape(1, LANES).astype(F32)

        h_norm3 = h_norm.reshape(batch, length, D_MODEL)
        proj3 = _proj_call(h_norm, w_rows).reshape(batch, length, N_ROW_BLOCKS * D_MODEL)
        proj_t = _proj_t_call(h_norm3, w_t)
        key_bias, bias_end, bias_t = _forget_call(h_norm3, wf_rep, bias_rep)
        key_norms = _key_norm_call(proj3, head_indicator)
        key_norm_max = key_norms[:, :, 0, :N_HEADS].reshape(-1)
        bias_max = bias_end[:, :, 0, 0:ONES_LANE:BIAS_PIECES].reshape(-1)
        y_attn = _attn_call(proj3, proj_t, key_bias, bias_t, key_norm_max, bias_max)
        next_gain = norm_mix_pre[(l + 1) % depth]
        h_res3, h_norm3 = _mix_ffn_call(
            proj3, y_attn, h_res.reshape(batch, length, D_MODEL), pool_band,
            w_pool[l].astype(BF16), gain(pool_scale[l]), w_out[l].astype(BF16),
            gain(norm_mix_post[l]), gain(norm_ffn_pre[l]),
            w_ffn_gate[l].astype(BF16), w_ffn_up[l].astype(BF16), w_ffn_down[l].astype(BF16),
            gain(norm_ffn_post[l]), gain(next_gain))
        h_res = h_res3.reshape(batch * length, D_MODEL)
        h_norm = h_norm3.reshape(batch * length, D_MODEL)
    return h_res.reshape(batch, length, D_MODEL)[:, META_TOKENS:tokens]
```

```python
import math

import jax
import jax.numpy as jnp
from jax import lax
from jax.experimental import pallas as pl
from jax.experimental.pallas import tpu as pltpu

D_MODEL = 1024
N_HEADS = 16
HEAD_DIM = 64
META_TOKENS = 16
POOL_WINDOWS = (2, 4, 8, 16)
POOL_GROUP_DIM = D_MODEL // len(POOL_WINDOWS)
POOL_HALO = max(POOL_WINDOWS)
D_FF = 2816
FF_CHUNK = D_FF
RMS_EPS = 1e-6
MASK_VALUE = -1e30
LOG2_E = math.log2(math.e)

LANES = 128
SUBLANES = 8
MXU_DIM = 256
ROW_TILE = 3 * MXU_DIM
CHUNK_COLS = 2 * MXU_DIM
TILES_PER_TRIP = 4
FUSED_ROW_TILE = ROW_TILE
MIX_BLOCK = FUSED_ROW_TILE // 4
MIX_LEAD = 2
PROJ_ROW_TILES = (11 * MXU_DIM, 2 * ROW_TILE, ROW_TILE)
PROJ_T_STEPS = 3
N_ROW_BLOCKS = 4
COL_POOL, COL_K, COL_GPOOL, COL_GATTN = range(N_ROW_BLOCKS)
N_T_BLOCKS = 2
T_Q, T_V = range(N_T_BLOCKS)
HEADS_PER_STEP = LANES // HEAD_DIM
BIAS_PIECES = 3
ONES_LANE = BIAS_PIECES * N_HEADS
REF_ROW = LANES + ONES_LANE
PACKED_ROWS = 16
NORM_SLACK = 1.01
BIAS_SLACK = 0.05
REFERENCE_HEADROOM = 32.0
DENOM_FLOOR = 2.0 ** -80
VMEM_LIMIT_BYTES = 56 * 1024 * 1024

F32 = jnp.float32
BF16 = jnp.bfloat16


def _rms_norm(x, gain):
    return x * lax.rsqrt(jnp.mean(x * x, axis=-1, keepdims=True) + RMS_EPS) * gain


def _params(*semantics):
    return pltpu.CompilerParams(dimension_semantics=semantics, vmem_limit_bytes=VMEM_LIMIT_BYTES)


def _split_bf16(x):
    hi = x.astype(BF16)
    rest = x - hi.astype(F32)
    mid = rest.astype(BF16)
    lo = (rest - mid.astype(F32)).astype(BF16)
    return hi, mid, lo


def _norm_kernel(h_ref, gain_ref, out_ref):
    out_ref[...] = _rms_norm(h_ref[...], gain_ref[...]).astype(out_ref.dtype)


def _norm_call(h_res, gain):
    rows = h_res.shape[0]
    return pl.pallas_call(
        _norm_kernel,
        grid=(rows // ROW_TILE,),
        in_specs=[pl.BlockSpec((ROW_TILE, D_MODEL), lambda i: (i, 0)),
                  pl.BlockSpec((1, D_MODEL), lambda i: (0, 0))],
        out_specs=pl.BlockSpec((ROW_TILE, D_MODEL), lambda i: (i, 0)),
        out_shape=jax.ShapeDtypeStruct((rows, D_MODEL), BF16),
        compiler_params=_params("parallel"),
        name="pre_norm",
    )(h_res, gain)


def _proj_kernel(h_ref, w_ref, out_ref):
    out_ref[...] = jnp.dot(h_ref[...], w_ref[...], preferred_element_type=F32).astype(out_ref.dtype)


def _proj_call(h_norm, w_rows):
    rows = h_norm.shape[0]
    tile = next(t for t in PROJ_ROW_TILES if rows % t == 0)
    return pl.pallas_call(
        _proj_kernel,
        grid=(rows // tile, N_ROW_BLOCKS),
        in_specs=[pl.BlockSpec((tile, D_MODEL), lambda i, j: (i, 0)),
                  pl.BlockSpec((D_MODEL, D_MODEL), lambda i, j: (0, j))],
        out_specs=pl.BlockSpec((tile, D_MODEL), lambda i, j: (i, j)),
        out_shape=jax.ShapeDtypeStruct((rows, N_ROW_BLOCKS * D_MODEL), BF16),
        compiler_params=_params("parallel", "arbitrary"),
        name="in_proj",
    )(h_norm, w_rows)


def _proj_t_kernel(h_ref, wt_ref, out_ref):
    acc = lax.dot_general(wt_ref[...], h_ref[...], (((1,), (1,)), ((), ())),
                          preferred_element_type=F32)
    scale = jnp.where(pl.program_id(2) == T_Q, LOG2_E * HEAD_DIM ** -0.5, 1.0).astype(F32)
    out_ref[...] = (acc * scale).astype(out_ref.dtype)


def _proj_t_call(h_norm3, w_t):
    batch, length, _ = h_norm3.shape
    tile = length // PROJ_T_STEPS
    assert tile * PROJ_T_STEPS == length and tile % LANES == 0
    return pl.pallas_call(
        _proj_t_kernel,
        grid=(batch, PROJ_T_STEPS, N_T_BLOCKS),
        in_specs=[pl.BlockSpec((None, tile, D_MODEL), lambda b, i, j: (b, i, 0)),
                  pl.BlockSpec((D_MODEL, D_MODEL), lambda b, i, j: (j, 0))],
        out_specs=pl.BlockSpec((None, D_MODEL, tile), lambda b, i, j: (b, j, i)),
        out_shape=jax.ShapeDtypeStruct((batch, N_T_BLOCKS * D_MODEL, length), BF16),
        compiler_params=_params("parallel", "parallel", "arbitrary"),
        name="in_proj_t",
    )(h_norm3, w_t)


def _forget_kernel(h_ref, wf_ref, bias_ref, out_ref, end_ref, t_ref, carry_ref):
    @pl.when(pl.program_id(1) == 0)
    def _():
        carry_ref[...] = jnp.zeros_like(carry_ref)

    logit = jnp.dot(h_ref[...], wf_ref[...], preferred_element_type=F32) + bias_ref[...]
    log_f = jnp.minimum(logit, 0.0) - jnp.log(1.0 + jnp.exp(-jnp.abs(logit)))
    tile = log_f.shape[0]
    dst = lax.broadcasted_iota(jnp.int32, (MXU_DIM, MXU_DIM), 0)
    src = lax.broadcasted_iota(jnp.int32, (MXU_DIM, MXU_DIM), 1)
    prefix = (src <= dst).astype(BF16)
    pieces = _split_bf16(log_f)
    carry = carry_ref[...]
    blocks = []
    for r0 in range(0, tile, MXU_DIM):
        block = jnp.broadcast_to(carry, (MXU_DIM, LANES))
        for piece in pieces:
            block = block + jnp.dot(prefix, piece[r0:r0 + MXU_DIM, :], preferred_element_type=F32)
        carry = block[MXU_DIM - 1:MXU_DIM, :]
        blocks.append(block)
    total = jnp.concatenate(blocks, axis=0)
    carry_ref[...] = carry
    bias = -LOG2_E * total
    end_ref[...] = jnp.broadcast_to(bias[tile - 1:tile, :], end_ref.shape)
    hi, mid, lo = _split_bf16(bias)
    lane = lax.broadcasted_iota(jnp.int32, total.shape, 1)
    piece = lane % BIAS_PIECES
    out = jnp.where(piece == 0, hi, jnp.where(piece == 1, mid, lo))
    tail = jnp.where(lane < ONES_LANE + BIAS_PIECES, 1.0, 0.0).astype(BF16)
    out_ref[...] = jnp.where(lane < ONES_LANE, out, tail)
    head = lax.broadcasted_iota(jnp.int32, (N_HEADS, LANES), 0)
    lane_h = lax.broadcasted_iota(jnp.int32, (N_HEADS, LANES), 1)
    pick = (lane_h >= head * BIAS_PIECES) & (lane_h < (head + 1) * BIAS_PIECES)
    t_ref[...] = lax.dot_general(jnp.where(pick, 1.0, 0.0).astype(BF16), out,
                                 (((1,), (1,)), ((), ())), preferred_element_type=F32)


def _forget_call(h_norm3, wf_rep, bias_rep):
    batch, length, _ = h_norm3.shape
    n_tiles = length // ROW_TILE
    return pl.pallas_call(
        _forget_kernel,
        grid=(batch, n_tiles),
        in_specs=[pl.BlockSpec((None, ROW_TILE, D_MODEL), lambda b, i: (b, i, 0)),
                  pl.BlockSpec((D_MODEL, LANES), lambda b, i: (0, 0)),
                  pl.BlockSpec((1, LANES), lambda b, i: (0, 0))],
        out_specs=[pl.BlockSpec((None, ROW_TILE, LANES), lambda b, i: (b, i, 0)),
                   pl.BlockSpec((None, None, SUBLANES, LANES), lambda b, i: (b, i, 0, 0)),
                   pl.BlockSpec((None, N_HEADS, ROW_TILE), lambda b, i: (b, 0, i))],
        out_shape=[jax.ShapeDtypeStruct((batch, length, LANES), BF16),
                   jax.ShapeDtypeStruct((batch, n_tiles, SUBLANES, LANES), F32),
                   jax.ShapeDtypeStruct((batch, N_HEADS, length), F32)],
        scratch_shapes=[pltpu.VMEM((1, LANES), F32)],
        compiler_params=_params("parallel", "arbitrary"),
        name="forget_bias",
    )(h_norm3, wf_rep, bias_rep)


def _key_norm_kernel(k_ref, head_ref, out_ref):
    k = k_ref[...].astype(F32)
    sums = jnp.dot((k * k).astype(BF16), head_ref[...], preferred_element_type=F32)
    out_ref[...] = jnp.broadcast_to(jnp.sqrt(jnp.max(sums, axis=0, keepdims=True)), out_ref.shape)


def _key_norm_call(proj3, head_indicator):
    batch, length, _ = proj3.shape
    n_tiles = length // ROW_TILE
    return pl.pallas_call(
        _key_norm_kernel,
        grid=(batch, n_tiles),
        in_specs=[pl.BlockSpec((None, ROW_TILE, D_MODEL), lambda b, i: (b, i, COL_K)),
                  pl.BlockSpec((D_MODEL, LANES), lambda b, i: (0, 0))],
        out_specs=pl.BlockSpec((None, None, SUBLANES, LANES), lambda b, i: (b, i, 0, 0)),
        out_shape=jax.ShapeDtypeStruct((batch, n_tiles, SUBLANES, LANES), F32),
        compiler_params=_params("parallel", "parallel"),
        name="key_norms",
    )(proj3, head_indicator)


def _attn_kernel(kmax_ref, gend_ref, gt_ref, qt_ref, k_ref, fb_ref, vt_ref, out_ref, *scratch):
    n_tiles = k_ref.shape[0] // ROW_TILE

    def one_tile(qi, carry):
        _attn_query_tile(qi, n_tiles, kmax_ref, gend_ref, gt_ref, qt_ref, k_ref, fb_ref, vt_ref,
                         out_ref, *scratch)
        return carry

    lax.fori_loop(0, n_tiles, one_tile, 0)


def _attn_query_tile(qi, n_tiles, kmax_ref, gend_ref, gt_ref, qt_ref, k_ref, fb_ref, vt_ref,
                     out_ref, acc_ref, qaug_ref, *chunk_refs):
    bi = pl.program_id(0)
    pair = pl.program_id(1)
    tile = ROW_TILE
    q_at = pl.ds(pl.multiple_of(qi * tile, LANES), tile)
    q_tile = qt_ref[:, q_at]
    both = HEADS_PER_STEP * tile
    n_chunks = both // CHUNK_COLS
    s_refs, p_refs = chunk_refs[:n_chunks], chunk_refs[n_chunks:]

    feat = lax.broadcasted_iota(jnp.int32, (LANES, tile), 0)
    q_norms = []
    for hh in range(HEADS_PER_STEP):
        head = pair * HEADS_PER_STEP + hh
        in_head = (feat >= hh * HEAD_DIM) & (feat < (hh + 1) * HEAD_DIM)
        q_top = jnp.where(in_head, q_tile, jnp.zeros_like(q_tile))
        is_piece = (feat >= head * BIAS_PIECES) & (feat < (head + 1) * BIAS_PIECES)
        qaug_ref[:, hh * tile:(hh + 1) * tile] = jnp.concatenate(
            [q_top, is_piece.astype(BF16)], axis=0)
        q_head = q_tile[hh * HEAD_DIM:(hh + 1) * HEAD_DIM, :].astype(F32)
        q_norms.append(jnp.sqrt(jnp.sum(q_head * q_head, axis=0, keepdims=True)))
    q_norm = jnp.concatenate(q_norms, axis=1)
    second_head = lax.broadcasted_iota(jnp.int32, (1, both), 1) >= tile
    g_diagonal = jnp.concatenate([gt_ref[hh:hh + 1, q_at] for hh in range(HEADS_PER_STEP)], axis=1)

    def head_segments(c):
        segments = []
        start = c * CHUNK_COLS
        while start < (c + 1) * CHUNK_COLS:
            hh = start // tile
            stop = min((c + 1) * CHUNK_COLS, (hh + 1) * tile)
            segments.append((hh, slice(start - hh * tile, stop - hh * tile),
                             slice(start - c * CHUNK_COLS, stop - c * CHUNK_COLS)))
            start = stop
        return segments

    def scores(c, k_start, on_diagonal):
        keys = jnp.concatenate([k_ref[pl.ds(k_start, tile), :],
                                fb_ref[pl.ds(k_start, tile), :]], axis=1)
        s_t = jnp.dot(keys, qaug_ref[:, c * CHUNK_COLS:(c + 1) * CHUNK_COLS],
                      preferred_element_type=F32)
        if on_diagonal:
            key_row = lax.broadcasted_iota(jnp.int32, s_t.shape, 0)
            col = lax.broadcasted_iota(jnp.int32, s_t.shape, 1) + c * CHUNK_COLS
            query_col = jnp.where(col >= tile, col - tile, col)
            s_t = jnp.where(key_row <= query_col, s_t, MASK_VALUE)
        return s_t

    def value_stage(c, k_start, alpha):
        for hh, head_cols, chunk_cols in head_segments(c):
            v_aug = jnp.concatenate(
                [vt_ref[hh * HEAD_DIM:(hh + 1) * HEAD_DIM, pl.ds(k_start, tile)],
                 jnp.ones((PACKED_ROWS, tile), BF16)], axis=0)
            acc_ref[hh, :, head_cols] = (
                alpha[:, c * CHUNK_COLS:(c + 1) * CHUNK_COLS][:, chunk_cols]
                * acc_ref[hh, :, head_cols]
                + jnp.dot(v_aug, p_refs[c][:, chunk_cols], preferred_element_type=F32))

    def set_reference(r):
        hi, mid, lo = (piece.astype(F32) for piece in _split_bf16(-r))
        row = lax.broadcasted_iota(jnp.int32, (PACKED_ROWS, both), 0)
        pieces = jnp.where(row == 0, hi, jnp.where(row == 1, mid, jnp.where(row == 2, lo, 0.0)))
        qaug_ref[REF_ROW:REF_ROW + PACKED_ROWS, :] = pieces.astype(BF16)

    def bound_tile(kj, r_prev, on_diagonal):
        k_start = pl.multiple_of(kj * tile, LANES)
        base = (bi * n_tiles + kj) * N_HEADS + pair * HEADS_PER_STEP
        k_max = jnp.where(second_head, kmax_ref[base + 1], kmax_ref[base])
        if on_diagonal:
            g_max = g_diagonal
        else:
            g_max = jnp.where(second_head, gend_ref[base + 1], gend_ref[base])
        bound = q_norm * k_max * NORM_SLACK + g_max + BIAS_SLACK
        r_new = jnp.maximum(r_prev, bound - REFERENCE_HEADROOM)
        alpha = jnp.exp2(r_prev - r_new)
        set_reference(r_new)

        if on_diagonal:
            diagonal_groups(k_start, alpha)
            return r_new

        def probabilities(c):
            p_refs[c][...] = jnp.exp2(scores(c, k_start, False)).astype(BF16)

        probabilities(0)
        probabilities(1)
        value_stage(0, k_start, alpha)
        probabilities(2)
        value_stage(1, k_start, alpha)
        value_stage(2, k_start, alpha)
        return r_new

    def diagonal_groups(k_start, alpha):
        n_groups = tile // MXU_DIM
        assert n_groups == n_chunks

        def group_cols(hh, g):
            return slice(hh * tile + g * MXU_DIM, hh * tile + (g + 1) * MXU_DIM)

        def probabilities(g):
            n_keys = (g + 1) * MXU_DIM
            keys = jnp.concatenate([k_ref[pl.ds(k_start, n_keys), :],
                                    fb_ref[pl.ds(k_start, n_keys), :]], axis=1)
            queries = jnp.concatenate(
                [qaug_ref[:, group_cols(hh, g)] for hh in range(HEADS_PER_STEP)], axis=1)
            s_t = jnp.dot(keys, queries, preferred_element_type=F32)
            key_row = lax.broadcasted_iota(jnp.int32, s_t.shape, 0)
            col = lax.broadcasted_iota(jnp.int32, s_t.shape, 1)
            query_col = jnp.where(col >= MXU_DIM, col - MXU_DIM, col) + g * MXU_DIM
            s_t = jnp.where(key_row <= query_col, s_t, MASK_VALUE)
            p_refs[g][0:n_keys, :] = jnp.exp2(s_t).astype(BF16)

        def values(g):
            n_keys = (g + 1) * MXU_DIM
            for hh in range(HEADS_PER_STEP):
                head_cols = slice(g * MXU_DIM, (g + 1) * MXU_DIM)
                v_aug = jnp.concatenate(
                    [vt_ref[hh * HEAD_DIM:(hh + 1) * HEAD_DIM, pl.ds(k_start, n_keys)],
                     jnp.ones((PACKED_ROWS, n_keys), BF16)], axis=0)
                acc_ref[hh, :, head_cols] = (
                    alpha[:, group_cols(hh, g)] * acc_ref[hh, :, head_cols]
                    + jnp.dot(v_aug, p_refs[g][0:n_keys, hh * MXU_DIM:(hh + 1) * MXU_DIM],
                              preferred_element_type=F32))

        probabilities(2)
        probabilities(1)
        values(2)
        probabilities(0)
        values(1)
        values(0)

    assert n_chunks == 3
    acc_ref[...] = jnp.zeros_like(acc_ref)
    r_run = jnp.full((1, both), MASK_VALUE, F32)
    def tiles(first, count, r):
        for t in range(count):
            r = bound_tile(first + t, r, False)
        return r

    r_run = lax.fori_loop(0, qi // TILES_PER_TRIP,
                          lambda kk, r: tiles(kk * TILES_PER_TRIP, TILES_PER_TRIP, r), r_run)
    done = qi // TILES_PER_TRIP * TILES_PER_TRIP
    count = TILES_PER_TRIP // 2
    while count:
        has = ((qi - done) & count) != 0
        r_run = lax.cond(has, lambda r, at=done, n=count: tiles(at, n, r), lambda r: r, r_run)
        done = done + jnp.where(has, count, 0)
        count //= 2
    bound_tile(qi, r_run, True)

    def write_output():
        heads_out = [acc_ref[hh, :HEAD_DIM, :] / acc_ref[hh, HEAD_DIM:HEAD_DIM + 1, :]
                     for hh in range(HEADS_PER_STEP)]
        out_ref[q_at, :] = jnp.concatenate(heads_out, axis=0).T.astype(out_ref.dtype)

    write_output()
    smallest_denominator = jnp.min(jnp.minimum(acc_ref[0, HEAD_DIM:HEAD_DIM + 1, :],
                                               acc_ref[1, HEAD_DIM:HEAD_DIM + 1, :]))

    @pl.when(jnp.logical_not(smallest_denominator > DENOM_FLOOR))
    def _():
        def max_tile(kj, m_prev, on_diagonal):
            k_start = pl.multiple_of(kj * tile, LANES)
            m_tile = []
            for c in range(n_chunks):
                s_t = scores(c, k_start, on_diagonal)
                s_refs[c][...] = s_t
                m_tile.append(jnp.max(s_t, axis=0, keepdims=True))
            m_new = jnp.maximum(m_prev, jnp.concatenate(m_tile, axis=1))
            alpha = jnp.exp2(m_prev - m_new)
            for c in range(n_chunks):
                cols = slice(c * CHUNK_COLS, (c + 1) * CHUNK_COLS)
                p_refs[c][...] = jnp.exp2(s_refs[c][...] - m_new[:, cols]).astype(BF16)
                value_stage(c, k_start, alpha)
            return m_new

        acc_ref[...] = jnp.zeros_like(acc_ref)
        set_reference(jnp.zeros((1, both), F32))
        m_run = jnp.full((1, both), MASK_VALUE, F32)
        m_run = lax.fori_loop(0, qi, lambda kj, m: max_tile(kj, m, False), m_run)
        max_tile(qi, m_run, True)
        write_output()


def _attn_call(proj3, proj_t, key_bias, bias_t, key_norm_max, bias_max):
    batch, length, _ = proj3.shape
    pairs = N_HEADS // HEADS_PER_STEP
    bias_t = bias_t.reshape(batch, pairs, HEADS_PER_STEP, length)
    blocks_per_section = D_MODEL // LANES
    n_chunks = HEADS_PER_STEP * ROW_TILE // CHUNK_COLS
    return pl.pallas_call(
        _attn_kernel,
        grid=(batch, pairs),
        in_specs=[
            pl.BlockSpec(memory_space=pltpu.SMEM),
            pl.BlockSpec(memory_space=pltpu.SMEM),
            pl.BlockSpec((None, None, HEADS_PER_STEP, length), lambda b, p: (b, p, 0, 0)),
            pl.BlockSpec((None, LANES, length), lambda b, p: (b, T_Q * blocks_per_section + p, 0)),
            pl.BlockSpec((None, length, LANES), lambda b, p: (b, 0, COL_K * blocks_per_section + p)),
            pl.BlockSpec((None, length, LANES), lambda b, p: (b, 0, 0)),
            pl.BlockSpec((None, LANES, length), lambda b, p: (b, T_V * blocks_per_section + p, 0)),
        ],
        out_specs=pl.BlockSpec((None, length, LANES), lambda b, p: (b, 0, p)),
        out_shape=jax.ShapeDtypeStruct((batch, length, D_MODEL), BF16),
        scratch_shapes=[
            pltpu.VMEM((HEADS_PER_STEP, HEAD_DIM + PACKED_ROWS, ROW_TILE), F32),
            pltpu.VMEM((2 * LANES, HEADS_PER_STEP * ROW_TILE), BF16),
        ] + [pltpu.VMEM((ROW_TILE, CHUNK_COLS), dtype) for dtype in (F32, BF16)
             for _ in range(n_chunks)],
        compiler_params=_params("parallel", "parallel"),
        name="forgetting_attention",
    )(key_norm_max, bias_max, bias_t, proj_t, proj3, key_bias, proj_t)


def _pool_band():
    row = jnp.arange(MIX_BLOCK)[:, None]
    col = jnp.arange(MIX_BLOCK)[None, :]
    bands = [((col <= row) & (col > row - w)).astype(F32) / w - (col == row).astype(F32)
             for w in POOL_WINDOWS]
    return jnp.stack(bands).astype(BF16)


def _mix_ffn_kernel(u_ref, halo_ref, gp_ref, ga_ref, ya_ref, h_ref, band_ref, wp_ref, ps_ref,
                    wo_ref, gain_mix_ref, gain_pre_ref, wg_ref, wu_ref, wd_ref, gain_ffn_ref,
                    gain_next_ref, hres_out, hnorm_out):
    li = pl.program_id(1)

    def mix_block(r0):
        rows = slice(r0, r0 + MIX_BLOCK)
        head = u_ref[r0:r0 + POOL_HALO, :].astype(F32)
        if r0 == 0:
            halo = jnp.where(li > 0, halo_ref[...].astype(F32), 0.0)
        else:
            halo = u_ref[r0 - POOL_HALO:r0, :].astype(F32)
        ext = jnp.concatenate([halo, head], axis=0)
        pos1 = (li * u_ref.shape[0] + r0 + 1
                + lax.broadcasted_iota(jnp.int32, (POOL_HALO, 1), 0))
        pooled = []
        for g, window in enumerate(POOL_WINDOWS):
            lanes = slice(g * POOL_GROUP_DIM, (g + 1) * POOL_GROUP_DIM)
            run = ext[:, lanes]
            width = 1
            while width < window:
                run = run[width:] + run[:-width]
                width *= 2
            first = POOL_HALO + 1 - window
            count = jnp.minimum(pos1, window).astype(F32)
            diff_head = run[first:first + POOL_HALO] / count - head[:, lanes]
            diff = jnp.dot(band_ref[g], u_ref[rows, lanes], preferred_element_type=F32)
            diff = jnp.concatenate([diff_head, diff[POOL_HALO:]], axis=0)
            pooled.append(jnp.dot(diff.astype(BF16), wp_ref[g], preferred_element_type=F32))
        y_pool = jnp.concatenate(pooled, axis=1) * ps_ref[...]
        merged = (jax.nn.sigmoid(gp_ref[rows, :].astype(F32)) * y_pool
                  + jax.nn.sigmoid(ga_ref[rows, :].astype(F32)) * ya_ref[rows, :].astype(F32))
        mix = jnp.dot(merged.astype(BF16), wo_ref[...], preferred_element_type=F32)
        return h_ref[rows, :] + _rms_norm(mix, gain_mix_ref[...])

    def ffn_block(r0, h_mid):
        rows = slice(r0, r0 + MIX_BLOCK)
        hn = _rms_norm(h_mid, gain_pre_ref[...]).astype(BF16)
        ff_out = jnp.zeros(h_mid.shape, F32)
        for c in range(D_FF // FF_CHUNK):
            cols = slice(c * FF_CHUNK, (c + 1) * FF_CHUNK)
            gate = jnp.dot(hn, wg_ref[:, cols], preferred_element_type=F32)
            up = jnp.dot(hn, wu_ref[:, cols], preferred_element_type=F32)
            ff = gate * jax.nn.sigmoid(gate) * up
            ff_out = ff_out + jnp.dot(ff.astype(BF16), wd_ref[cols, :], preferred_element_type=F32)
        h_new = h_mid + _rms_norm(ff_out, gain_ffn_ref[...])
        hres_out[rows, :] = h_new
        hnorm_out[rows, :] = _rms_norm(h_new, gain_next_ref[...]).astype(hnorm_out.dtype)

    n_blocks = u_ref.shape[0] // MIX_BLOCK
    pending = {}
    for step in range(n_blocks + MIX_LEAD):
        if step < n_blocks:
            pending[step] = mix_block(step * MIX_BLOCK)
        if step >= MIX_LEAD:
            ffn_block((step - MIX_LEAD) * MIX_BLOCK, pending.pop(step - MIX_LEAD))


def _mix_ffn_call(proj3, y_attn, h_res3, band, w_pool, pool_scale, w_out, gain_mix, gain_pre,
                  w_gate, w_up, w_down, gain_ffn, gain_next):
    batch, length, _ = proj3.shape
    halo_blocks = FUSED_ROW_TILE // POOL_HALO
    row_spec = lambda col: pl.BlockSpec((None, FUSED_ROW_TILE, D_MODEL), lambda b, i: (b, i, col))
    const = lambda shape: pl.BlockSpec(shape, lambda b, i: (0,) * len(shape),
                                       pipeline_mode=pl.Buffered(1))
    return pl.pallas_call(
        _mix_ffn_kernel,
        grid=(batch, length // FUSED_ROW_TILE),
        in_specs=[
            row_spec(COL_POOL),
            pl.BlockSpec((None, POOL_HALO, D_MODEL),
                         lambda b, i: (b, jnp.maximum(i * halo_blocks - 1, 0), COL_POOL)),
            row_spec(COL_GPOOL),
            row_spec(COL_GATTN),
            row_spec(0),
            row_spec(0),
            const((len(POOL_WINDOWS), MIX_BLOCK, MIX_BLOCK)),
            const((len(POOL_WINDOWS), POOL_GROUP_DIM, POOL_GROUP_DIM)),
            const((1, D_MODEL)),
            const((D_MODEL, D_MODEL)),
            const((1, D_MODEL)),
            const((1, D_MODEL)),
            const((D_MODEL, D_FF)), const((D_MODEL, D_FF)), const((D_FF, D_MODEL)),
            const((1, D_MODEL)),
            const((1, D_MODEL)),
        ],
        out_specs=[row_spec(0), row_spec(0)],
        out_shape=[jax.ShapeDtypeStruct((batch, length, D_MODEL), F32),
                   jax.ShapeDtypeStruct((batch, length, D_MODEL), BF16)],
        compiler_params=_params("parallel", "parallel"),
        name="mix_ffn",
    )(proj3, proj3, proj3, proj3, y_attn, h_res3, band, w_pool, pool_scale, w_out, gain_mix,
      gain_pre, w_gate, w_up, w_down, gain_ffn, gain_next)


def kernel(x, meta_tokens, norm_mix_pre, norm_mix_post, norm_ffn_pre, norm_ffn_post,
           w_in, b_forget, w_pool, pool_scale, w_out, w_ffn_gate, w_ffn_up, w_ffn_down):
    batch, seq, _ = x.shape
    depth = w_in.shape[0]
    tokens = META_TOKENS + seq
    length = -(-tokens // ROW_TILE) * ROW_TILE
    meta = jnp.broadcast_to(meta_tokens[None].astype(x.dtype), (batch, META_TOKENS, D_MODEL))
    tail = jnp.zeros((batch, length - tokens, D_MODEL), x.dtype)
    h_res = jnp.concatenate([meta, x, tail], axis=1).reshape(batch * length, D_MODEL)
    gain = lambda g: g.reshape(1, D_MODEL).astype(F32)
    sec = lambda w, n: w[:, n * D_MODEL:(n + 1) * D_MODEL]
    gates_at = 4 * D_MODEL + N_HEADS
    lane_head = jnp.minimum(jnp.arange(LANES) // BIAS_PIECES, N_HEADS - 1)
    head_indicator = (jnp.arange(D_MODEL)[:, None] // HEAD_DIM == jnp.arange(LANES)[None, :]).astype(BF16)

    pool_band = _pool_band()

    h_norm = _norm_call(h_res, gain(norm_mix_pre[0]))
    for l in range(depth):
        w = w_in[l]
        w_rows = jnp.concatenate([sec(w, 0), sec(w, 2), w[:, gates_at:]], axis=1).astype(BF16)
        w_t = jnp.concatenate([sec(w, 1), sec(w, 3)], axis=1).T.astype(BF16)
        w_forget = w[:, 4 * D_MODEL:gates_at]
        wf_rep = w_forget[:, lane_head].astype(BF16)
        bias_rep = b_forget[l][lane_head].reshape(1, LANES).astype(F32)

        h_norm3 = h_norm.reshape(batch, length, D_MODEL)
        proj3 = _proj_call(h_norm, w_rows).reshape(batch, length, N_ROW_BLOCKS * D_MODEL)
        proj_t = _proj_t_call(h_norm3, w_t)
        key_bias, bias_end, bias_t = _forget_call(h_norm3, wf_rep, bias_rep)
        key_norms = _key_norm_call(proj3, head_indicator)
        key_norm_max = key_norms[:, :, 0, :N_HEADS].reshape(-1)
        bias_max = bias_end[:, :, 0, 0:ONES_LANE:BIAS_PIECES].reshape(-1)
        y_attn = _attn_call(proj3, proj_t, key_bias, bias_t, key_norm_max, bias_max)
        next_gain = norm_mix_pre[(l + 1) % depth]
        h_res3, h_norm3 = _mix_ffn_call(
            proj3, y_attn, h_res.reshape(batch, length, D_MODEL), pool_band,
            w_pool[l].astype(BF16), gain(pool_scale[l]), w_out[l].astype(BF16),
            gain(norm_mix_post[l]), gain(norm_ffn_pre[l]),
            w_ffn_gate[l].astype(BF16), w_ffn_up[l].astype(BF16), w_ffn_down[l].astype(BF16),
            gain(norm_ffn_post[l]), gain(next_gain))
        h_res = h_res3.reshape(batch * length, D_MODEL)
        h_norm = h_norm3.reshape(batch * length, D_MODEL)
    return h_res.reshape(batch, length, D_MODEL)[:, META_TOKENS:tokens]
```

```python
import math

import jax
import jax.numpy as jnp
from jax import lax
from jax.experimental import pallas as pl
from jax.experimental.pallas import tpu as pltpu

D_MODEL = 1024
N_HEADS = 16
HEAD_DIM = 64
META_TOKENS = 16
POOL_WINDOWS = (2, 4, 8, 16)
POOL_GROUP_DIM = D_MODEL // len(POOL_WINDOWS)
POOL_HALO = max(POOL_WINDOWS)
D_FF = 2816
FF_CHUNK = D_FF
RMS_EPS = 1e-6
MASK_VALUE = -1e30
LOG2_E = math.log2(math.e)

LANES = 128
SUBLANES = 8
MXU_DIM = 256
ROW_TILE = 3 * MXU_DIM
CHUNK_COLS = 2 * MXU_DIM
TILES_PER_TRIP = 4
FUSED_ROW_TILE = ROW_TILE
MIX_BLOCK = FUSED_ROW_TILE // 4
MIX_LEAD = 2
PROJ_ROW_TILES = (11 * MXU_DIM, 2 * ROW_TILE, ROW_TILE)
PROJ_T_STEPS = 3
N_ROW_BLOCKS = 4
COL_POOL, COL_K, COL_GPOOL, COL_GATTN = range(N_ROW_BLOCKS)
N_T_BLOCKS = 2
T_Q, T_V = range(N_T_BLOCKS)
HEADS_PER_STEP = LANES // HEAD_DIM
BIAS_PIECES = 3
ONES_LANE = BIAS_PIECES * N_HEADS
REF_ROW = LANES + ONES_LANE
PACKED_ROWS = 16
NORM_SLACK = 1.01
BIAS_SLACK = 0.05
REFERENCE_HEADROOM = 32.0
DENOM_FLOOR = 2.0 ** -80
VMEM_LIMIT_BYTES = 56 * 1024 * 1024

F32 = jnp.float32
BF16 = jnp.bfloat16


def _rms_norm(x, gain):
    return x * lax.rsqrt(jnp.mean(x * x, axis=-1, keepdims=True) + RMS_EPS) * gain


def _params(*semantics):
    return pltpu.CompilerParams(dimension_semantics=semantics, vmem_limit_bytes=VMEM_LIMIT_BYTES)


def _split_bf16(x):
    hi = x.astype(BF16)
    rest = x - hi.astype(F32)
    mid = rest.astype(BF16)
    lo = (rest - mid.astype(F32)).astype(BF16)
    return hi, mid, lo


def _norm_kernel(h_ref, gain_ref, out_ref):
    out_ref[...] = _rms_norm(h_ref[...], gain_ref[...]).astype(out_ref.dtype)


def _norm_call(h_res, gain):
    rows = h_res.shape[0]
    return pl.pallas_call(
        _norm_kernel,
        grid=(rows // ROW_TILE,),
        in_specs=[pl.BlockSpec((ROW_TILE, D_MODEL), lambda i: (i, 0)),
                  pl.BlockSpec((1, D_MODEL), lambda i: (0, 0))],
        out_specs=pl.BlockSpec((ROW_TILE, D_MODEL), lambda i: (i, 0)),
        out_shape=jax.ShapeDtypeStruct((rows, D_MODEL), BF16),
        compiler_params=_params("parallel"),
        name="pre_norm",
    )(h_res, gain)


def _proj_kernel(h_ref, w_ref, out_ref):
    out_ref[...] = jnp.dot(h_ref[...], w_ref[...], preferred_element_type=F32).astype(out_ref.dtype)


def _proj_call(h_norm, w_rows):
    rows = h_norm.shape[0]
    tile = next(t for t in PROJ_ROW_TILES if rows % t == 0)
    return pl.pallas_call(
        _proj_kernel,
        grid=(rows // tile, N_ROW_BLOCKS),
        in_specs=[pl.BlockSpec((tile, D_MODEL), lambda i, j: (i, 0)),
                  pl.BlockSpec((D_MODEL, D_MODEL), lambda i, j: (0, j))],
        out_specs=pl.BlockSpec((tile, D_MODEL), lambda i, j: (i, j)),
        out_shape=jax.ShapeDtypeStruct((rows, N_ROW_BLOCKS * D_MODEL), BF16),
        compiler_params=_params("parallel", "arbitrary"),
        name="in_proj",
    )(h_norm, w_rows)


def _proj_t_kernel(h_ref, wt_ref, out_ref):
    acc = lax.dot_general(wt_ref[...], h_ref[...], (((1,), (1,)), ((), ())),
                          preferred_element_type=F32)
    scale = jnp.where(pl.program_id(2) == T_Q, LOG2_E * HEAD_DIM ** -0.5, 1.0).astype(F32)
    out_ref[...] = (acc * scale).astype(out_ref.dtype)


def _proj_t_call(h_norm3, w_t):
    batch, length, _ = h_norm3.shape
    tile = length // PROJ_T_STEPS
    assert tile * PROJ_T_STEPS == length and tile % LANES == 0
    return pl.pallas_call(
        _proj_t_kernel,
        grid=(batch, PROJ_T_STEPS, N_T_BLOCKS),
        in_specs=[pl.BlockSpec((None, tile, D_MODEL), lambda b, i, j: (b, i, 0)),
                  pl.BlockSpec((D_MODEL, D_MODEL), lambda b, i, j: (j, 0))],
        out_specs=pl.BlockSpec((None, D_MODEL, tile), lambda b, i, j: (b, j, i)),
        out_shape=jax.ShapeDtypeStruct((batch, N_T_BLOCKS * D_MODEL, length), BF16),
        compiler_params=_params("parallel", "parallel", "arbitrary"),
        name="in_proj_t",
    )(h_norm3, w_t)


def _forget_kernel(h_ref, wf_ref, bias_ref, out_ref, end_ref, t_ref, carry_ref):
    @pl.when(pl.program_id(1) == 0)
    def _():
        carry_ref[...] = jnp.zeros_like(carry_ref)

    logit = jnp.dot(h_ref[...], wf_ref[...], preferred_element_type=F32) + bias_ref[...]
    log_f = jnp.minimum(logit, 0.0) - jnp.log(1.0 + jnp.exp(-jnp.abs(logit)))
    tile = log_f.shape[0]
    dst = lax.broadcasted_iota(jnp.int32, (MXU_DIM, MXU_DIM), 0)
    src = lax.broadcasted_iota(jnp.int32, (MXU_DIM, MXU_DIM), 1)
    prefix = (src <= dst).astype(BF16)
    pieces = _split_bf16(log_f)
    carry = carry_ref[...]
    blocks = []
    for r0 in range(0, tile, MXU_DIM):
        block = jnp.broadcast_to(carry, (MXU_DIM, LANES))
        for piece in pieces:
            block = block + jnp.dot(prefix, piece[r0:r0 + MXU_DIM, :], preferred_element_type=F32)
        carry = block[MXU_DIM - 1:MXU_DIM, :]
        blocks.append(block)
    total = jnp.concatenate(blocks, axis=0)
    carry_ref[...] = carry
    bias = -LOG2_E * total
    end_ref[...] = jnp.broadcast_to(bias[tile - 1:tile, :], end_ref.shape)
    hi, mid, lo = _split_bf16(bias)
    lane = lax.broadcasted_iota(jnp.int32, total.shape, 1)
    piece = lane % BIAS_PIECES
    out = jnp.where(piece == 0, hi, jnp.where(piece == 1, mid, lo))
    tail = jnp.where(lane < ONES_LANE + BIAS_PIECES, 1.0, 0.0).astype(BF16)
    out_ref[...] = jnp.where(lane < ONES_LANE, out, tail)
    head = lax.broadcasted_iota(jnp.int32, (N_HEADS, LANES), 0)
    lane_h = lax.broadcasted_iota(jnp.int32, (N_HEADS, LANES), 1)
    pick = (lane_h >= head * BIAS_PIECES) & (lane_h < (head + 1) * BIAS_PIECES)
    t_ref[...] = lax.dot_general(jnp.where(pick, 1.0, 0.0).astype(BF16), out,
                                 (((1,), (1,)), ((), ())), preferred_element_type=F32)


def _forget_call(h_norm3, wf_rep, bias_rep):
    batch, length, _ = h_norm3.shape
    n_tiles = length // ROW_TILE
    return pl.pallas_call(
        _forget_kernel,
        grid=(batch, n_tiles),
        in_specs=[pl.BlockSpec((None, ROW_TILE, D_MODEL), lambda b, i: (b, i, 0)),
                  pl.BlockSpec((D_MODEL, LANES), lambda b, i: (0, 0)),
                  pl.BlockSpec((1, LANES), lambda b, i: (0, 0))],
        out_specs=[pl.BlockSpec((None, ROW_TILE, LANES), lambda b, i: (b, i, 0)),
                   pl.BlockSpec((None, None, SUBLANES, LANES), lambda b, i: (b, i, 0, 0)),
                   pl.BlockSpec((None, N_HEADS, ROW_TILE), lambda b, i: (b, 0, i))],
        out_shape=[jax.ShapeDtypeStruct((batch, length, LANES), BF16),
                   jax.ShapeDtypeStruct((batch, n_tiles, SUBLANES, LANES), F32),
                   jax.ShapeDtypeStruct((batch, N_HEADS, length), F32)],
        scratch_shapes=[pltpu.VMEM((1, LANES), F32)],
        compiler_params=_params("parallel", "arbitrary"),
        name="forget_bias",
    )(h_norm3, wf_rep, bias_rep)


def _key_norm_kernel(k_ref, head_ref, out_ref):
    k = k_ref[...].astype(F32)
    sums = jnp.dot((k * k).astype(BF16), head_ref[...], preferred_element_type=F32)
    out_ref[...] = jnp.broadcast_to(jnp.sqrt(jnp.max(sums, axis=0, keepdims=True)), out_ref.shape)


def _key_norm_call(proj3, head_indicator):
    batch, length, _ = proj3.shape
    n_tiles = length // ROW_TILE
    return pl.pallas_call(
        _key_norm_kernel,
        grid=(batch, n_tiles),
        in_specs=[pl.BlockSpec((None, ROW_TILE, D_MODEL), lambda b, i: (b, i, COL_K)),
                  pl.BlockSpec((D_MODEL, LANES), lambda b, i: (0, 0))],
        out_specs=pl.BlockSpec((None, None, SUBLANES, LANES), lambda b, i: (b, i, 0, 0)),
        out_shape=jax.ShapeDtypeStruct((batch, n_tiles, SUBLANES, LANES), F32),
        compiler_params=_params("parallel", "parallel"),
        name="key_norms",
    )(proj3, head_indicator)


def _attn_kernel(kmax_ref, gend_ref, gt_ref, qt_ref, k_ref, fb_ref, vt_ref, out_ref, *scratch):
    n_tiles = k_ref.shape[0] // ROW_TILE
    qaug_ref = scratch[1]
    feat = lax.broadcasted_iota(jnp.int32, (LANES, ROW_TILE), 0)
    for hh in range(HEADS_PER_STEP):
        head = pl.program_id(1) * HEADS_PER_STEP + hh
        is_piece = (feat >= head * BIAS_PIECES) & (feat < (head + 1) * BIAS_PIECES)
        qaug_ref[LANES:2 * LANES, hh * ROW_TILE:(hh + 1) * ROW_TILE] = jnp.where(
            is_piece, 1.0, 0.0).astype(BF16)

    def one_tile(qi, carry):
        _attn_query_tile(qi, n_tiles, kmax_ref, gend_ref, gt_ref, qt_ref, k_ref, fb_ref, vt_ref,
                         out_ref, *scratch)
        return carry

    lax.fori_loop(0, n_tiles, one_tile, 0)


def _attn_query_tile(qi, n_tiles, kmax_ref, gend_ref, gt_ref, qt_ref, k_ref, fb_ref, vt_ref,
                     out_ref, acc_ref, qaug_ref, *chunk_refs):
    bi = pl.program_id(0)
    pair = pl.program_id(1)
    tile = ROW_TILE
    q_at = pl.ds(pl.multiple_of(qi * tile, LANES), tile)
    q_tile = qt_ref[:, q_at]
    both = HEADS_PER_STEP * tile
    n_chunks = both // CHUNK_COLS
    s_refs, p_refs = chunk_refs[:n_chunks], chunk_refs[n_chunks:]

    feat = lax.broadcasted_iota(jnp.int32, (LANES, tile), 0)
    q_norms = []
    for hh in range(HEADS_PER_STEP):
        head = pair * HEADS_PER_STEP + hh
        in_head = (feat >= hh * HEAD_DIM) & (feat < (hh + 1) * HEAD_DIM)
        q_top = jnp.where(in_head, q_tile, jnp.zeros_like(q_tile))
        qaug_ref[0:LANES, hh * tile:(hh + 1) * tile] = q_top
        q_head = q_tile[hh * HEAD_DIM:(hh + 1) * HEAD_DIM, :].astype(F32)
        q_norms.append(jnp.sqrt(jnp.sum(q_head * q_head, axis=0, keepdims=True)))
    q_norm = jnp.concatenate(q_norms, axis=1)
    second_head = lax.broadcasted_iota(jnp.int32, (1, both), 1) >= tile
    g_diagonal = jnp.concatenate([gt_ref[hh:hh + 1, q_at] for hh in range(HEADS_PER_STEP)], axis=1)

    def head_segments(c):
        segments = []
        start = c * CHUNK_COLS
        while start < (c + 1) * CHUNK_COLS:
            hh = start // tile
            stop = min((c + 1) * CHUNK_COLS, (hh + 1) * tile)
            segments.append((hh, slice(start - hh * tile, stop - hh * tile),
                             slice(start - c * CHUNK_COLS, stop - c * CHUNK_COLS)))
            start = stop
        return segments

    def scores(c, k_start, on_diagonal):
        keys = jnp.concatenate([k_ref[pl.ds(k_start, tile), :],
                                fb_ref[pl.ds(k_start, tile), :]], axis=1)
        s_t = jnp.dot(keys, qaug_ref[:, c * CHUNK_COLS:(c + 1) * CHUNK_COLS],
                      preferred_element_type=F32)
        if on_diagonal:
            key_row = lax.broadcasted_iota(jnp.int32, s_t.shape, 0)
            col = lax.broadcasted_iota(jnp.int32, s_t.shape, 1) + c * CHUNK_COLS
            query_col = jnp.where(col >= tile, col - tile, col)
            s_t = jnp.where(key_row <= query_col, s_t, MASK_VALUE)
        return s_t

    def value_stage(c, k_start, alpha):
        for hh, head_cols, chunk_cols in head_segments(c):
            v_aug = jnp.concatenate(
                [vt_ref[hh * HEAD_DIM:(hh + 1) * HEAD_DIM, pl.ds(k_start, tile)],
                 jnp.ones((PACKED_ROWS, tile), BF16)], axis=0)
            acc_ref[hh, :, head_cols] = (
                alpha[:, c * CHUNK_COLS:(c + 1) * CHUNK_COLS][:, chunk_cols]
                * acc_ref[hh, :, head_cols]
                + jnp.dot(v_aug, p_refs[c][:, chunk_cols], preferred_element_type=F32))

    def set_reference(r):
        hi, mid, lo = (piece.astype(F32) for piece in _split_bf16(-r))
        row = lax.broadcasted_iota(jnp.int32, (PACKED_ROWS, both), 0)
        pieces = jnp.where(row == 0, hi, jnp.where(row == 1, mid, jnp.where(row == 2, lo, 0.0)))
        qaug_ref[REF_ROW:REF_ROW + PACKED_ROWS, :] = pieces.astype(BF16)

    def bound_tile(kj, r_prev, on_diagonal):
        k_start = pl.multiple_of(kj * tile, LANES)
        base = (bi * n_tiles + kj) * N_HEADS + pair * HEADS_PER_STEP
        k_max = jnp.where(second_head, kmax_ref[base + 1], kmax_ref[base])
        if on_diagonal:
            g_max = g_diagonal
        else:
            g_max = jnp.where(second_head, gend_ref[base + 1], gend_ref[base])
        bound = q_norm * k_max * NORM_SLACK + g_max + BIAS_SLACK
        r_new = jnp.maximum(r_prev, bound - REFERENCE_HEADROOM)
        alpha = jnp.exp2(r_prev - r_new)
        set_reference(r_new)

        if on_diagonal:
            diagonal_groups(k_start, alpha)
            return r_new

        def probabilities(c):
            p_refs[c][...] = jnp.exp2(scores(c, k_start, False)).astype(BF16)

        probabilities(0)
        probabilities(1)
        value_stage(0, k_start, alpha)
        probabilities(2)
        value_stage(1, k_start, alpha)
        value_stage(2, k_start, alpha)
        return r_new

    def diagonal_groups(k_start, alpha):
        n_groups = tile // MXU_DIM
        assert n_groups == n_chunks

        def group_cols(hh, g):
            return slice(hh * tile + g * MXU_DIM, hh * tile + (g + 1) * MXU_DIM)

        def probabilities(g):
            n_keys = (g + 1) * MXU_DIM
            keys = jnp.concatenate([k_ref[pl.ds(k_start, n_keys), :],
                                    fb_ref[pl.ds(k_start, n_keys), :]], axis=1)
            queries = jnp.concatenate(
                [qaug_ref[:, group_cols(hh, g)] for hh in range(HEADS_PER_STEP)], axis=1)
            s_t = jnp.dot(keys, queries, preferred_element_type=F32)
            key_row = lax.broadcasted_iota(jnp.int32, s_t.shape, 0)
            col = lax.broadcasted_iota(jnp.int32, s_t.shape, 1)
            query_col = jnp.where(col >= MXU_DIM, col - MXU_DIM, col) + g * MXU_DIM
            s_t = jnp.where(key_row <= query_col, s_t, MASK_VALUE)
            p_refs[g][0:n_keys, :] = jnp.exp2(s_t).astype(BF16)

        def values(g):
            n_keys = (g + 1) * MXU_DIM
            for hh in range(HEADS_PER_STEP):
                head_cols = slice(g * MXU_DIM, (g + 1) * MXU_DIM)
                v_aug = jnp.concatenate(
                    [vt_ref[hh * HEAD_DIM:(hh + 1) * HEAD_DIM, pl.ds(k_start, n_keys)],
                     jnp.ones((PACKED_ROWS, n_keys), BF16)], axis=0)
                acc_ref[hh, :, head_cols] = (
                    alpha[:, group_cols(hh, g)] * acc_ref[hh, :, head_cols]
                    + jnp.dot(v_aug, p_refs[g][0:n_keys, hh * MXU_DIM:(hh + 1) * MXU_DIM],
                              preferred_element_type=F32))

        probabilities(2)
        probabilities(1)
        values(2)
        probabilities(0)
        values(1)
        values(0)

    assert n_chunks == 3
    acc_ref[...] = jnp.zeros_like(acc_ref)
    r_run = jnp.full((1, both), MASK_VALUE, F32)
    def tiles(first, count, r):
        for t in range(count):
            r = bound_tile(first + t, r, False)
        return r

    r_run = lax.fori_loop(0, qi // TILES_PER_TRIP,
                          lambda kk, r: tiles(kk * TILES_PER_TRIP, TILES_PER_TRIP, r), r_run)
    done = qi // TILES_PER_TRIP * TILES_PER_TRIP
    count = TILES_PER_TRIP // 2
    while count:
        has = ((qi - done) & count) != 0
        r_run = lax.cond(has, lambda r, at=done, n=count: tiles(at, n, r), lambda r: r, r_run)
        done = done + jnp.where(has, count, 0)
        count //= 2
    bound_tile(qi, r_run, True)

    def write_output():
        heads_out = [acc_ref[hh, :HEAD_DIM, :] / acc_ref[hh, HEAD_DIM:HEAD_DIM + 1, :]
                     for hh in range(HEADS_PER_STEP)]
        out_ref[q_at, :] = jnp.concatenate(heads_out, axis=0).T.astype(out_ref.dtype)

    write_output()
    smallest_denominator = jnp.min(jnp.minimum(acc_ref[0, HEAD_DIM:HEAD_DIM + 1, :],
                                               acc_ref[1, HEAD_DIM:HEAD_DIM + 1, :]))

    @pl.when(jnp.logical_not(smallest_denominator > DENOM_FLOOR))
    def _():
        def max_tile(kj, m_prev, on_diagonal):
            k_start = pl.multiple_of(kj * tile, LANES)
            m_tile = []
            for c in range(n_chunks):
                s_t = scores(c, k_start, on_diagonal)
                s_refs[c][...] = s_t
                m_tile.append(jnp.max(s_t, axis=0, keepdims=True))
            m_new = jnp.maximum(m_prev, jnp.concatenate(m_tile, axis=1))
            alpha = jnp.exp2(m_prev - m_new)
            for c in range(n_chunks):
                cols = slice(c * CHUNK_COLS, (c + 1) * CHUNK_COLS)
                p_refs[c][...] = jnp.exp2(s_refs[c][...] - m_new[:, cols]).astype(BF16)
                value_stage(c, k_start, alpha)
            return m_new

        acc_ref[...] = jnp.zeros_like(acc_ref)
        set_reference(jnp.zeros((1, both), F32))
        m_run = jnp.full((1, both), MASK_VALUE, F32)
        m_run = lax.fori_loop(0, qi, lambda kj, m: max_tile(kj, m, False), m_run)
        max_tile(qi, m_run, True)
        write_output()


def _attn_call(proj3, proj_t, key_bias, bias_t, key_norm_max, bias_max):
    batch, length, _ = proj3.shape
    pairs = N_HEADS // HEADS_PER_STEP
    bias_t = bias_t.reshape(batch, pairs, HEADS_PER_STEP, length)
    blocks_per_section = D_MODEL // LANES
    n_chunks = HEADS_PER_STEP * ROW_TILE // CHUNK_COLS
    return pl.pallas_call(
        _attn_kernel,
        grid=(batch, pairs),
        in_specs=[
            pl.BlockSpec(memory_space=pltpu.SMEM),
            pl.BlockSpec(memory_space=pltpu.SMEM),
            pl.BlockSpec((None, None, HEADS_PER_STEP, length), lambda b, p: (b, p, 0, 0)),
            pl.BlockSpec((None, LANES, length), lambda b, p: (b, T_Q * blocks_per_section + p, 0)),
            pl.BlockSpec((None, length, LANES), lambda b, p: (b, 0, COL_K * blocks_per_section + p)),
            pl.BlockSpec((None, length, LANES), lambda b, p: (b, 0, 0)),
            pl.BlockSpec((None, LANES, length), lambda b, p: (b, T_V * blocks_per_section + p, 0)),
        ],
        out_specs=pl.BlockSpec((None, length, LANES), lambda b, p: (b, 0, p)),
        out_shape=jax.ShapeDtypeStruct((batch, length, D_MODEL), BF16),
        scratch_shapes=[
            pltpu.VMEM((HEADS_PER_STEP, HEAD_DIM + PACKED_ROWS, ROW_TILE), F32),
            pltpu.VMEM((2 * LANES, HEADS_PER_STEP * ROW_TILE), BF16),
        ] + [pltpu.VMEM((ROW_TILE, CHUNK_COLS), dtype) for dtype in (F32, BF16)
             for _ in range(n_chunks)],
        compiler_params=_params("parallel", "parallel"),
        name="forgetting_attention",
    )(key_norm_max, bias_max, bias_t, proj_t, proj3, key_bias, proj_t)


def _pool_band():
    row = jnp.arange(MIX_BLOCK)[:, None]
    col = jnp.arange(MIX_BLOCK)[None, :]
    bands = [((col <= row) & (col > row - w)).astype(F32) / w - (col == row).astype(F32)
             for w in POOL_WINDOWS]
    return jnp.stack(bands).astype(BF16)


def _mix_ffn_kernel(u_ref, halo_ref, gp_ref, ga_ref, ya_ref, h_ref, band_ref, wp_ref, ps_ref,
                    wo_ref, gain_mix_ref, gain_pre_ref, wg_ref, wu_ref, wd_ref, gain_ffn_ref,
                    gain_next_ref, hres_out, hnorm_out):
    li = pl.program_id(1)

    def mix_block(r0):
        rows = slice(r0, r0 + MIX_BLOCK)
        head = u_ref[r0:r0 + POOL_HALO, :].astype(F32)
        if r0 == 0:
            halo = jnp.where(li > 0, halo_ref[...].astype(F32), 0.0)
        else:
            halo = u_ref[r0 - POOL_HALO:r0, :].astype(F32)
        ext = jnp.concatenate([halo, head], axis=0)
        pos1 = (li * u_ref.shape[0] + r0 + 1
                + lax.broadcasted_iota(jnp.int32, (POOL_HALO, 1), 0))
        pooled = []
        for g, window in enumerate(POOL_WINDOWS):
            lanes = slice(g * POOL_GROUP_DIM, (g + 1) * POOL_GROUP_DIM)
            run = ext[:, lanes]
            width = 1
            while width < window:
                run = run[width:] + run[:-width]
                width *= 2
            first = POOL_HALO + 1 - window
            count = jnp.minimum(pos1, window).astype(F32)
            diff_head = run[first:first + POOL_HALO] / count - head[:, lanes]
            diff = jnp.dot(band_ref[g], u_ref[rows, lanes], preferred_element_type=F32)
            diff = jnp.concatenate([diff_head, diff[POOL_HALO:]], axis=0)
            pooled.append(jnp.dot(diff.astype(BF16), wp_ref[g], preferred_element_type=F32))
        y_pool = jnp.concatenate(pooled, axis=1) * ps_ref[...]
        merged = (jax.nn.sigmoid(gp_ref[rows, :].astype(F32)) * y_pool
                  + jax.nn.sigmoid(ga_ref[rows, :].astype(F32)) * ya_ref[rows, :].astype(F32))
        mix = jnp.dot(merged.astype(BF16), wo_ref[...], preferred_element_type=F32)
        return h_ref[rows, :] + _rms_norm(mix, gain_mix_ref[...])

    def ffn_block(r0, h_mid):
        rows = slice(r0, r0 + MIX_BLOCK)
        hn = _rms_norm(h_mid, gain_pre_ref[...]).astype(BF16)
        ff_out = jnp.zeros(h_mid.shape, F32)
        for c in range(D_FF // FF_CHUNK):
            cols = slice(c * FF_CHUNK, (c + 1) * FF_CHUNK)
            gate = jnp.dot(hn, wg_ref[:, cols], preferred_element_type=F32)
            up = jnp.dot(hn, wu_ref[:, cols], preferred_element_type=F32)
            ff = gate * jax.nn.sigmoid(gate) * up
            ff_out = ff_out + jnp.dot(ff.astype(BF16), wd_ref[cols, :], preferred_element_type=F32)
        h_new = h_mid + _rms_norm(ff_out, gain_ffn_ref[...])
        hres_out[rows, :] = h_new
        hnorm_out[rows, :] = _rms_norm(h_new, gain_next_ref[...]).astype(hnorm_out.dtype)

    n_blocks = u_ref.shape[0] // MIX_BLOCK
    pending = {}
    for step in range(n_blocks + MIX_LEAD):
        if step < n_blocks:
            pending[step] = mix_block(step * MIX_BLOCK)
        if step >= MIX_LEAD:
            ffn_block((step - MIX_LEAD) * MIX_BLOCK, pending.pop(step - MIX_LEAD))


def _mix_ffn_call(proj3, y_attn, h_res3, band, w_pool, pool_scale, w_out, gain_mix, gain_pre,
                  w_gate, w_up, w_down, gain_ffn, gain_next):
    batch, length, _ = proj3.shape
    halo_blocks = FUSED_ROW_TILE // POOL_HALO
    row_spec = lambda col: pl.BlockSpec((None, FUSED_ROW_TILE, D_MODEL), lambda b, i: (b, i, col))
    const = lambda shape: pl.BlockSpec(shape, lambda b, i: (0,) * len(shape),
                                       pipeline_mode=pl.Buffered(1))
    return pl.pallas_call(
        _mix_ffn_kernel,
        grid=(batch, length // FUSED_ROW_TILE),
        in_specs=[
            row_spec(COL_POOL),
            pl.BlockSpec((None, POOL_HALO, D_MODEL),
                         lambda b, i: (b, jnp.maximum(i * halo_blocks - 1, 0), COL_POOL)),
            row_spec(COL_GPOOL),
            row_spec(COL_GATTN),
            row_spec(0),
            row_spec(0),
            const((len(POOL_WINDOWS), MIX_BLOCK, MIX_BLOCK)),
            const((len(POOL_WINDOWS), POOL_GROUP_DIM, POOL_GROUP_DIM)),
            const((1, D_MODEL)),
            const((D_MODEL, D_MODEL)),
            const((1, D_MODEL)),
            const((1, D_MODEL)),
            const((D_MODEL, D_FF)), const((D_MODEL, D_FF)), const((D_FF, D_MODEL)),
            const((1, D_MODEL)),
            const((1, D_MODEL)),
        ],
        out_specs=[row_spec(0), row_spec(0)],
        out_shape=[jax.ShapeDtypeStruct((batch, length, D_MODEL), F32),
                   jax.ShapeDtypeStruct((batch, length, D_MODEL), BF16)],
        compiler_params=_params("parallel", "parallel"),
        name="mix_ffn",
    )(proj3, proj3, proj3, proj3, y_attn, h_res3, band, w_pool, pool_scale, w_out, gain_mix,
      gain_pre, w_gate, w_up, w_down, gain_ffn, gain_next)


def kernel(x, meta_tokens, norm_mix_pre, norm_mix_post, norm_ffn_pre, norm_ffn_post,
           w_in, b_forget, w_pool, pool_scale, w_out, w_ffn_gate, w_ffn_up, w_ffn_down):
    batch, seq, _ = x.shape
    depth = w_in.shape[0]
    tokens = META_TOKENS + seq
    length = -(-tokens // ROW_TILE) * ROW_TILE
    meta = jnp.broadcast_to(meta_tokens[None].astype(x.dtype), (batch, META_TOKENS, D_MODEL))
    tail = jnp.zeros((batch, length - tokens, D_MODEL), x.dtype)
    h_res = jnp.concatenate([meta, x, tail], axis=1).reshape(batch * length, D_MODEL)
    gain = lambda g: g.reshape(1, D_MODEL).astype(F32)
    sec = lambda w, n: w[:, n * D_MODEL:(n + 1) * D_MODEL]
    gates_at = 4 * D_MODEL + N_HEADS
    lane_head = jnp.minimum(jnp.arange(LANES) // BIAS_PIECES, N_HEADS - 1)
    head_indicator = (jnp.arange(D_MODEL)[:, None] // HEAD_DIM == jnp.arange(LANES)[None, :]).astype(BF16)

    pool_band = _pool_band()

    h_norm = _norm_call(h_res, gain(norm_mix_pre[0]))
    for l in range(depth):
        w = w_in[l]
        w_rows = jnp.concatenate([sec(w, 0), sec(w, 2), w[:, gates_at:]], axis=1).astype(BF16)
        w_t = jnp.concatenate([sec(w, 1), sec(w, 3)], axis=1).T.astype(BF16)
        w_forget = w[:, 4 * D_MODEL:gates_at]
        wf_rep = w_forget[:, lane_head].astype(BF16)
        bias_rep = b_forget[l][lane_head].reshape(1, LANES).astype(F32)

        h_norm3 = h_norm.reshape(batch, length, D_MODEL)
        proj3 = _proj_call(h_norm, w_rows).reshape(batch, length, N_ROW_BLOCKS * D_MODEL)
        proj_t = _proj_t_call(h_norm3, w_t)
        key_bias, bias_end, bias_t = _forget_call(h_norm3, wf_rep, bias_rep)
        key_norms = _key_norm_call(proj3, head_indicator)
        key_norm_max = key_norms[:, :, 0, :N_HEADS].reshape(-1)
        bias_max = bias_end[:, :, 0, 0:ONES_LANE:BIAS_PIECES].reshape(-1)
        y_attn = _attn_call(proj3, proj_t, key_bias, bias_t, key_norm_max, bias_max)
        next_gain = norm_mix_pre[(l + 1) % depth]
        h_res3, h_norm3 = _mix_ffn_call(
            proj3, y_attn, h_res.reshape(batch, length, D_MODEL), pool_band,
            w_pool[l].astype(BF16), gain(pool_scale[l]), w_out[l].astype(BF16),
            gain(norm_mix_post[l]), gain(norm_ffn_pre[l]),
            w_ffn_gate[l].astype(BF16), w_ffn_up[l].astype(BF16), w_ffn_down[l].astype(BF16),
            gain(norm_ffn_post[l]), gain(next_gain))
        h_res = h_res3.reshape(batch * length, D_MODEL)
        h_norm = h_norm3.reshape(batch * length, D_MODEL)
    return h_res.reshape(batch, length, D_MODEL)[:, META_TOKENS:tokens]
```
